```python
import math
import jax, jax.numpy as jnp
from jax import lax
import numpy as np

D_MODEL = 1024
BATCH = 4
SEQ = 8192
DEPTH = 2
DEC_BATCH = 128
DEC_SEQ = 4
PAST_LEN = 16384
PAGE_SIZE = 128

HEAD_DIM = 64
N_Q_HEADS = 8
N_KV_HEADS = 2
Q_PER_KV = N_Q_HEADS // N_KV_HEADS
ATTN_WIDTH = N_Q_HEADS * HEAD_DIM
KV_WIDTH = N_KV_HEADS * HEAD_DIM
WINDOW = 128
ROPE_THETA = 10000.0
ATTN_SCALE = HEAD_DIM ** -0.5

SSM_HEAD_DIM = 64
N_SSM_HEADS = 16
SSM_WIDTH = N_SSM_HEADS * SSM_HEAD_DIM
N_SSM_GROUPS = 2
HEADS_PER_GROUP = N_SSM_HEADS // N_SSM_GROUPS
D_STATE = 128
CONV_W = 4
CONV_DIM = SSM_WIDTH + 2 * N_SSM_GROUPS * D_STATE
SSD_CHUNK = 128

MIX_WIDTH = ATTN_WIDTH + SSM_WIDTH
SPLITS = [ATTN_WIDTH,
          ATTN_WIDTH + KV_WIDTH,
          ATTN_WIDTH + 2 * KV_WIDTH,
          ATTN_WIDTH + 2 * KV_WIDTH + SSM_WIDTH,
          ATTN_WIDTH + 2 * KV_WIDTH + SSM_WIDTH + CONV_DIM]
IN_PROJ_WIDTH = ATTN_WIDTH + 2 * KV_WIDTH + SSM_WIDTH + CONV_DIM + N_SSM_HEADS

D_FF = -(-8 * D_MODEL // (3 * 256)) * 256
EPS = 1e-6

kernel_name = "hymba_swa_sink_ssd_step"


def rms_norm(x, g):
    xf = x.astype(jnp.float32)
    y = xf * lax.rsqrt(jnp.mean(xf * xf, axis=-1, keepdims=True) + EPS)
    return (y * g.astype(jnp.float32)).astype(x.dtype)


def rope(x, pos):
    half = HEAD_DIM // 2
    inv = ROPE_THETA ** (-jnp.arange(half, dtype=jnp.float32) / half)
    ang = pos.astype(jnp.float32)[:, None] * inv[None, :]
    cos = jnp.cos(ang)[None, :, None, :]
    sin = jnp.sin(ang)[None, :, None, :]
    xf = x.astype(jnp.float32)
    x1, x2 = xf[..., :half], xf[..., half:]
    return jnp.concatenate([x1 * cos - x2 * sin, x2 * cos + x1 * sin], axis=-1).astype(x.dtype)


def sink_softmax(s, sinks):
    sk = sinks.astype(jnp.float32).reshape(N_KV_HEADS, Q_PER_KV)[:, :, None, None]
    m = jnp.maximum(jnp.max(s, axis=-1, keepdims=True), sk)
    p = jnp.exp(s - m)
    return p / (jnp.sum(p, axis=-1, keepdims=True) + jnp.exp(sk - m))


def banded_window_attention(q, k, v, sinks):
    b, S = q.shape[:2]
    nb = S // WINDOW
    qb = q.reshape(b, nb, WINDOW, N_KV_HEADS, Q_PER_KV, HEAD_DIM)
    kb = k.reshape(b, nb, WINDOW, N_KV_HEADS, HEAD_DIM)
    vb = v.reshape(b, nb, WINDOW, N_KV_HEADS, HEAD_DIM)

    def with_prev(t):
        prev = jnp.pad(t[:, :-1], ((0, 0), (1, 0), (0, 0), (0, 0), (0, 0)))
        return jnp.concatenate([prev, t], axis=2)

    kk, vv = with_prev(kb), with_prev(vb)
    s = jnp.einsum('bnqgrd,bnkgd->bngrqk', qb, kk,
                   preferred_element_type=jnp.float32) * ATTN_SCALE
    qi = jnp.arange(WINDOW)[:, None]
    kj = jnp.arange(2 * WINDOW)[None, :]
    diff = qi + WINDOW - kj
    band = (diff >= 0) & (diff < WINDOW)
    mask = band[None] & ((jnp.arange(nb)[:, None, None] > 0) | (kj[None] >= WINDOW))
    s = jnp.where(mask[:, None, None], s, -jnp.inf)
    p = sink_softmax(s, sinks)
    o = jnp.einsum('bngrqk,bnkgd->bnqgrd', p.astype(v.dtype), vv)
    return o.reshape(b, S, ATTN_WIDTH)


def cached_window_attention(q, k, v, k_cache, v_cache, sinks, pos):
    b, T = q.shape[:2]
    C = k_cache.shape[1]
    kk = jnp.concatenate([k_cache.astype(k.dtype), k], axis=1)
    vv = jnp.concatenate([v_cache.astype(v.dtype), v], axis=1)
    qg = q.reshape(b, T, N_KV_HEADS, Q_PER_KV, HEAD_DIM)
    s = jnp.einsum('btgrd,bkgd->bgrtk', qg, kk,
                   preferred_element_type=jnp.float32) * ATTN_SCALE
    k_pos = jnp.concatenate([pos[0] - C + jnp.arange(C, dtype=jnp.int32), pos])
    diff = pos[:, None] - k_pos[None, :]
    mask = (diff >= 0) & (diff < WINDOW)
    s = jnp.where(mask, s, -jnp.inf)
    p = sink_softmax(s, sinks)
    o = jnp.einsum('bgrtk,bkgd->btgrd', p.astype(v.dtype), vv)
    return o.reshape(b, T, ATTN_WIDTH), kk[:, T:], vv[:, T:]


def causal_conv(xbc, conv_state, w, bias):
    L = xbc.shape[1]
    xp = jnp.concatenate([conv_state.astype(xbc.dtype), xbc], axis=1)
    out = sum(xp[:, i:i + L] * w[i] for i in range(CONV_W)) + bias
    return jax.nn.silu(out), xp[:, L:]


def ssd_scan(x, dt, A, Bm, Cm, h0):
    b, L = x.shape[:2]
    Q = min(SSD_CHUNK, L)
    pad = (-L) % Q
    xf = x.astype(jnp.float32)
    Bf = Bm.astype(jnp.float32)
    Cf = Cm.astype(jnp.float32)
    if pad:
        xf = jnp.pad(xf, ((0, 0), (0, pad), (0, 0), (0, 0)))
        dt = jnp.pad(dt, ((0, 0), (0, pad), (0, 0)))
        Bf = jnp.pad(Bf, ((0, 0), (0, pad), (0, 0), (0, 0)))
        Cf = jnp.pad(Cf, ((0, 0), (0, pad), (0, 0), (0, 0)))
    c = (L + pad) // Q
    G, Hg, P, N = N_SSM_GROUPS, HEADS_PER_GROUP, SSM_HEAD_DIM, D_STATE
    xc = xf.reshape(b, c, Q, G, Hg, P)
    dtc = dt.reshape(b, c, Q, G, Hg)
    Bc = Bf.reshape(b, c, Q, G, N)
    Cc = Cf.reshape(b, c, Q, G, N)
    acum = jnp.cumsum(dtc * A.astype(jnp.float32).reshape(G, Hg), axis=2)
    acT = jnp.moveaxis(acum, 2, -1)
    seg = acT[..., :, None] - acT[..., None, :]
    causal = jnp.tril(jnp.ones((Q, Q), dtype=bool))
    decay = jnp.exp(jnp.where(causal, seg, -jnp.inf))
    cb = jnp.einsum('bcqgn,bcsgn->bcgqs', Cc, Bc)
    xdt = xc * dtc[..., None]
    y_intra = jnp.einsum('bcghqs,bcsghp->bcqghp', cb[:, :, :, None] * decay, xdt)
    decay_end = jnp.exp(acum[:, :, -1:] - acum)
    states = jnp.einsum('bcsgn,bcsghp->bcghpn', Bc, xdt * decay_end[..., None])
    chunk_decay = jnp.exp(acum[:, :, -1])

    def step(h, inp):
        st, dc = inp
        return h * dc[..., None, None] + st, h

    h_init = h0.astype(jnp.float32).reshape(b, G, Hg, P, N)
    h_final, h_starts = lax.scan(step, h_init,
                                 (jnp.moveaxis(states, 1, 0), jnp.moveaxis(chunk_decay, 1, 0)))
    h_starts = jnp.moveaxis(h_starts, 0, 1)
    y_inter = jnp.einsum('bcqgn,bcghpn->bcqghp', Cc, h_starts) * jnp.exp(acum)[..., None]
    y = (y_intra + y_inter).reshape(b, c * Q, N_SSM_HEADS, P)[:, :L]
    return y, h_final.reshape(b, N_SSM_HEADS, P, N)


def hybrid_layer(x, pos, win_kv, conv_state, ssm_state, p):
    b, L, _ = x.shape
    u = rms_norm(x, p['norm_mix'])
    proj = u @ p['w_in']
    q, k, v, z, xbc, dt_raw = jnp.split(proj, SPLITS, axis=-1)
    q = rope(rms_norm(q.reshape(b, L, N_Q_HEADS, HEAD_DIM), p['q_norm']), pos)
    k = rope(rms_norm(k.reshape(b, L, N_KV_HEADS, HEAD_DIM), p['k_norm']), pos)
    v = v.reshape(b, L, N_KV_HEADS, HEAD_DIM)
    if win_kv is None:
        o_attn = banded_window_attention(q, k, v, p['sinks'])
        n_keep = min(WINDOW, L)
        new_k, new_v = k[:, L - n_keep:], v[:, L - n_keep:]
    else:
        o_attn, new_k, new_v = cached_window_attention(q, k, v, win_kv[0], win_kv[1],
                                                       p['sinks'], pos)
    xbc_act, new_conv = causal_conv(xbc, conv_state, p['conv_w'], p['conv_b'])
    xs, Bm, Cm = jnp.split(xbc_act, [SSM_WIDTH, SSM_WIDTH + N_SSM_GROUPS * D_STATE], axis=-1)
    dt = jax.nn.softplus(dt_raw.astype(jnp.float32) + p['dt_bias'].astype(jnp.float32))
    A = -jnp.exp(p['a_log'].astype(jnp.float32))
    xs_h = xs.reshape(b, L, N_SSM_HEADS, SSM_HEAD_DIM)
    y, new_h = ssd_scan(xs_h, dt, A,
                        Bm.reshape(b, L, N_SSM_GROUPS, D_STATE),
                        Cm.reshape(b, L, N_SSM_GROUPS, D_STATE), ssm_state)
    y = y + p['d_skip'].astype(jnp.float32)[:, None] * xs_h.astype(jnp.float32)
    yg = (y.reshape(b, L, SSM_WIDTH) * jax.nn.silu(z.astype(jnp.float32))).reshape(
        b, L, N_SSM_GROUPS, SSM_WIDTH // N_SSM_GROUPS)
    yg = yg * lax.rsqrt(jnp.mean(yg * yg, axis=-1, keepdims=True) + EPS)
    y_ssm = (yg.reshape(b, L, SSM_WIDTH) * p['ssm_norm'].astype(jnp.float32)).astype(x.dtype)
    x = x + jnp.concatenate([o_attn, y_ssm], axis=-1) @ p['w_out']
    hn = rms_norm(x, p['norm_ffn'])
    g, up = jnp.split(hn @ p['w_gate_up'], [D_FF], axis=-1)
    x = x + (jax.nn.silu(g) * up) @ p['w_down']
    return x, new_k, new_v, new_conv, new_h.astype(ssm_state.dtype)


def setup_inputs(seed: int = 0) -> dict:
    key = jax.random.key(seed)
    ks = jax.random.split(key, 24)
    f32 = jnp.float32

    def nrm(k, shape, scale):
        return jax.random.normal(k, shape, f32) * scale

    n_win = min(WINDOW, PAST_LEN)
    dt0 = jnp.exp(jax.random.uniform(ks[10], (DEPTH, N_SSM_HEADS), f32)
                  * (math.log(0.1) - math.log(0.001)) + math.log(0.001))
    return {
        'x_prompt': nrm(ks[0], (BATCH, SEQ, D_MODEL), 1.0),
        'x_sample': nrm(ks[1], (DEC_BATCH, DEC_SEQ, D_MODEL), 1.0),
        'cache_win_k': nrm(ks[2], (DEPTH, DEC_BATCH, n_win, N_KV_HEADS, HEAD_DIM), 1.0),
        'cache_win_v': nrm(ks[3], (DEPTH, DEC_BATCH, n_win, N_KV_HEADS, HEAD_DIM), 1.0),
        'state_conv': nrm(ks[4], (DEPTH, DEC_BATCH, CONV_W - 1, CONV_DIM), 1.0),
        'state_ssm': nrm(ks[5], (DEPTH, DEC_BATCH, N_SSM_HEADS, SSM_HEAD_DIM, D_STATE), 0.5),
        'norm_mix': 1.0 + nrm(ks[6], (DEPTH, D_MODEL), 0.02),
        'w_in': nrm(ks[7], (DEPTH, D_MODEL, IN_PROJ_WIDTH), D_MODEL ** -0.5),
        'q_norm': 1.0 + nrm(ks[8], (DEPTH, HEAD_DIM), 0.02),
        'k_norm': 1.0 + nrm(ks[9], (DEPTH, HEAD_DIM), 0.02),
        'sinks': nrm(ks[11], (DEPTH, N_Q_HEADS), 0.5),
        'conv_w': nrm(ks[12], (DEPTH, CONV_W, CONV_DIM), CONV_W ** -0.5),
        'conv_b': nrm(ks[13], (DEPTH, CONV_DIM), 0.01),
        'dt_bias': dt0 + jnp.log(-jnp.expm1(-dt0)),
        'a_log': jnp.log(jax.random.uniform(ks[14], (DEPTH, N_SSM_HEADS), f32, 1.0, 16.0)),
        'd_skip': 1.0 + nrm(ks[15], (DEPTH, N_SSM_HEADS), 0.1),
        'ssm_norm': 1.0 + nrm(ks[16], (DEPTH, SSM_WIDTH), 0.02),
        'w_out': nrm(ks[17], (DEPTH, MIX_WIDTH, D_MODEL), MIX_WIDTH ** -0.5),
        'norm_ffn': 1.0 + nrm(ks[18], (DEPTH, D_MODEL), 0.02),
        'w_gate_up': nrm(ks[19], (DEPTH, D_MODEL, 2 * D_FF), D_MODEL ** -0.5),
        'w_down': nrm(ks[20], (DEPTH, D_FF, D_MODEL), D_FF ** -0.5),
    }


def reference(x_prompt, x_sample, cache_win_k, cache_win_v, state_conv, state_ssm,
              norm_mix, w_in, q_norm, k_norm, sinks, conv_w, conv_b, dt_bias, a_log,
              d_skip, ssm_norm, w_out, norm_ffn, w_gate_up, w_down):
    b_p = x_prompt.shape[0]
    pos_p = jnp.arange(x_prompt.shape[1], dtype=jnp.int32)
    pos_s = PAST_LEN + jnp.arange(x_sample.shape[1], dtype=jnp.int32)
    conv0 = jnp.zeros((b_p, CONV_W - 1, CONV_DIM), x_prompt.dtype)
    h0 = jnp.zeros((b_p, N_SSM_HEADS, SSM_HEAD_DIM, D_STATE), state_ssm.dtype)

    xp, xs = x_prompt, x_sample
    pk, pv, pc, ph = [], [], [], []
    sk, sv, sc, sh = [], [], [], []
    for l in range(DEPTH):
        p = {'norm_mix': norm_mix[l], 'w_in': w_in[l], 'q_norm': q_norm[l],
             'k_norm': k_norm[l], 'sinks': sinks[l], 'conv_w': conv_w[l],
             'conv_b': conv_b[l], 'dt_bias': dt_bias[l], 'a_log': a_log[l],
             'd_skip': d_skip[l], 'ssm_norm': ssm_norm[l], 'w_out': w_out[l],
             'norm_ffn': norm_ffn[l], 'w_gate_up': w_gate_up[l], 'w_down': w_down[l]}
        xp, k1, v1, c1, h1 = hybrid_layer(xp, pos_p, None, conv0, h0, p)
        xs, k2, v2, c2, h2 = hybrid_layer(xs, pos_s, (cache_win_k[l], cache_win_v[l]),
                                          state_conv[l], state_ssm[l], p)
        pk.append(k1); pv.append(v1); pc.append(c1); ph.append(h1)
        sk.append(k2); sv.append(v2); sc.append(c2); sh.append(h2)

    return (xp, xs,
            jnp.stack(pk), jnp.stack(pv), jnp.stack(pc), jnp.stack(ph),
            jnp.stack(sk), jnp.stack(sv), jnp.stack(sc), jnp.stack(sh))
```

```python
import functools
import math

import jax
import jax.numpy as jnp
from jax import lax
from jax.experimental import pallas as pl
from jax.experimental.pallas import tpu as pltpu

F32 = jnp.float32
BF16 = jnp.bfloat16

D_MODEL = 1024
HEAD_DIM = 64
N_Q_HEADS = 8
N_KV_HEADS = 2
ATTN_WIDTH = N_Q_HEADS * HEAD_DIM
KV_WIDTH = N_KV_HEADS * HEAD_DIM
WINDOW = 128
ROPE_THETA = 10000.0
ATTN_SCALE = HEAD_DIM ** -0.5
SSM_HEAD_DIM = 64
N_SSM_HEADS = 16
SSM_WIDTH = N_SSM_HEADS * SSM_HEAD_DIM
N_SSM_GROUPS = 2
D_STATE = 128
CONV_W = 4
CONV_DIM = SSM_WIDTH + 2 * N_SSM_GROUPS * D_STATE
SSD_CHUNK = 128
D_FF = 2816
EPS = 1e-6
PAST_LEN = 16384

LANES = 128
SUBLANES = 8

COL_Q = 0
COL_K = COL_Q + ATTN_WIDTH
COL_V = COL_K + KV_WIDTH
COL_Z = COL_V + KV_WIDTH
COL_XBC = COL_Z + SSM_WIDTH
COL_DT = COL_XBC + CONV_DIM
IN_PROJ_WIDTH = COL_DT + N_SSM_HEADS
PROJ_PAD = COL_DT + LANES

VMEM_LIMIT = 56 * 1024 * 1024


def _dot(a, b):
    return jnp.dot(a, b, preferred_element_type=F32)


def _dot_nt(a, b):
    return lax.dot_general(a, b, (((1,), (1,)), ((), ())), preferred_element_type=F32)


def _silu(x):
    return x / (1.0 + jnp.exp(-x))


def _softplus(x):
    return jnp.maximum(x, 0.0) + jnp.log1p(jnp.exp(-jnp.abs(x)))


def _rms_rows(x, g):
    return x * lax.rsqrt(jnp.mean(x * x, axis=-1, keepdims=True) + EPS) * g


def _split2(x):
    hi = x.astype(BF16)
    lo = (x - hi.astype(F32)).astype(BF16)
    return hi, lo


def _split3(x):
    hi = x.astype(BF16)
    r1 = x - hi.astype(F32)
    mid = r1.astype(BF16)
    lo = (r1 - mid.astype(F32)).astype(BF16)
    return hi, mid, lo


def _head_norm_rope(x, m128, g, cos, sin_signed, low_half):
    hi, lo = _split2(x * x)
    ms = _dot(hi, m128) + _dot(lo, m128)
    xn = x * lax.rsqrt(ms + EPS) * g
    rot = jnp.where(low_half, pltpu.roll(xn, LANES - HEAD_DIM // 2, 1),
                    pltpu.roll(xn, HEAD_DIM // 2, 1))
    return xn * cos + rot * sin_signed


def _inproj_kernel(*refs, tm, carry_rows, shift, n_keep, has_state):
    if has_state:
        (x_ref, gmix_ref, win_ref, m128_ref, qn_ref, kn_ref, cos_ref, sin_ref,
         convw_ref, convb_ref, dtb_ref, conv0_ref,
         q_ref, k2_ref, v2_ref, z_ref, xs_ref, bc_ref, dt_ref, knew_ref, vnew_ref, convst_ref,
         xbc_s) = refs
    else:
        (x_ref, gmix_ref, win_ref, m128_ref, qn_ref, kn_ref, cos_ref, sin_ref,
         convw_ref, convb_ref, dtb_ref,
         q_ref, k2_ref, v2_ref, z_ref, xs_ref, bc_ref, dt_ref, knew_ref, vnew_ref, convst_ref,
         xbc_s) = refs
        conv0_ref = None
    l = pl.program_id(1)
    n_l = pl.num_programs(1)

    x = x_ref[0]
    u = _rms_rows(x, gmix_ref[...]).astype(BF16)
    m128 = m128_ref[...]
    cos = cos_ref[...]
    sin = sin_ref[...]
    lane = lax.broadcasted_iota(jnp.int32, (tm, LANES), 1)
    low_half = (lane % HEAD_DIM) < (HEAD_DIM // 2)

    for j in range(ATTN_WIDTH // LANES):
        c0 = COL_Q + j * LANES
        qj = _dot(u, win_ref[:, c0:c0 + LANES])
        qj = _head_norm_rope(qj, m128, qn_ref[...], cos, sin, low_half)
        q_ref[0, :, j * LANES:(j + 1) * LANES] = (qj * ATTN_SCALE).astype(BF16)

    k = _dot(u, win_ref[:, COL_K:COL_K + KV_WIDTH])
    k = _head_norm_rope(k, m128, kn_ref[...], cos, sin, low_half)
    v = _dot(u, win_ref[:, COL_V:COL_V + KV_WIDTH])
    k2_ref[0, :, 0:LANES] = k.astype(BF16)
    k2_ref[0, :, LANES:2 * LANES] = pltpu.roll(k, HEAD_DIM, 1).astype(BF16)
    v2_ref[0, :, 0:LANES] = v.astype(BF16)
    v2_ref[0, :, LANES:2 * LANES] = pltpu.roll(v, HEAD_DIM, 1).astype(BF16)

    @pl.when(l == n_l - 1)
    def _():
        knew_ref[0] = k[tm - n_keep:, :]
        vnew_ref[0] = v[tm - n_keep:, :]

    for j in range(SSM_WIDTH // 512):
        c0 = COL_Z + j * 512
        z_ref[0, :, j * 512:(j + 1) * 512] = _dot(u, win_ref[:, c0:c0 + 512])

    dt = _dot(u, win_ref[:, COL_DT:COL_DT + LANES]) + dtb_ref[...]
    dt_ref[0] = _softplus(dt)

    @pl.when(l == 0)
    def _():
        if has_state:
            xbc_s[0:carry_rows, :] = conv0_ref[...]
        else:
            xbc_s[0:carry_rows, :] = jnp.zeros((carry_rows, CONV_DIM), F32)

    cw = 512
    for j in range(CONV_DIM // cw):
        c0 = COL_XBC + j * cw
        xbc_s[carry_rows:carry_rows + tm, j * cw:(j + 1) * cw] = _dot(u, win_ref[:, c0:c0 + cw])
    for j in range(CONV_DIM // cw):
        cs = slice(j * cw, (j + 1) * cw)
        acc = convb_ref[:, cs]
        for i in range(CONV_W):
            off = carry_rows - (CONV_W - 1 - i) * shift
            acc = acc + xbc_s[off:off + tm, cs] * convw_ref[i:i + 1, cs]
        act = _silu(acc)
        if j * cw < SSM_WIDTH:
            xs_ref[0, :, cs] = act
        else:
            bc_ref[0] = act.astype(BF16)

    @pl.when(l == n_l - 1)
    def _():
        convst_ref[0] = xbc_s[tm:tm + carry_rows, :]

    xbc_s[0:carry_rows, :] = xbc_s[tm:tm + carry_rows, :]


def _inproj(x, gmix, win, m128, qn, kn, cos, sin, convw, convb, dtb, conv0, *, tm, carry_rows, shift, n_keep):
    nb, seq, _ = x.shape
    n_l = seq // tm
    has_state = conv0 is not None
    const = lambda shape: pl.BlockSpec(shape, lambda b, l: (0,) * len(shape))
    row = lambda w: pl.BlockSpec((1, tm, w), lambda b, l: (b, l, 0))
    in_specs = [row(D_MODEL), const((1, D_MODEL)), const((D_MODEL, PROJ_PAD)), const((LANES, LANES)),
                const((1, LANES)), const((1, LANES)),
                pl.BlockSpec((tm, LANES), lambda b, l: (l, 0)), pl.BlockSpec((tm, LANES), lambda b, l: (l, 0)),
                const((CONV_W, CONV_DIM)), const((1, CONV_DIM)), const((1, LANES))]
    args = [x, gmix, win, m128, qn, kn, cos, sin, convw, convb, dtb]
    if has_state:
        in_specs.append(const((carry_rows, CONV_DIM)))
        args.append(conv0)
    last = lambda rows, w: pl.BlockSpec((1, rows, w), lambda b, l: (b, 0, 0))
    out_specs = [row(ATTN_WIDTH), row(2 * KV_WIDTH), row(2 * KV_WIDTH), row(SSM_WIDTH), row(SSM_WIDTH),
                 row(CONV_DIM - SSM_WIDTH), row(LANES),
                 last(n_keep, KV_WIDTH), last(n_keep, KV_WIDTH), last(carry_rows, CONV_DIM)]
    out_shape = [jax.ShapeDtypeStruct((nb, seq, ATTN_WIDTH), BF16),
                 jax.ShapeDtypeStruct((nb, seq, 2 * KV_WIDTH), BF16),
                 jax.ShapeDtypeStruct((nb, seq, 2 * KV_WIDTH), BF16),
                 jax.ShapeDtypeStruct((nb, seq, SSM_WIDTH), F32),
                 jax.ShapeDtypeStruct((nb, seq, SSM_WIDTH), F32),
                 jax.ShapeDtypeStruct((nb, seq, CONV_DIM - SSM_WIDTH), BF16),
                 jax.ShapeDtypeStruct((nb, seq, LANES), F32),
                 jax.ShapeDtypeStruct((nb, n_keep, KV_WIDTH), F32),
                 jax.ShapeDtypeStruct((nb, n_keep, KV_WIDTH), F32),
                 jax.ShapeDtypeStruct((nb, carry_rows, CONV_DIM), F32)]
    kern = functools.partial(_inproj_kernel, tm=tm, carry_rows=carry_rows, shift=shift, n_keep=n_keep,
                             has_state=has_state)
    return pl.pallas_call(
        kern, grid=(nb, n_l), in_specs=in_specs, out_specs=out_specs, out_shape=out_shape,
        scratch_shapes=[pltpu.VMEM((carry_rows + tm, CONV_DIM), F32)],
        compiler_params=pltpu.CompilerParams(dimension_semantics=("arbitrary", "arbitrary"),
                                             vmem_limit_bytes=VMEM_LIMIT),
        name="inproj",
    )(*args)


def _attn_kernel(sinks_ref, q_ref, kc_ref, kp_ref, vc_ref, vp_ref, o_ref, *, tq):
    i = pl.program_id(1)
    nblk = tq // WINDOW
    lane = lax.broadcasted_iota(jnp.int32, (WINDOW, LANES), 1)
    low = lane < HEAD_DIM
    qi = lax.broadcasted_iota(jnp.int32, (WINDOW, 2 * WINDOW), 0)
    kj = lax.broadcasted_iota(jnp.int32, (WINDOW, 2 * WINDOW), 1)
    diff = qi + WINDOW - kj
    band = (diff >= 0) & (diff < WINDOW)
    zero_bf = jnp.zeros((WINDOW, LANES), BF16)
    for blk in range(nblk):
        r0 = blk * WINDOW
        if blk == 0:
            k_prev, v_prev = kp_ref[0], vp_ref[0]
            mask = band & ((i > 0) | (kj >= WINDOW))
        else:
            k_prev = kc_ref[0, r0 - WINDOW:r0, :]
            v_prev = vc_ref[0, r0 - WINDOW:r0, :]
            mask = band
        kk = jnp.concatenate([k_prev, kc_ref[0, r0:r0 + WINDOW, :]], axis=0)
        vv = jnp.concatenate([v_prev, vc_ref[0, r0:r0 + WINDOW, :]], axis=0)
        for pair in range(N_Q_HEADS // 2):
            g = (2 * pair) // (N_Q_HEADS // N_KV_HEADS)
            qp = q_ref[0, r0:r0 + WINDOW, pair * LANES:(pair + 1) * LANES]
            outs = []
            for e in range(2):
                h = 2 * pair + e
                var = 0 if g == e else 1
                qh = jnp.where(low if e == 0 else ~low, qp, zero_bf)
                s = _dot_nt(qh, kk[:, var * LANES:(var + 1) * LANES])
                s = jnp.where(mask, s, -jnp.inf)
                sk = sinks_ref[h]
                m = jnp.maximum(jnp.max(s, axis=-1, keepdims=True), sk)
                p = jnp.exp(s - m)
                den = jnp.sum(p, axis=-1, keepdims=True) + jnp.exp(sk - m)
                o = _dot(p.astype(BF16), vv[:, var * LANES:(var + 1) * LANES])
                outs.append(o / den)
            o_ref[0, r0:r0 + WINDOW, pair * LANES:(pair + 1) * LANES] = (
                jnp.where(low, outs[0], outs[1]).astype(BF16))


def _attention(sinks, q, k2, v2, *, tq):
    nb, seq, _ = q.shape
    r = tq // WINDOW
    cur = lambda w: pl.BlockSpec((1, tq, w), lambda b, i: (b, i, 0))
    prev = lambda w: pl.BlockSpec((1, WINDOW, w), lambda b, i: (b, jnp.maximum(i * r - 1, 0), 0))
    return pl.pallas_call(
        functools.partial(_attn_kernel, tq=tq),
        grid=(nb, seq // tq),
        in_specs=[pl.BlockSpec(memory_space=pltpu.SMEM),
                  cur(ATTN_WIDTH), cur(2 * KV_WIDTH), prev(2 * KV_WIDTH), cur(2 * KV_WIDTH), prev(2 * KV_WIDTH)],
        out_specs=cur(ATTN_WIDTH),
        out_shape=jax.ShapeDtypeStruct((nb, seq, ATTN_WIDTH), BF16),
        compiler_params=pltpu.CompilerParams(dimension_semantics=("arbitrary", "arbitrary"),
                                             vmem_limit_bytes=VMEM_LIMIT),
        name="swa_attn",
    )(sinks, q, k2, k2, v2, v2)


def _ssd_kernel(xs_ref, bc_ref, dt_ref, z_ref, a_ref, dskip_ref, gnorm_ref, tri_ref,
                y_ref, hT_ref, h_s, y_s, *, tc):
    l = pl.program_id(1)
    n_l = pl.num_programs(1)
    gw = SSM_WIDTH // N_SSM_GROUPS
    hpg = N_SSM_HEADS // N_SSM_GROUPS

    @pl.when(l == 0)
    def _():
        h_s[...] = jnp.zeros(h_s.shape, F32)

    lane = lax.broadcasted_iota(jnp.int32, (SSD_CHUNK, LANES), 1)
    low = lane < SSM_HEAD_DIM
    qi = lax.broadcasted_iota(jnp.int32, (SSD_CHUNK, SSD_CHUNK), 0)
    sj = lax.broadcasted_iota(jnp.int32, (SSD_CHUNK, SSD_CHUNK), 1)
    causal = sj <= qi
    a_neg = -jnp.exp(a_ref[...])
    tri = tri_ref[...]
    zero_bf = jnp.zeros((SSD_CHUNK, LANES), BF16)

    def chunk(c, carry):
        r0 = pl.multiple_of(c * SSD_CHUNK, SSD_CHUNK)
        rows = pl.ds(r0, SSD_CHUNK)
        dt = dt_ref[0, rows, :]
        hi, mid, lo = _split3(dt * a_neg)
        acum = _dot(tri, hi) + _dot(tri, mid) + _dot(tri, lo)
        acum_t = acum.T
        dt_t = dt.T
        a_end_t = acum_t[:, SSD_CHUNK - 1:SSD_CHUNK]
        w_end_t = dt_t * jnp.exp(a_end_t - acum_t)
        dec_t = jnp.exp(a_end_t)
        dec_t = jnp.broadcast_to(dec_t, (LANES, LANES))
        for g in range(N_SSM_GROUPS):
            b_g = bc_ref[0, rows, g * D_STATE:(g + 1) * D_STATE]
            c_g = bc_ref[0, rows, (N_SSM_GROUPS + g) * D_STATE:(N_SSM_GROUPS + g + 1) * D_STATE]
            cb = _dot_nt(c_g, b_g)
            b_gt = b_g.astype(F32).T
            y_inter = _dot(c_g, h_s[g].astype(BF16))
            for pr in range(hpg // 2):
                lhs_y, lhs_s, escale, dsc = [], [], [], []
                for e in range(2):
                    h = g * hpg + 2 * pr + e
                    col = jnp.broadcast_to(acum[:, h:h + 1], (SSD_CHUNK, SSD_CHUNK))
                    row = acum_t[h:h + 1, :]
                    decay = jnp.where(causal, jnp.exp(col - row), 0.0)
                    lhs_y.append((cb * decay * dt_t[h:h + 1, :]).astype(BF16))
                    lhs_s.append((b_gt * w_end_t[h:h + 1, :]).astype(BF16))
                    escale.append(jnp.exp(col))
                    dsc.append(dec_t[h:h + 1, :])
                c0 = g * gw + pr * LANES
                x_pair = xs_ref[0, rows, c0:c0 + LANES]
                x_bf = x_pair.astype(BF16)
                rhs = jnp.concatenate([jnp.where(low, x_bf, zero_bf), jnp.where(low, zero_bf, x_bf)], axis=0)
                y_in = _dot(jnp.concatenate(lhs_y, axis=1), rhs)
                st = _dot(jnp.concatenate(lhs_s, axis=1), rhs)
                lc = pr * LANES
                y = y_in + y_inter[:, lc:lc + LANES] * jnp.where(low, escale[0], escale[1])
                y = y + dskip_ref[:, c0:c0 + LANES] * x_pair
                zz = z_ref[0, rows, c0:c0 + LANES]
                y_s[:, c0:c0 + LANES] = y * _silu(zz)
                h_s[g, :, lc:lc + LANES] = h_s[g, :, lc:lc + LANES] * jnp.where(low, dsc[0], dsc[1]) + st
        for g in range(N_SSM_GROUPS):
            yg = y_s[:, g * gw:(g + 1) * gw]
            yg = yg * lax.rsqrt(jnp.mean(yg * yg, axis=-1, keepdims=True) + EPS)
            y_ref[0, rows, g * gw:(g + 1) * gw] = (yg * gnorm_ref[:, g * gw:(g + 1) * gw]).astype(BF16)
        return carry

    lax.fori_loop(0, tc // SSD_CHUNK, chunk, 0)

    @pl.when(l == n_l - 1)
    def _():
        hT_ref[0] = h_s[...]


def _ssd(xs, bc, dt, z, a_pad, dskip, gnorm, tri, *, tc):
    nb, seq, _ = xs.shape
    row = lambda w: pl.BlockSpec((1, tc, w), lambda b, l: (b, l, 0))
    const = lambda shape: pl.BlockSpec(shape, lambda b, l: (0,) * len(shape))
    gw = SSM_WIDTH // N_SSM_GROUPS
    return pl.pallas_call(
        functools.partial(_ssd_kernel, tc=tc),
        grid=(nb, seq // tc),
        in_specs=[row(SSM_WIDTH), row(CONV_DIM - SSM_WIDTH), row(LANES), row(SSM_WIDTH),
                  const((1, LANES)), const((1, SSM_WIDTH)), const((1, SSM_WIDTH)), const((SSD_CHUNK, SSD_CHUNK))],
        out_specs=[row(SSM_WIDTH),
                   pl.BlockSpec((1, N_SSM_GROUPS, D_STATE, gw), lambda b, l: (b, 0, 0, 0))],
        out_shape=[jax.ShapeDtypeStruct((nb, seq, SSM_WIDTH), BF16),
                   jax.ShapeDtypeStruct((nb, N_SSM_GROUPS, D_STATE, gw), F32)],
        scratch_shapes=[pltpu.VMEM((N_SSM_GROUPS, D_STATE, gw), F32), pltpu.VMEM((SSD_CHUNK, SSM_WIDTH), F32)],
        compiler_params=pltpu.CompilerParams(dimension_semantics=("arbitrary", "arbitrary"),
                                             vmem_limit_bytes=VMEM_LIMIT),
        name="ssd_scan",
    )(xs, bc, dt, z, a_pad, dskip, gnorm, tri)


def _outffn_kernel(x_ref, oa_ref, ys_ref, wo_ref, gffn_ref, wgu_ref, wd_ref, o_ref, *, ffc):
    x = x_ref[...]
    xm = x + _dot(oa_ref[...], wo_ref[0:ATTN_WIDTH, :]) + _dot(ys_ref[...], wo_ref[ATTN_WIDTH:, :])
    hn = _rms_rows(xm, gffn_ref[...]).astype(BF16)
    acc = jnp.zeros_like(xm)
    for j in range(D_FF // ffc):
        g = _dot(hn, wgu_ref[:, j * ffc:(j + 1) * ffc])
        up = _dot(hn, wgu_ref[:, D_FF + j * ffc:D_FF + (j + 1) * ffc])
        acc = acc + _dot((_silu(g) * up).astype(BF16), wd_ref[j * ffc:(j + 1) * ffc, :])
    o_ref[...] = xm + acc


def _outffn(x, oa, ys, wo, gffn, wgu, wd, *, tm, ffc):
    rows = x.shape[0]
    row = lambda w: pl.BlockSpec((tm, w), lambda i: (i, 0))
    const = lambda shape: pl.BlockSpec(shape, lambda i: (0,) * len(shape), pipeline_mode=pl.Buffered(1))
    return pl.pallas_call(
        functools.partial(_outffn_kernel, ffc=ffc),
        grid=(rows // tm,),
        in_specs=[row(D_MODEL), row(ATTN_WIDTH), row(SSM_WIDTH),
                  const((ATTN_WIDTH + SSM_WIDTH, D_MODEL)), const((1, D_MODEL)),
                  const((D_MODEL, 2 * D_FF)), const((D_FF, D_MODEL))],
        out_specs=row(D_MODEL),
        out_shape=jax.ShapeDtypeStruct((rows, D_MODEL), F32),
        compiler_params=pltpu.CompilerParams(dimension_semantics=("arbitrary",),
                                             vmem_limit_bytes=VMEM_LIMIT),
        name="outproj_ffn",
    )(x, oa, ys, wo, gffn, wgu, wd)


T_PAD = SUBLANES


def _attn_dec_kernel(q_ref, kc_ref, kn_ref, vc_ref, vn_ref, sink_ref, o_ref, *, t_new):
    q = q_ref[...]
    nbat, nq, _ = q.shape
    qpk = N_Q_HEADS // N_KV_HEADS
    s_c = jnp.einsum('bqd,bkd->bqk', q, kc_ref[...], preferred_element_type=F32)
    s_n = jnp.einsum('bqd,bkd->bqk', q, kn_ref[...], preferred_element_type=F32)
    n_c = s_c.shape[-1]
    t_c = lax.broadcasted_iota(jnp.int32, (nbat, nq, n_c), 1) // qpk
    j_c = lax.broadcasted_iota(jnp.int32, (nbat, nq, n_c), 2)
    s_c = jnp.where(j_c > t_c + (n_c - WINDOW), s_c, -jnp.inf)
    t_n = lax.broadcasted_iota(jnp.int32, (nbat, nq, T_PAD), 1) // qpk
    j_n = lax.broadcasted_iota(jnp.int32, (nbat, nq, T_PAD), 2)
    s_n = jnp.where((j_n <= t_n) & (j_n < t_new), s_n, -jnp.inf)
    sk = sink_ref[...]
    m = jnp.maximum(jnp.maximum(jnp.max(s_c, axis=-1, keepdims=True), jnp.max(s_n, axis=-1, keepdims=True)), sk)
    p_c = jnp.exp(s_c - m)
    p_n = jnp.exp(s_n - m)
    den = jnp.sum(p_c, axis=-1, keepdims=True) + jnp.sum(p_n, axis=-1, keepdims=True) + jnp.exp(sk - m)
    o = (jnp.einsum('bqk,bkd->bqd', p_c.astype(BF16), vc_ref[...], preferred_element_type=F32)
         + jnp.einsum('bqk,bkd->bqd', p_n.astype(BF16), vn_ref[...], preferred_element_type=F32))
    o_ref[...] = (o / den).astype(BF16)


def _attention_dec(q, kc, kn, vc, vn, sink_rows, *, t_new, bb):
    nbat, nq, hd = q.shape
    n_c = kc.shape[1]
    blk = lambda r, w: pl.BlockSpec((bb, r, w), lambda i: (i, 0, 0))
    return pl.pallas_call(
        functools.partial(_attn_dec_kernel, t_new=t_new),
        grid=(nbat // bb,),
        in_specs=[blk(nq, hd), blk(n_c, hd), blk(T_PAD, hd), blk(n_c, hd), blk(T_PAD, hd), blk(nq, 1)],
        out_specs=blk(nq, hd),
        out_shape=jax.ShapeDtypeStruct((nbat, nq, hd), BF16),
        compiler_params=pltpu.CompilerParams(dimension_semantics=("arbitrary",), vmem_limit_bytes=VMEM_LIMIT),
        name="swa_attn_decode",
    )(q, kc, kn, vc, vn, sink_rows)


def _ssd_dec_kernel(xs_ref, bc_ref, dt_ref, z_ref, h0_ref, a_ref, dskip_ref, gnorm_ref, tri_ref, exp_ref,
                    y_ref, h_ref, *, bb, t_new):
    gw = SSM_WIDTH // N_SSM_GROUPS
    hpg = N_SSM_HEADS // N_SSM_GROUPS
    a_neg = -jnp.exp(a_ref[...])
    tri = tri_ref[...]
    expand = exp_ref[...]
    t_row = lax.broadcasted_iota(jnp.int32, (T_PAD, gw), 0)

    def widen(v):
        hi, mid, lo = _split3(v)
        return _dot(hi, expand) + _dot(mid, expand) + _dot(lo, expand)

    def seq_body(i, carry):
        dt = dt_ref[i]
        hi, mid, lo = _split3(dt * a_neg)
        acum = _dot(tri, hi) + _dot(tri, mid) + _dot(tri, lo)
        a_w = widen(acum)
        xs = xs_ref[i]
        xdt = xs * widen(dt)
        a_end = a_w[T_PAD - 1:T_PAD, :]
        xw = (xdt * jnp.exp(a_end - a_w)).astype(BF16)
        ea = jnp.exp(a_w)
        dec = jnp.broadcast_to(jnp.exp(acum[T_PAD - 1:T_PAD, :]), (LANES, LANES)).T
        bc = bc_ref[i]
        zz = z_ref[i]
        for g in range(N_SSM_GROUPS):
            b_g = bc[:, g * D_STATE:(g + 1) * D_STATE]
            c_g = bc[:, (N_SSM_GROUPS + g) * D_STATE:(N_SSM_GROUPS + g + 1) * D_STATE]
            cb = _dot_nt(c_g, b_g)
            ls = slice(g * gw, (g + 1) * gw)
            a_g = a_w[:, ls]
            y = jnp.zeros((T_PAD, gw), F32)
            for s in range(t_new):
                w = jnp.where(t_row >= s, jnp.exp(a_g - a_g[s:s + 1, :]), 0.0)
                y = y + (w * cb[:, s:s + 1]) * xdt[s:s + 1, ls]
            h0g = h0_ref[i, g * hpg:(g + 1) * hpg].reshape(gw, D_STATE)
            y = y + _dot_nt(c_g, h0g.astype(BF16)) * ea[:, ls]
            y = y + dskip_ref[:, ls] * xs[:, ls]
            y = y * _silu(zz[:, ls])
            y = y * lax.rsqrt(jnp.mean(y * y, axis=-1, keepdims=True) + EPS)
            y_ref[i, :, ls] = (y * gnorm_ref[:, ls]).astype(BF16)
            delta = lax.dot_general(xw[:, ls], b_g, (((0,), (0,)), ((), ())), preferred_element_type=F32)
            for hl in range(hpg):
                h = g * hpg + hl
                h_ref[i, h] = (h0_ref[i, h] * dec[h:h + 1, :]
                               + delta[hl * SSM_HEAD_DIM:(hl + 1) * SSM_HEAD_DIM, :])
        return carry

    lax.fori_loop(0, bb, seq_body, 0)


def _ssd_dec(xs3, bc3, dt3, z3, h0, a_pad, dskip, gnorm, tri8, expand, *, bb, t_new):
    nb = xs3.shape[0]
    blk = lambda w: pl.BlockSpec((bb, T_PAD, w), lambda i: (i, 0, 0))
    const = lambda shape: pl.BlockSpec(shape, lambda i: (0,) * len(shape))
    st = pl.BlockSpec((bb, N_SSM_HEADS, SSM_HEAD_DIM, D_STATE), lambda i: (i, 0, 0, 0))
    return pl.pallas_call(
        functools.partial(_ssd_dec_kernel, bb=bb, t_new=t_new),
        grid=(nb // bb,),
        in_specs=[blk(SSM_WIDTH), blk(CONV_DIM - SSM_WIDTH), blk(LANES), blk(SSM_WIDTH), st,
                  const((1, LANES)), const((1, SSM_WIDTH)), const((1, SSM_WIDTH)),
                  const((T_PAD, T_PAD)), const((LANES, SSM_WIDTH))],
        out_specs=[blk(SSM_WIDTH), st],
        out_shape=[jax.ShapeDtypeStruct((nb, T_PAD, SSM_WIDTH), BF16),
                   jax.ShapeDtypeStruct(h0.shape, F32)],
        compiler_params=pltpu.CompilerParams(dimension_semantics=("arbitrary",), vmem_limit_bytes=VMEM_LIMIT),
        name="ssd_decode",
    )(xs3, bc3, dt3, z3, h0, a_pad, dskip, gnorm, tri8, expand)


def _rope_tables(pos):
    half = HEAD_DIM // 2
    inv = ROPE_THETA ** (-jnp.arange(half, dtype=F32) / half)
    ang = pos.astype(F32)[:, None] * inv[None, :]
    cos = jnp.tile(jnp.cos(ang), (1, LANES // half))
    sin = jnp.sin(ang)
    sin_signed = jnp.tile(jnp.concatenate([-sin, sin], axis=1), (1, LANES // HEAD_DIM))
    return cos, sin_signed


def _layer_consts(l, p):
    w_in = jnp.pad(p['w_in'][l], ((0, 0), (0, PROJ_PAD - IN_PROJ_WIDTH))).astype(BF16)
    return dict(
        gmix=p['norm_mix'][l][None, :],
        win=w_in,
        qn=jnp.tile(p['q_norm'][l], LANES // HEAD_DIM)[None, :],
        kn=jnp.tile(p['k_norm'][l], LANES // HEAD_DIM)[None, :],
        convw=p['conv_w'][l],
        convb=p['conv_b'][l][None, :],
        dtb=jnp.pad(p['dt_bias'][l], (0, LANES - N_SSM_HEADS))[None, :],
        a_pad=jnp.pad(p['a_log'][l], (0, LANES - N_SSM_HEADS))[None, :],
        dskip=jnp.repeat(p['d_skip'][l], SSM_HEAD_DIM)[None, :],
        gnorm=p['ssm_norm'][l][None, :],
        sinks=p['sinks'][l],
        wo=p['w_out'][l].astype(BF16),
        gffn=p['norm_ffn'][l][None, :],
        wgu=p['w_gate_up'][l].astype(BF16),
        wd=p['w_down'][l].astype(BF16),
    )


def _state_from_t(h_t):
    nb = h_t.shape[0]
    hpg = N_SSM_HEADS // N_SSM_GROUPS
    h = h_t.reshape(nb, N_SSM_GROUPS, D_STATE, hpg, SSM_HEAD_DIM)
    return jnp.transpose(h, (0, 1, 3, 4, 2)).reshape(nb, N_SSM_HEADS, SSM_HEAD_DIM, D_STATE)


def _prompt_layer(x, c, shared, *, tm):
    nb, seq, _ = x.shape
    (q, k2, v2, z, xs, bc, dt, knew, vnew, convst) = _inproj(
        x, c['gmix'], c['win'], shared['m128'], c['qn'], c['kn'], shared['cos_p'], shared['sin_p'],
        c['convw'], c['convb'], c['dtb'], None, tm=tm, carry_rows=SUBLANES, shift=1, n_keep=min(WINDOW, seq))
    oa = _attention(c['sinks'], q, k2, v2, tq=tm)
    ys, h_t = _ssd(xs, bc, dt, z, c['a_pad'], c['dskip'], c['gnorm'], shared['tri'], tc=tm)
    rows = nb * seq
    xo = _outffn(x.reshape(rows, D_MODEL), oa.reshape(rows, ATTN_WIDTH), ys.reshape(rows, SSM_WIDTH),
                 c['wo'], c['gffn'], c['wgu'], c['wd'], tm=tm, ffc=D_FF // 2)
    n_keep = knew.shape[1]
    return (xo.reshape(nb, seq, D_MODEL),
            knew.reshape(nb, n_keep, N_KV_HEADS, HEAD_DIM), vnew.reshape(nb, n_keep, N_KV_HEADS, HEAD_DIM),
            convst[:, SUBLANES - (CONV_W - 1):, :], _state_from_t(h_t))


def _sample_layer(x_t, cache_k, cache_v, conv0, h0, c, shared, *, n_seq, t_new):
    rows = t_new * n_seq
    qpk = N_Q_HEADS // N_KV_HEADS
    n_c = cache_k.shape[1]
    conv0_t = jnp.transpose(conv0, (1, 0, 2)).reshape((CONV_W - 1) * n_seq, CONV_DIM)
    (q, _, _, z, xs, bc, dt, knew, vnew, convst) = _inproj(
        x_t, c['gmix'], c['win'], shared['m128'], c['qn'], c['kn'], shared['cos_s'], shared['sin_s'],
        c['convw'], c['convb'], c['dtb'], conv0_t, tm=rows, carry_rows=(CONV_W - 1) * n_seq, shift=n_seq,
        n_keep=rows)

    def to_bg(a, t, inner):
        a = a.reshape(t, n_seq, N_KV_HEADS, inner, HEAD_DIM)
        return jnp.transpose(a, (1, 2, 0, 3, 4)).reshape(n_seq * N_KV_HEADS, t * inner, HEAD_DIM)

    def pad_t(a):
        return jnp.pad(a, ((0, 0), (0, T_PAD - a.shape[1]), (0, 0)))

    def cache_bg(a):
        return jnp.transpose(a, (0, 2, 1, 3)).reshape(n_seq * N_KV_HEADS, n_c, HEAD_DIM).astype(BF16)

    q_bg = to_bg(q[0], t_new, qpk)
    kn_bg = pad_t(to_bg(knew[0], t_new, 1)).astype(BF16)
    vn_bg = pad_t(to_bg(vnew[0], t_new, 1)).astype(BF16)
    sink_rows = jnp.tile(c['sinks'].reshape(N_KV_HEADS, 1, qpk), (n_seq, t_new, 1)).reshape(
        n_seq * N_KV_HEADS, t_new * qpk, 1)
    bb_a = min(64, n_seq * N_KV_HEADS)
    o_bg = _attention_dec(q_bg, cache_bg(cache_k), kn_bg, cache_bg(cache_v), vn_bg, sink_rows,
                          t_new=t_new, bb=bb_a)
    oa = jnp.transpose(o_bg.reshape(n_seq, N_KV_HEADS, t_new, qpk, HEAD_DIM), (2, 0, 1, 3, 4)).reshape(
        rows, ATTN_WIDTH)

    def to_seq(a):
        a = jnp.transpose(a[0].reshape(t_new, n_seq, a.shape[-1]), (1, 0, 2))
        return pad_t(a)

    y3, h_new = _ssd_dec(to_seq(xs), to_seq(bc), to_seq(dt), to_seq(z), h0, c['a_pad'], c['dskip'], c['gnorm'],
                         shared['tri8'], shared['expand'], bb=min(8, n_seq), t_new=t_new)
    ys = jnp.transpose(y3[:, :t_new], (1, 0, 2)).reshape(rows, SSM_WIDTH)

    xo = _outffn(x_t[0], oa, ys, c['wo'], c['gffn'], c['wgu'], c['wd'], tm=min(512, rows), ffc=D_FF // 2)

    def new_rows(a):
        return jnp.transpose(a.reshape(t_new, n_seq, N_KV_HEADS, HEAD_DIM), (1, 0, 2, 3))

    win_k = jnp.concatenate([cache_k, new_rows(knew[0])], axis=1)[:, t_new:]
    win_v = jnp.concatenate([cache_v, new_rows(vnew[0])], axis=1)[:, t_new:]
    conv_new = jnp.transpose(convst[0].reshape(CONV_W - 1, n_seq, CONV_DIM), (1, 0, 2))
    return xo[None], win_k, win_v, conv_new, h_new


def kernel(x_prompt, x_sample, cache_win_k, cache_win_v, state_conv, state_ssm,
           norm_mix, w_in, q_norm, k_norm, sinks, conv_w, conv_b, dt_bias, a_log,
           d_skip, ssm_norm, w_out, norm_ffn, w_gate_up, w_down):
    p = dict(norm_mix=norm_mix, w_in=w_in, q_norm=q_norm, k_norm=k_norm, sinks=sinks, conv_w=conv_w,
             conv_b=conv_b, dt_bias=dt_bias, a_log=a_log, d_skip=d_skip, ssm_norm=ssm_norm, w_out=w_out,
             norm_ffn=norm_ffn, w_gate_up=w_gate_up, w_down=w_down)
    depth = w_in.shape[0]
    seq = x_prompt.shape[1]
    cos_p, sin_p = _rope_tables(jnp.arange(seq, dtype=jnp.int32))
    half_blk = jnp.arange(LANES) // HEAD_DIM
    m128 = (jnp.where(half_blk[:, None] == half_blk[None, :], 1.0 / HEAD_DIM, 0.0)).astype(BF16)
    tri = (jnp.arange(SSD_CHUNK)[:, None] >= jnp.arange(SSD_CHUNK)[None, :]).astype(BF16)
    n_seq, t_new, _ = x_sample.shape
    pos_s = PAST_LEN + jnp.repeat(jnp.arange(t_new, dtype=jnp.int32), n_seq)
    cos_s, sin_s = _rope_tables(pos_s)
    tri8 = (jnp.arange(T_PAD)[:, None] >= jnp.arange(T_PAD)[None, :]).astype(BF16)
    expand = (jnp.arange(LANES)[:, None] == (jnp.arange(SSM_WIDTH) // SSM_HEAD_DIM)[None, :]).astype(BF16)
    shared = dict(cos_p=cos_p, sin_p=sin_p, cos_s=cos_s, sin_s=sin_s, m128=m128, tri=tri, tri8=tri8,
                  expand=expand)
    tm = min(512, seq)

    xp = x_prompt
    xs = jnp.transpose(x_sample, (1, 0, 2)).reshape(1, t_new * n_seq, D_MODEL)
    pk, pv, pc, ph = [], [], [], []
    sk, sv, sc, sh = [], [], [], []
    for l in range(depth):
        c = _layer_consts(l, p)
        xp, k1, v1, c1, h1 = _prompt_layer(xp, c, shared, tm=tm)
        xs, k2, v2, c2, h2 = _sample_layer(xs, cache_win_k[l], cache_win_v[l], state_conv[l], state_ssm[l],
                                           c, shared, n_seq=n_seq, t_new=t_new)
        pk.append(k1); pv.append(v1); pc.append(c1); ph.append(h1)
        sk.append(k2); sv.append(v2); sc.append(c2); sh.append(h2)
    ys = jnp.transpose(xs.reshape(t_new, n_seq, D_MODEL), (1, 0, 2))
    return (xp, ys, jnp.stack(pk), jnp.stack(pv), jnp.stack(pc), jnp.stack(ph),
            jnp.stack(sk), jnp.stack(sv), jnp.stack(sc), jnp.stack(sh))
```

```python
import functools
import math

import jax
import jax.numpy as jnp
from jax import lax
from jax.experimental import pallas as pl
from jax.experimental.pallas import tpu as pltpu

F32 = jnp.float32
BF16 = jnp.bfloat16

D_MODEL = 1024
HEAD_DIM = 64
N_Q_HEADS = 8
N_KV_HEADS = 2
ATTN_WIDTH = N_Q_HEADS * HEAD_DIM
KV_WIDTH = N_KV_HEADS * HEAD_DIM
WINDOW = 128
ROPE_THETA = 10000.0
ATTN_SCALE = HEAD_DIM ** -0.5
SSM_HEAD_DIM = 64
N_SSM_HEADS = 16
SSM_WIDTH = N_SSM_HEADS * SSM_HEAD_DIM
N_SSM_GROUPS = 2
D_STATE = 128
CONV_W = 4
CONV_DIM = SSM_WIDTH + 2 * N_SSM_GROUPS * D_STATE
SSD_CHUNK = 128
D_FF = 2816
EPS = 1e-6
LOG2E = math.log2(math.e)
PAST_LEN = 16384

LANES = 128
SUBLANES = 8

COL_Q = 0
COL_K = COL_Q + ATTN_WIDTH
COL_V = COL_K + KV_WIDTH
COL_DT = COL_V + KV_WIDTH
COL_Z = COL_DT + LANES
COL_XBC = COL_Z + SSM_WIDTH
PROJ_PAD = COL_XBC + CONV_DIM
HEAD_COLS = COL_Z

VMEM_LIMIT = 56 * 1024 * 1024


def _dot(a, b):
    return jnp.dot(a, b, preferred_element_type=F32)


def _dot_nt(a, b):
    return lax.dot_general(a, b, (((1,), (1,)), ((), ())), preferred_element_type=F32)


def _silu(x):
    return x / (1.0 + jnp.exp(-x))


def _softplus(x):
    return jnp.maximum(x, 0.0) + jnp.log1p(jnp.exp(-jnp.abs(x)))


def _rms_rows(x, g):
    return x * lax.rsqrt(jnp.mean(x * x, axis=-1, keepdims=True) + EPS) * g


def _split2(x):
    hi = x.astype(BF16)
    lo = (x - hi.astype(F32)).astype(BF16)
    return hi, lo


def _split3(x):
    hi = x.astype(BF16)
    r1 = x - hi.astype(F32)
    mid = r1.astype(BF16)
    lo = (r1 - mid.astype(F32)).astype(BF16)
    return hi, mid, lo


def _head_norm_rope(x, m128, g, cos, sin_signed, low_half):
    hi, lo = _split2(x * x)
    ms = _dot(jnp.concatenate([hi, lo], axis=1), m128)
    xn = x * lax.rsqrt(ms + EPS) * g
    rot = jnp.where(low_half, pltpu.roll(xn, LANES - HEAD_DIM // 2, 1),
                    pltpu.roll(xn, HEAD_DIM // 2, 1))
    return xn * cos + rot * sin_signed


def _inproj_kernel(*refs, tm, carry_rows, shift, n_keep, has_state):
    if has_state:
        (x_ref, gmix_ref, win_ref, m128_ref, qn_ref, kn_ref, cos_ref, sin_ref,
         convw_ref, convb_ref, dtb_ref, conv0_ref,
         q_ref, k2_ref, v2_ref, z_ref, xs_ref, bc_ref, dt_ref, knew_ref, vnew_ref, convst_ref,
         xbc_s) = refs
    else:
        (x_ref, gmix_ref, win_ref, m128_ref, qn_ref, kn_ref, cos_ref, sin_ref,
         convw_ref, convb_ref, dtb_ref,
         q_ref, k2_ref, v2_ref, z_ref, xs_ref, bc_ref, dt_ref, knew_ref, vnew_ref, convst_ref,
         xbc_s) = refs
        conv0_ref = None
    l = pl.program_id(1)

    @pl.when(l == 0)
    def _():
        if has_state:
            xbc_s[0:carry_rows, :] = conv0_ref[...]
        else:
            xbc_s[0:carry_rows, :] = jnp.zeros((carry_rows, CONV_DIM), F32)

    x = x_ref[0]
    u = _rms_rows(x, gmix_ref[...]).astype(BF16)

    cw = 512
    for j in range(CONV_DIM // cw):
        c0 = COL_XBC + j * cw
        xbc_s[carry_rows:carry_rows + tm, j * cw:(j + 1) * cw] = _dot(u, win_ref[:, c0:c0 + cw])
    z_ref[0] = _dot(u, win_ref[:, COL_Z:COL_Z + SSM_WIDTH])
    heads = _dot(u, win_ref[:, 0:HEAD_COLS])

    m128 = m128_ref[...]
    cos = cos_ref[...]
    sin = sin_ref[...]
    lane = lax.broadcasted_iota(jnp.int32, (tm, LANES), 1)
    low_half = (lane % HEAD_DIM) < (HEAD_DIM // 2)

    for j in range(ATTN_WIDTH // LANES):
        qj = heads[:, COL_Q + j * LANES:COL_Q + (j + 1) * LANES]
        qj = _head_norm_rope(qj, m128, qn_ref[...], cos, sin, low_half)
        q_ref[0, :, j * LANES:(j + 1) * LANES] = (qj * ATTN_SCALE).astype(BF16)

    k = _head_norm_rope(heads[:, COL_K:COL_K + KV_WIDTH], m128, kn_ref[...], cos, sin, low_half)
    v = heads[:, COL_V:COL_V + KV_WIDTH]
    k2_ref[0, :, 0:LANES] = k.astype(BF16)
    k2_ref[0, :, LANES:2 * LANES] = pltpu.roll(k, HEAD_DIM, 1).astype(BF16)
    v2_ref[0, :, 0:LANES] = v.astype(BF16)
    v2_ref[0, :, LANES:2 * LANES] = pltpu.roll(v, HEAD_DIM, 1).astype(BF16)
    knew_ref[0] = k[tm - n_keep:, :]
    vnew_ref[0] = v[tm - n_keep:, :]

    dt_ref[0] = _softplus(heads[:, COL_DT:COL_DT + LANES] + dtb_ref[...])

    for j in range(CONV_DIM // cw):
        cs = slice(j * cw, (j + 1) * cw)
        w = [convw_ref[i:i + 1, cs] for i in range(CONV_W)]
        bias = convb_ref[:, cs]
        if shift % SUBLANES == 0:
            acc = bias
            for i in range(CONV_W):
                off = carry_rows - (CONV_W - 1 - i) * shift
                acc = acc + xbc_s[off:off + tm, cs] * w[i]
            act = _silu(acc)
        else:
            cur = xbc_s[carry_rows:carry_rows + tm, cs]
            acc = cur * w[0]
            for i in range(1, CONV_W):
                acc = pltpu.roll(acc, shift, 0) + cur * w[i]
            head = bias
            for i in range(CONV_W):
                off = carry_rows - (CONV_W - 1 - i) * shift
                head = head + xbc_s[off:off + SUBLANES, cs] * w[i]
            act = _silu(jnp.concatenate([head, acc[SUBLANES:, :] + bias], axis=0))
        if j * cw < SSM_WIDTH:
            xs_ref[0, :, cs] = act
        else:
            bc_ref[0] = act.astype(BF16)

    convst_ref[0] = xbc_s[tm:tm + carry_rows, :]
    xbc_s[0:carry_rows, :] = xbc_s[tm:tm + carry_rows, :]


def _inproj(x, gmix, win, m128, qn, kn, cos, sin, convw, convb, dtb, conv0, *, tm, carry_rows, shift, n_keep):
    nb, seq, _ = x.shape
    n_l = seq // tm
    has_state = conv0 is not None
    const = lambda shape: pl.BlockSpec(shape, lambda b, l: (0,) * len(shape))
    row = lambda w: pl.BlockSpec((1, tm, w), lambda b, l: (b, l, 0))
    in_specs = [row(D_MODEL), const((1, D_MODEL)), const((D_MODEL, PROJ_PAD)), const((2 * LANES, LANES)),
                const((1, LANES)), const((1, LANES)),
                pl.BlockSpec((tm, LANES), lambda b, l: (l, 0)), pl.BlockSpec((tm, LANES), lambda b, l: (l, 0)),
                const((CONV_W, CONV_DIM)), const((1, CONV_DIM)), const((1, LANES))]
    args = [x, gmix, win, m128, qn, kn, cos, sin, convw, convb, dtb]
    if has_state:
        in_specs.append(const((carry_rows, CONV_DIM)))
        args.append(conv0)
    last = lambda rows, w: pl.BlockSpec((1, rows, w), lambda b, l: (b, 0, 0))
    out_specs = [row(ATTN_WIDTH), row(2 * KV_WIDTH), row(2 * KV_WIDTH), row(SSM_WIDTH), row(SSM_WIDTH),
                 row(CONV_DIM - SSM_WIDTH), row(LANES),
                 last(n_keep, KV_WIDTH), last(n_keep, KV_WIDTH), last(carry_rows, CONV_DIM)]
    out_shape = [jax.ShapeDtypeStruct((nb, seq, ATTN_WIDTH), BF16),
                 jax.ShapeDtypeStruct((nb, seq, 2 * KV_WIDTH), BF16),
                 jax.ShapeDtypeStruct((nb, seq, 2 * KV_WIDTH), BF16),
                 jax.ShapeDtypeStruct((nb, seq, SSM_WIDTH), F32),
                 jax.ShapeDtypeStruct((nb, seq, SSM_WIDTH), F32),
                 jax.ShapeDtypeStruct((nb, seq, CONV_DIM - SSM_WIDTH), BF16),
                 jax.ShapeDtypeStruct((nb, seq, LANES), F32),
                 jax.ShapeDtypeStruct((nb, n_keep, KV_WIDTH), F32),
                 jax.ShapeDtypeStruct((nb, n_keep, KV_WIDTH), F32),
                 jax.ShapeDtypeStruct((nb, carry_rows, CONV_DIM), F32)]
    kern = functools.partial(_inproj_kernel, tm=tm, carry_rows=carry_rows, shift=shift, n_keep=n_keep,
                             has_state=has_state)
    return pl.pallas_call(
        kern, grid=(nb, n_l), in_specs=in_specs, out_specs=out_specs, out_shape=out_shape,
        scratch_shapes=[pltpu.VMEM((carry_rows + tm, CONV_DIM), F32)],
        compiler_params=pltpu.CompilerParams(dimension_semantics=("arbitrary", "arbitrary"),
                                             vmem_limit_bytes=VMEM_LIMIT),
        name="inproj",
    )(*args)


def _attn_kernel(sinks_ref, q_ref, kc_ref, kp_ref, vc_ref, vp_ref, o_ref, *, tq):
    i = pl.program_id(1)
    nblk = tq // WINDOW
    lane = lax.broadcasted_iota(jnp.int32, (WINDOW, LANES), 1)
    low = lane < HEAD_DIM
    qi = lax.broadcasted_iota(jnp.int32, (WINDOW, 2 * WINDOW), 0)
    kj = lax.broadcasted_iota(jnp.int32, (WINDOW, 2 * WINDOW), 1)
    diff = qi + WINDOW - kj
    band = (diff >= 0) & (diff < WINDOW)
    zero_bf = jnp.zeros((WINDOW, LANES), BF16)
    for blk in range(nblk):
        r0 = blk * WINDOW
        if blk == 0:
            k_prev, v_prev = kp_ref[0], vp_ref[0]
            mask = band & ((i > 0) | (kj >= WINDOW))
        else:
            k_prev = kc_ref[0, r0 - WINDOW:r0, :]
            v_prev = vc_ref[0, r0 - WINDOW:r0, :]
            mask = band
        kk = jnp.concatenate([k_prev, kc_ref[0, r0:r0 + WINDOW, :]], axis=0)
        vv = jnp.concatenate([v_prev, vc_ref[0, r0:r0 + WINDOW, :]], axis=0)
        for pair in range(N_Q_HEADS // 2):
            g = (2 * pair) // (N_Q_HEADS // N_KV_HEADS)
            qp = q_ref[0, r0:r0 + WINDOW, pair * LANES:(pair + 1) * LANES]
            outs = []
            for e in range(2):
                h = 2 * pair + e
                var = 0 if g == e else 1
                qh = jnp.where(low if e == 0 else ~low, qp, zero_bf)
                s = _dot_nt(qh, kk[:, var * LANES:(var + 1) * LANES])
                s = jnp.where(mask, s, -jnp.inf)
                sk = sinks_ref[h]
                m = jnp.maximum(jnp.max(s, axis=-1, keepdims=True), sk)
                p = jnp.exp(s - m)
                den = jnp.sum(p, axis=-1, keepdims=True) + jnp.exp(sk - m)
                o = _dot(p.astype(BF16), vv[:, var * LANES:(var + 1) * LANES])
                outs.append(o / den)
            o_ref[0, r0:r0 + WINDOW, pair * LANES:(pair + 1) * LANES] = (
                jnp.where(low, outs[0], outs[1]).astype(BF16))


def _attention(sinks, q, k2, v2, *, tq):
    nb, seq, _ = q.shape
    r = tq // WINDOW
    cur = lambda w: pl.BlockSpec((1, tq, w), lambda b, i: (b, i, 0))
    prev = lambda w: pl.BlockSpec((1, WINDOW, w), lambda b, i: (b, jnp.maximum(i * r - 1, 0), 0))
    return pl.pallas_call(
        functools.partial(_attn_kernel, tq=tq),
        grid=(nb, seq // tq),
        in_specs=[pl.BlockSpec(memory_space=pltpu.SMEM),
                  cur(ATTN_WIDTH), cur(2 * KV_WIDTH), prev(2 * KV_WIDTH), cur(2 * KV_WIDTH), prev(2 * KV_WIDTH)],
        out_specs=cur(ATTN_WIDTH),
        out_shape=jax.ShapeDtypeStruct((nb, seq, ATTN_WIDTH), BF16),
        compiler_params=pltpu.CompilerParams(dimension_semantics=("arbitrary", "arbitrary"),
                                             vmem_limit_bytes=VMEM_LIMIT),
        name="swa_attn",
    )(sinks, q, k2, k2, v2, v2)


def _ssd_kernel(xs_ref, bc_ref, dt_ref, z_ref, a_ref, dskip_ref, gnorm_ref, tri_ref,
                y_ref, hT_ref, h_s, y_s, *, tc):
    l = pl.program_id(1)
    n_l = pl.num_programs(1)
    gw = SSM_WIDTH // N_SSM_GROUPS
    hpg = N_SSM_HEADS // N_SSM_GROUPS

    @pl.when(l == 0)
    def _():
        h_s[...] = jnp.zeros(h_s.shape, F32)

    lane = lax.broadcasted_iota(jnp.int32, (SSD_CHUNK, LANES), 1)
    low = lane < SSM_HEAD_DIM
    qi = lax.broadcasted_iota(jnp.int32, (SSD_CHUNK, SSD_CHUNK), 0)
    sj = lax.broadcasted_iota(jnp.int32, (SSD_CHUNK, SSD_CHUNK), 1)
    causal = sj <= qi
    a_neg = -jnp.exp(a_ref[...])
    tri = tri_ref[...]
    zero_bf = jnp.zeros((SSD_CHUNK, LANES), BF16)

    for c in range(tc // SSD_CHUNK):
        rows = slice(c * SSD_CHUNK, (c + 1) * SSD_CHUNK)
        dt = dt_ref[0, rows, :]
        hi, mid, lo = _split3(dt * a_neg)
        acum = (_dot(tri, hi) + _dot(tri, mid) + _dot(tri, lo)) * LOG2E
        acum_t = acum.T
        dt_t = dt.T
        a_end_t = acum_t[:, SSD_CHUNK - 1:SSD_CHUNK]
        w_end_t = dt_t * jnp.exp2(a_end_t - acum_t)
        dec_t = jnp.exp2(a_end_t)
        dec_t = jnp.broadcast_to(dec_t, (LANES, LANES))
        row_t = acum_t - jnp.log2(dt_t)
        for g in range(N_SSM_GROUPS):
            b_g = bc_ref[0, rows, g * D_STATE:(g + 1) * D_STATE]
            c_g = bc_ref[0, rows, (N_SSM_GROUPS + g) * D_STATE:(N_SSM_GROUPS + g + 1) * D_STATE]
            cb = _dot_nt(c_g, b_g)
            b_gt = b_g.astype(F32).T
            y_inter = _dot(c_g, h_s[g].astype(BF16))
            for pr in range(hpg // 2):
                lhs_y, lhs_s, escale, dsc = [], [], [], []
                for e in range(2):
                    h = g * hpg + 2 * pr + e
                    col = jnp.broadcast_to(acum[:, h:h + 1], (SSD_CHUNK, SSD_CHUNK))
                    decay_dt = jnp.where(causal, jnp.exp2(col - row_t[h:h + 1, :]), 0.0)
                    lhs_y.append((cb * decay_dt).astype(BF16))
                    lhs_s.append((b_gt * w_end_t[h:h + 1, :]).astype(BF16))
                    escale.append(jnp.exp2(col))
                    dsc.append(dec_t[h:h + 1, :])
                c0 = g * gw + pr * LANES
                x_pair = xs_ref[0, rows, c0:c0 + LANES]
                x_bf = x_pair.astype(BF16)
                rhs = jnp.concatenate([jnp.where(low, x_bf, zero_bf), jnp.where(low, zero_bf, x_bf)], axis=0)
                y_in = _dot(jnp.concatenate(lhs_y, axis=1), rhs)
                st = _dot(jnp.concatenate(lhs_s, axis=1), rhs)
                lc = pr * LANES
                y = y_in + y_inter[:, lc:lc + LANES] * jnp.where(low, escale[0], escale[1])
                y = y + dskip_ref[:, c0:c0 + LANES] * x_pair
                zz = z_ref[0, rows, c0:c0 + LANES]
                y_s[rows, c0:c0 + LANES] = y * _silu(zz)
                h_s[g, :, lc:lc + LANES] = h_s[g, :, lc:lc + LANES] * jnp.where(low, dsc[0], dsc[1]) + st
        for g in range(N_SSM_GROUPS):
            yg = y_s[rows, g * gw:(g + 1) * gw]
            yg = yg * lax.rsqrt(jnp.mean(yg * yg, axis=-1, keepdims=True) + EPS)
            y_ref[0, rows, g * gw:(g + 1) * gw] = (yg * gnorm_ref[:, g * gw:(g + 1) * gw]).astype(BF16)

    @pl.when(l == n_l - 1)
    def _():
        hT_ref[0] = h_s[...]


def _ssd(xs, bc, dt, z, a_pad, dskip, gnorm, tri, *, tc):
    nb, seq, _ = xs.shape
    row = lambda w: pl.BlockSpec((1, tc, w), lambda b, l: (b, l, 0))
    const = lambda shape: pl.BlockSpec(shape, lambda b, l: (0,) * len(shape))
    gw = SSM_WIDTH // N_SSM_GROUPS
    return pl.pallas_call(
        functools.partial(_ssd_kernel, tc=tc),
        grid=(nb, seq // tc),
        in_specs=[row(SSM_WIDTH), row(CONV_DIM - SSM_WIDTH), row(LANES), row(SSM_WIDTH),
                  const((1, LANES)), const((1, SSM_WIDTH)), const((1, SSM_WIDTH)), const((SSD_CHUNK, SSD_CHUNK))],
        out_specs=[row(SSM_WIDTH),
                   pl.BlockSpec((1, N_SSM_GROUPS, D_STATE, gw), lambda b, l: (b, 0, 0, 0))],
        out_shape=[jax.ShapeDtypeStruct((nb, seq, SSM_WIDTH), BF16),
                   jax.ShapeDtypeStruct((nb, N_SSM_GROUPS, D_STATE, gw), F32)],
        scratch_shapes=[pltpu.VMEM((N_SSM_GROUPS, D_STATE, gw), F32), pltpu.VMEM((tc, SSM_WIDTH), F32)],
        compiler_params=pltpu.CompilerParams(dimension_semantics=("arbitrary", "arbitrary"),
                                             vmem_limit_bytes=VMEM_LIMIT),
        name="ssd_scan",
    )(xs, bc, dt, z, a_pad, dskip, gnorm, tri)


def _outffn_kernel(x_ref, oa_ref, ys_ref, wo_ref, gffn_ref, wgu_ref, wd_ref, o_ref, *, ffc):
    x = x_ref[...]
    xm = x + _dot(oa_ref[...], wo_ref[0:ATTN_WIDTH, :]) + _dot(ys_ref[...], wo_ref[ATTN_WIDTH:, :])
    hn = _rms_rows(xm, gffn_ref[...]).astype(BF16)
    acc = jnp.zeros_like(xm)
    for j in range(D_FF // ffc):
        g = _dot(hn, wgu_ref[:, j * ffc:(j + 1) * ffc])
        up = _dot(hn, wgu_ref[:, D_FF + j * ffc:D_FF + (j + 1) * ffc])
        acc = acc + _dot((_silu(g) * up).astype(BF16), wd_ref[j * ffc:(j + 1) * ffc, :])
    o_ref[...] = xm + acc


def _outffn(x, oa, ys, wo, gffn, wgu, wd, *, tm, ffc):
    rows = x.shape[0]
    row = lambda w: pl.BlockSpec((tm, w), lambda i: (i, 0))
    const = lambda shape: pl.BlockSpec(shape, lambda i: (0,) * len(shape), pipeline_mode=pl.Buffered(1))
    return pl.pallas_call(
        functools.partial(_outffn_kernel, ffc=ffc),
        grid=(rows // tm,),
        in_specs=[row(D_MODEL), row(ATTN_WIDTH), row(SSM_WIDTH),
                  const((ATTN_WIDTH + SSM_WIDTH, D_MODEL)), const((1, D_MODEL)),
                  const((D_MODEL, 2 * D_FF)), const((D_FF, D_MODEL))],
        out_specs=row(D_MODEL),
        out_shape=jax.ShapeDtypeStruct((rows, D_MODEL), F32),
        compiler_params=pltpu.CompilerParams(dimension_semantics=("arbitrary",),
                                             vmem_limit_bytes=VMEM_LIMIT),
        name="outproj_ffn",
    )(x, oa, ys, wo, gffn, wgu, wd)


T_PAD = SUBLANES


def _attn_dec_kernel(q_ref, kc_ref, kn_ref, vc_ref, vn_ref, sink_ref, o_ref, *, t_new):
    q = q_ref[...]
    nbat, nq, _ = q.shape
    qpk = N_Q_HEADS // N_KV_HEADS
    s_c = jnp.einsum('bqd,bkd->bqk', q, kc_ref[...], preferred_element_type=F32)
    s_n = jnp.einsum('bqd,bkd->bqk', q, kn_ref[...], preferred_element_type=F32)
    n_c = s_c.shape[-1]
    t_c = lax.broadcasted_iota(jnp.int32, (nbat, nq, n_c), 1) // qpk
    j_c = lax.broadcasted_iota(jnp.int32, (nbat, nq, n_c), 2)
    s_c = jnp.where(j_c > t_c + (n_c - WINDOW), s_c, -jnp.inf)
    t_n = lax.broadcasted_iota(jnp.int32, (nbat, nq, T_PAD), 1) // qpk
    j_n = lax.broadcasted_iota(jnp.int32, (nbat, nq, T_PAD), 2)
    s_n = jnp.where((j_n <= t_n) & (j_n < t_new), s_n, -jnp.inf)
    sk = sink_ref[...]
    m = jnp.maximum(jnp.maximum(jnp.max(s_c, axis=-1, keepdims=True), jnp.max(s_n, axis=-1, keepdims=True)), sk)
    p_c = jnp.exp(s_c - m)
    p_n = jnp.exp(s_n - m)
    den = jnp.sum(p_c, axis=-1, keepdims=True) + jnp.sum(p_n, axis=-1, keepdims=True) + jnp.exp(sk - m)
    o = (jnp.einsum('bqk,bkd->bqd', p_c.astype(BF16), vc_ref[...], preferred_element_type=F32)
         + jnp.einsum('bqk,bkd->bqd', p_n.astype(BF16), vn_ref[...], preferred_element_type=F32))
    o_ref[...] = (o / den).astype(BF16)


def _attention_dec(q, kc, kn, vc, vn, sink_rows, *, t_new, bb):
    nbat, nq, hd = q.shape
    n_c = kc.shape[1]
    blk = lambda r, w: pl.BlockSpec((bb, r, w), lambda i: (i, 0, 0))
    return pl.pallas_call(
        functools.partial(_attn_dec_kernel, t_new=t_new),
        grid=(nbat // bb,),
        in_specs=[blk(nq, hd), blk(n_c, hd), blk(T_PAD, hd), blk(n_c, hd), blk(T_PAD, hd), blk(nq, 1)],
        out_specs=blk(nq, hd),
        out_shape=jax.ShapeDtypeStruct((nbat, nq, hd), BF16),
        compiler_params=pltpu.CompilerParams(dimension_semantics=("arbitrary",), vmem_limit_bytes=VMEM_LIMIT),
        name="swa_attn_decode",
    )(q, kc, kn, vc, vn, sink_rows)


def _ssd_dec_kernel(xs_ref, bc_ref, dt_ref, z_ref, h0_ref, a_ref, dskip_ref, gnorm_ref, tri_ref, exp_ref,
                    y_ref, h_ref, *, bb, t_new):
    gw = SSM_WIDTH // N_SSM_GROUPS
    hpg = N_SSM_HEADS // N_SSM_GROUPS
    a_neg = -jnp.exp(a_ref[...])
    tri = tri_ref[...]
    expand = exp_ref[...]
    t_row = lax.broadcasted_iota(jnp.int32, (T_PAD, gw), 0)

    def widen(v):
        hi, mid, lo = _split3(v)
        return _dot(hi, expand) + _dot(mid, expand) + _dot(lo, expand)

    def seq_body(i, carry):
        dt = dt_ref[i]
        hi, mid, lo = _split3(dt * a_neg)
        acum = _dot(tri, hi) + _dot(tri, mid) + _dot(tri, lo)
        a_w = widen(acum)
        xs = xs_ref[i]
        xdt = xs * widen(dt)
        a_end = a_w[T_PAD - 1:T_PAD, :]
        xw = (xdt * jnp.exp(a_end - a_w)).astype(BF16)
        ea = jnp.exp(a_w)
        dec = jnp.broadcast_to(jnp.exp(acum[T_PAD - 1:T_PAD, :]), (LANES, LANES)).T
        bc = bc_ref[i]
        zz = z_ref[i]
        for g in range(N_SSM_GROUPS):
            b_g = bc[:, g * D_STATE:(g + 1) * D_STATE]
            c_g = bc[:, (N_SSM_GROUPS + g) * D_STATE:(N_SSM_GROUPS + g + 1) * D_STATE]
            cb = _dot_nt(c_g, b_g)
            ls = slice(g * gw, (g + 1) * gw)
            a_g = a_w[:, ls]
            y = jnp.zeros((T_PAD, gw), F32)
            for s in range(t_new):
                w = jnp.where(t_row >= s, jnp.exp(a_g - a_g[s:s + 1, :]), 0.0)
                y = y + (w * cb[:, s:s + 1]) * xdt[s:s + 1, ls]
            h0g = h0_ref[i, g * hpg:(g + 1) * hpg].reshape(gw, D_STATE)
            y = y + _dot_nt(c_g, h0g.astype(BF16)) * ea[:, ls]
            y = y + dskip_ref[:, ls] * xs[:, ls]
            y = y * _silu(zz[:, ls])
            y = y * lax.rsqrt(jnp.mean(y * y, axis=-1, keepdims=True) + EPS)
            y_ref[i, :, ls] = (y * gnorm_ref[:, ls]).astype(BF16)
            delta = lax.dot_general(xw[:, ls], b_g, (((0,), (0,)), ((), ())), preferred_element_type=F32)
            for hl in range(hpg):
                h = g * hpg + hl
                h_ref[i, h] = (h0_ref[i, h] * dec[h:h + 1, :]
                               + delta[hl * SSM_HEAD_DIM:(hl + 1) * SSM_HEAD_DIM, :])
        return carry

    lax.fori_loop(0, bb, seq_body, 0)


def _ssd_dec(xs3, bc3, dt3, z3, h0, a_pad, dskip, gnorm, tri8, expand, *, bb, t_new):
    nb = xs3.shape[0]
    blk = lambda w: pl.BlockSpec((bb, T_PAD, w), lambda i: (i, 0, 0))
    const = lambda shape: pl.BlockSpec(shape, lambda i: (0,) * len(shape))
    st = pl.BlockSpec((bb, N_SSM_HEADS, SSM_HEAD_DIM, D_STATE), lambda i: (i, 0, 0, 0))
    return pl.pallas_call(
        functools.partial(_ssd_dec_kernel, bb=bb, t_new=t_new),
        grid=(nb // bb,),
        in_specs=[blk(SSM_WIDTH), blk(CONV_DIM - SSM_WIDTH), blk(LANES), blk(SSM_WIDTH), st,
                  const((1, LANES)), const((1, SSM_WIDTH)), const((1, SSM_WIDTH)),
                  const((T_PAD, T_PAD)), const((LANES, SSM_WIDTH))],
        out_specs=[blk(SSM_WIDTH), st],
        out_shape=[jax.ShapeDtypeStruct((nb, T_PAD, SSM_WIDTH), BF16),
                   jax.ShapeDtypeStruct(h0.shape, F32)],
        compiler_params=pltpu.CompilerParams(dimension_semantics=("arbitrary",), vmem_limit_bytes=VMEM_LIMIT),
        name="ssd_decode",
    )(xs3, bc3, dt3, z3, h0, a_pad, dskip, gnorm, tri8, expand)


def _rope_tables(pos):
    half = HEAD_DIM // 2
    inv = ROPE_THETA ** (-jnp.arange(half, dtype=F32) / half)
    ang = pos.astype(F32)[:, None] * inv[None, :]
    cos = jnp.tile(jnp.cos(ang), (1, LANES // half))
    sin = jnp.sin(ang)
    sin_signed = jnp.tile(jnp.concatenate([-sin, sin], axis=1), (1, LANES // HEAD_DIM))
    return cos, sin_signed


def _layer_consts(l, p):
    w = p['w_in'][l]
    o_z = ATTN_WIDTH + 2 * KV_WIDTH
    o_xbc = o_z + SSM_WIDTH
    o_dt = o_xbc + CONV_DIM
    w_in = jnp.concatenate(
        [w[:, :o_z], jnp.pad(w[:, o_dt:], ((0, 0), (0, LANES - N_SSM_HEADS))), w[:, o_z:o_dt]], axis=1).astype(BF16)
    return dict(
        gmix=p['norm_mix'][l][None, :],
        win=w_in,
        qn=jnp.tile(p['q_norm'][l], LANES // HEAD_DIM)[None, :],
        kn=jnp.tile(p['k_norm'][l], LANES // HEAD_DIM)[None, :],
        convw=p['conv_w'][l],
        convb=p['conv_b'][l][None, :],
        dtb=jnp.pad(p['dt_bias'][l], (0, LANES - N_SSM_HEADS))[None, :],
        a_pad=jnp.pad(p['a_log'][l], (0, LANES - N_SSM_HEADS))[None, :],
        dskip=jnp.repeat(p['d_skip'][l], SSM_HEAD_DIM)[None, :],
        gnorm=p['ssm_norm'][l][None, :],
        sinks=p['sinks'][l],
        wo=p['w_out'][l].astype(BF16),
        gffn=p['norm_ffn'][l][None, :],
        wgu=p['w_gate_up'][l].astype(BF16),
        wd=p['w_down'][l].astype(BF16),
    )


def _state_from_t(h_t):
    nb = h_t.shape[0]
    hpg = N_SSM_HEADS // N_SSM_GROUPS
    h = h_t.reshape(nb, N_SSM_GROUPS, D_STATE, hpg, SSM_HEAD_DIM)
    return jnp.transpose(h, (0, 1, 3, 4, 2)).reshape(nb, N_SSM_HEADS, SSM_HEAD_DIM, D_STATE)


def _prompt_layer(x, c, shared, *, tm):
    nb, seq, _ = x.shape
    (q, k2, v2, z, xs, bc, dt, knew, vnew, convst) = _inproj(
        x, c['gmix'], c['win'], shared['m128'], c['qn'], c['kn'], shared['cos_p'], shared['sin_p'],
        c['convw'], c['convb'], c['dtb'], None, tm=tm, carry_rows=SUBLANES, shift=1, n_keep=min(WINDOW, seq))
    oa = _attention(c['sinks'], q, k2, v2, tq=tm)
    ys, h_t = _ssd(xs, bc, dt, z, c['a_pad'], c['dskip'], c['gnorm'], shared['tri'], tc=tm)
    rows = nb * seq
    xo = _outffn(x.reshape(rows, D_MODEL), oa.reshape(rows, ATTN_WIDTH), ys.reshape(rows, SSM_WIDTH),
                 c['wo'], c['gffn'], c['wgu'], c['wd'], tm=tm, ffc=D_FF // 2)
    n_keep = knew.shape[1]
    return (xo.reshape(nb, seq, D_MODEL),
            knew.reshape(nb, n_keep, N_KV_HEADS, HEAD_DIM), vnew.reshape(nb, n_keep, N_KV_HEADS, HEAD_DIM),
            convst[:, SUBLANES - (CONV_W - 1):, :], _state_from_t(h_t))


def _sample_layer(x_t, cache_k, cache_v, conv0, h0, c, shared, *, n_seq, t_new):
    rows = t_new * n_seq
    qpk = N_Q_HEADS // N_KV_HEADS
    n_c = cache_k.shape[1]
    conv0_t = jnp.transpose(conv0, (1, 0, 2)).reshape((CONV_W - 1) * n_seq, CONV_DIM)
    (q, _, _, z, xs, bc, dt, knew, vnew, convst) = _inproj(
        x_t, c['gmix'], c['win'], shared['m128'], c['qn'], c['kn'], shared['cos_s'], shared['sin_s'],
        c['convw'], c['convb'], c['dtb'], conv0_t, tm=rows, carry_rows=(CONV_W - 1) * n_seq, shift=n_seq,
        n_keep=rows)

    def to_bg(a, t, inner):
        a = a.reshape(t, n_seq, N_KV_HEADS, inner, HEAD_DIM)
        return jnp.transpose(a, (1, 2, 0, 3, 4)).reshape(n_seq * N_KV_HEADS, t * inner, HEAD_DIM)

    def pad_t(a):
        return jnp.pad(a, ((0, 0), (0, T_PAD - a.shape[1]), (0, 0)))

    def cache_bg(a):
        return jnp.transpose(a, (0, 2, 1, 3)).reshape(n_seq * N_KV_HEADS, n_c, HEAD_DIM).astype(BF16)

    q_bg = to_bg(q[0], t_new, qpk)
    kn_bg = pad_t(to_bg(knew[0], t_new, 1)).astype(BF16)
    vn_bg = pad_t(to_bg(vnew[0], t_new, 1)).astype(BF16)
    sink_rows = jnp.tile(c['sinks'].reshape(N_KV_HEADS, 1, qpk), (n_seq, t_new, 1)).reshape(
        n_seq * N_KV_HEADS, t_new * qpk, 1)
    bb_a = min(64, n_seq * N_KV_HEADS)
    o_bg = _attention_dec(q_bg, cache_bg(cache_k), kn_bg, cache_bg(cache_v), vn_bg, sink_rows,
                          t_new=t_new, bb=bb_a)
    oa = jnp.transpose(o_bg.reshape(n_seq, N_KV_HEADS, t_new, qpk, HEAD_DIM), (2, 0, 1, 3, 4)).reshape(
        rows, ATTN_WIDTH)

    def to_seq(a):
        a = jnp.transpose(a[0].reshape(t_new, n_seq, a.shape[-1]), (1, 0, 2))
        return pad_t(a)

    y3, h_new = _ssd_dec(to_seq(xs), to_seq(bc), to_seq(dt), to_seq(z), h0, c['a_pad'], c['dskip'], c['gnorm'],
                         shared['tri8'], shared['expand'], bb=min(8, n_seq), t_new=t_new)
    ys = jnp.transpose(y3[:, :t_new], (1, 0, 2)).reshape(rows, SSM_WIDTH)

    xo = _outffn(x_t[0], oa, ys, c['wo'], c['gffn'], c['wgu'], c['wd'], tm=min(512, rows), ffc=D_FF // 2)

    def new_rows(a):
        return jnp.transpose(a.reshape(t_new, n_seq, N_KV_HEADS, HEAD_DIM), (1, 0, 2, 3))

    win_k = jnp.concatenate([cache_k, new_rows(knew[0])], axis=1)[:, t_new:]
    win_v = jnp.concatenate([cache_v, new_rows(vnew[0])], axis=1)[:, t_new:]
    conv_new = jnp.transpose(convst[0].reshape(CONV_W - 1, n_seq, CONV_DIM), (1, 0, 2))
    return xo[None], win_k, win_v, conv_new, h_new


def kernel(x_prompt, x_sample, cache_win_k, cache_win_v, state_conv, state_ssm,
           norm_mix, w_in, q_norm, k_norm, sinks, conv_w, conv_b, dt_bias, a_log,
           d_skip, ssm_norm, w_out, norm_ffn, w_gate_up, w_down):
    p = dict(norm_mix=norm_mix, w_in=w_in, q_norm=q_norm, k_norm=k_norm, sinks=sinks, conv_w=conv_w,
             conv_b=conv_b, dt_bias=dt_bias, a_log=a_log, d_skip=d_skip, ssm_norm=ssm_norm, w_out=w_out,
             norm_ffn=norm_ffn, w_gate_up=w_gate_up, w_down=w_down)
    depth = w_in.shape[0]
    seq = x_prompt.shape[1]
    cos_p, sin_p = _rope_tables(jnp.arange(seq, dtype=jnp.int32))
    half_blk = jnp.arange(LANES) // HEAD_DIM
    m128 = (jnp.where(half_blk[:, None] == half_blk[None, :], 1.0 / HEAD_DIM, 0.0)).astype(BF16)
    m128 = jnp.concatenate([m128, m128], axis=0)
    tri = (jnp.arange(SSD_CHUNK)[:, None] >= jnp.arange(SSD_CHUNK)[None, :]).astype(BF16)
    n_seq, t_new, _ = x_sample.shape
    pos_s = PAST_LEN + jnp.repeat(jnp.arange(t_new, dtype=jnp.int32), n_seq)
    cos_s, sin_s = _rope_tables(pos_s)
    tri8 = (jnp.arange(T_PAD)[:, None] >= jnp.arange(T_PAD)[None, :]).astype(BF16)
    expand = (jnp.arange(LANES)[:, None] == (jnp.arange(SSM_WIDTH) // SSM_HEAD_DIM)[None, :]).astype(BF16)
    shared = dict(cos_p=cos_p, sin_p=sin_p, cos_s=cos_s, sin_s=sin_s, m128=m128, tri=tri, tri8=tri8,
                  expand=expand)
    tm = min(512, seq)

    xp = x_prompt
    xs = jnp.transpose(x_sample, (1, 0, 2)).reshape(1, t_new * n_seq, D_MODEL)
    pk, pv, pc, ph = [], [], [], []
    sk, sv, sc, sh = [], [], [], []
    for l in range(depth):
        c = _layer_consts(l, p)
        xp, k1, v1, c1, h1 = _prompt_layer(xp, c, shared, tm=tm)
        xs, k2, v2, c2, h2 = _sample_layer(xs, cache_win_k[l], cache_win_v[l], state_conv[l], state_ssm[l],
                                           c, shared, n_seq=n_seq, t_new=t_new)
        pk.append(k1); pv.append(v1); pc.append(c1); ph.append(h1)
        sk.append(k2); sv.append(v2); sc.append(c2); sh.append(h2)
    ys = jnp.transpose(xs.reshape(t_new, n_seq, D_MODEL), (1, 0, 2))
    return (xp, ys, jnp.stack(pk), jnp.stack(pv), jnp.stack(pc), jnp.stack(ph),
            jnp.stack(sk), jnp.stack(sv), jnp.stack(sc), jnp.stack(sh))
```

```python
import functools
import math

import jax
import jax.numpy as jnp
from jax import lax
from jax.experimental import pallas as pl
from jax.experimental.pallas import tpu as pltpu

F32 = jnp.float32
BF16 = jnp.bfloat16

D_MODEL = 1024
HEAD_DIM = 64
N_Q_HEADS = 8
N_KV_HEADS = 2
ATTN_WIDTH = N_Q_HEADS * HEAD_DIM
KV_WIDTH = N_KV_HEADS * HEAD_DIM
WINDOW = 128
ROPE_THETA = 10000.0
ATTN_SCALE = HEAD_DIM ** -0.5
SSM_HEAD_DIM = 64
N_SSM_HEADS = 16
SSM_WIDTH = N_SSM_HEADS * SSM_HEAD_DIM
N_SSM_GROUPS = 2
D_STATE = 128
CONV_W = 4
CONV_DIM = SSM_WIDTH + 2 * N_SSM_GROUPS * D_STATE
SSD_CHUNK = 128
D_FF = 2816
EPS = 1e-6
LOG2E = math.log2(math.e)
PAST_LEN = 16384

LANES = 128
SUBLANES = 8

COL_Q = 0
COL_K = COL_Q + ATTN_WIDTH
COL_V = COL_K + KV_WIDTH
COL_DT = COL_V + KV_WIDTH
COL_Z = COL_DT + LANES
COL_XBC = COL_Z + SSM_WIDTH
PROJ_PAD = COL_XBC + CONV_DIM
HEAD_COLS = COL_Z

VMEM_LIMIT = 56 * 1024 * 1024
FF_CHUNKS = (6 * 256, 5 * 256)


def _dot(a, b):
    return jnp.dot(a, b, preferred_element_type=F32)


def _dot_nt(a, b):
    return lax.dot_general(a, b, (((1,), (1,)), ((), ())), preferred_element_type=F32)


def _silu(x):
    return x / (1.0 + jnp.exp(-x))


def _softplus(x):
    return jnp.maximum(x, 0.0) + jnp.log1p(jnp.exp(-jnp.abs(x)))


def _rms_rows(x, g):
    return x * lax.rsqrt(jnp.mean(x * x, axis=-1, keepdims=True) + EPS) * g


def _split2(x):
    hi = x.astype(BF16)
    lo = (x - hi.astype(F32)).astype(BF16)
    return hi, lo


def _split3(x):
    hi = x.astype(BF16)
    r1 = x - hi.astype(F32)
    mid = r1.astype(BF16)
    lo = (r1 - mid.astype(F32)).astype(BF16)
    return hi, mid, lo


def _head_norm_rope(x, m128, g, cos, sin_signed, low_half):
    hi, lo = _split2(x * x)
    ms = _dot(jnp.concatenate([hi, lo], axis=1), m128)
    xn = x * lax.rsqrt(ms + EPS) * g
    rot = jnp.where(low_half, pltpu.roll(xn, LANES - HEAD_DIM // 2, 1),
                    pltpu.roll(xn, HEAD_DIM // 2, 1))
    return xn * cos + rot * sin_signed


def _inproj_kernel(*refs, tm, carry_rows, shift, n_keep, has_state):
    if has_state:
        (x_ref, gmix_ref, win_ref, m128_ref, qn_ref, kn_ref, cos_ref, sin_ref,
         convw_ref, convb_ref, dtb_ref, conv0_ref,
         q_ref, k2_ref, v2_ref, z_ref, xs_ref, bc_ref, dt_ref, knew_ref, vnew_ref, convst_ref,
         xbc_s) = refs
    else:
        (x_ref, gmix_ref, win_ref, m128_ref, qn_ref, kn_ref, cos_ref, sin_ref,
         convw_ref, convb_ref, dtb_ref,
         q_ref, k2_ref, v2_ref, z_ref, xs_ref, bc_ref, dt_ref, knew_ref, vnew_ref, convst_ref,
         xbc_s) = refs
        conv0_ref = None
    l = pl.program_id(1)

    @pl.when(l == 0)
    def _():
        if has_state:
            xbc_s[0:carry_rows, :] = conv0_ref[...]
        else:
            xbc_s[0:carry_rows, :] = jnp.zeros((carry_rows, CONV_DIM), F32)

    x = x_ref[0]
    u = _rms_rows(x, gmix_ref[...]).astype(BF16)

    cw = 512
    for j in range(CONV_DIM // cw):
        c0 = COL_XBC + j * cw
        xbc_s[carry_rows:carry_rows + tm, j * cw:(j + 1) * cw] = _dot(u, win_ref[:, c0:c0 + cw])
    z_ref[0] = _dot(u, win_ref[:, COL_Z:COL_Z + SSM_WIDTH])
    heads = _dot(u, win_ref[:, 0:HEAD_COLS])

    m128 = m128_ref[...]
    cos = cos_ref[...]
    sin = sin_ref[...]
    lane = lax.broadcasted_iota(jnp.int32, (tm, LANES), 1)
    low_half = (lane % HEAD_DIM) < (HEAD_DIM // 2)

    for j in range(ATTN_WIDTH // LANES):
        qj = heads[:, COL_Q + j * LANES:COL_Q + (j + 1) * LANES]
        qj = _head_norm_rope(qj, m128, qn_ref[...], cos, sin, low_half)
        q_ref[0, :, j * LANES:(j + 1) * LANES] = (qj * ATTN_SCALE).astype(BF16)

    k = _head_norm_rope(heads[:, COL_K:COL_K + KV_WIDTH], m128, kn_ref[...], cos, sin, low_half)
    v = heads[:, COL_V:COL_V + KV_WIDTH]
    k2_ref[0, :, 0:LANES] = k.astype(BF16)
    k2_ref[0, :, LANES:2 * LANES] = pltpu.roll(k, HEAD_DIM, 1).astype(BF16)
    v2_ref[0, :, 0:LANES] = v.astype(BF16)
    v2_ref[0, :, LANES:2 * LANES] = pltpu.roll(v, HEAD_DIM, 1).astype(BF16)
    knew_ref[0] = k[tm - n_keep:, :]
    vnew_ref[0] = v[tm - n_keep:, :]

    dt_ref[0] = _softplus(heads[:, COL_DT:COL_DT + LANES] + dtb_ref[...])

    for j in range(CONV_DIM // cw):
        cs = slice(j * cw, (j + 1) * cw)
        w = [convw_ref[i:i + 1, cs] for i in range(CONV_W)]
        bias = convb_ref[:, cs]
        if shift % SUBLANES == 0:
            acc = bias
            for i in range(CONV_W):
                off = carry_rows - (CONV_W - 1 - i) * shift
                acc = acc + xbc_s[off:off + tm, cs] * w[i]
            act = _silu(acc)
        else:
            cur = xbc_s[carry_rows:carry_rows + tm, cs]
            acc = cur * w[0]
            for i in range(1, CONV_W):
                acc = pltpu.roll(acc, shift, 0) + cur * w[i]
            head = bias
            for i in range(CONV_W):
                off = carry_rows - (CONV_W - 1 - i) * shift
                head = head + xbc_s[off:off + SUBLANES, cs] * w[i]
            act = _silu(jnp.concatenate([head, acc[SUBLANES:, :] + bias], axis=0))
        if j * cw < SSM_WIDTH:
            xs_ref[0, :, cs] = act
        else:
            bc_ref[0] = act.astype(BF16)

    convst_ref[0] = xbc_s[tm:tm + carry_rows, :]
    xbc_s[0:carry_rows, :] = xbc_s[tm:tm + carry_rows, :]


def _inproj(x, gmix, win, m128, qn, kn, cos, sin, convw, convb, dtb, conv0, *, tm, carry_rows, shift, n_keep):
    nb, seq, _ = x.shape
    n_l = seq // tm
    has_state = conv0 is not None
    const = lambda shape: pl.BlockSpec(shape, lambda b, l: (0,) * len(shape))
    row = lambda w: pl.BlockSpec((1, tm, w), lambda b, l: (b, l, 0))
    in_specs = [row(D_MODEL), const((1, D_MODEL)), const((D_MODEL, PROJ_PAD)), const((2 * LANES, LANES)),
                const((1, LANES)), const((1, LANES)),
                pl.BlockSpec((tm, LANES), lambda b, l: (l, 0)), pl.BlockSpec((tm, LANES), lambda b, l: (l, 0)),
                const((CONV_W, CONV_DIM)), const((1, CONV_DIM)), const((1, LANES))]
    args = [x, gmix, win, m128, qn, kn, cos, sin, convw, convb, dtb]
    if has_state:
        in_specs.append(const((carry_rows, CONV_DIM)))
        args.append(conv0)
    last = lambda rows, w: pl.BlockSpec((1, rows, w), lambda b, l: (b, 0, 0))
    out_specs = [row(ATTN_WIDTH), row(2 * KV_WIDTH), row(2 * KV_WIDTH), row(SSM_WIDTH), row(SSM_WIDTH),
                 row(CONV_DIM - SSM_WIDTH), row(LANES),
                 last(n_keep, KV_WIDTH), last(n_keep, KV_WIDTH), last(carry_rows, CONV_DIM)]
    out_shape = [jax.ShapeDtypeStruct((nb, seq, ATTN_WIDTH), BF16),
                 jax.ShapeDtypeStruct((nb, seq, 2 * KV_WIDTH), BF16),
                 jax.ShapeDtypeStruct((nb, seq, 2 * KV_WIDTH), BF16),
                 jax.ShapeDtypeStruct((nb, seq, SSM_WIDTH), F32),
                 jax.ShapeDtypeStruct((nb, seq, SSM_WIDTH), F32),
                 jax.ShapeDtypeStruct((nb, seq, CONV_DIM - SSM_WIDTH), BF16),
                 jax.ShapeDtypeStruct((nb, seq, LANES), F32),
                 jax.ShapeDtypeStruct((nb, n_keep, KV_WIDTH), F32),
                 jax.ShapeDtypeStruct((nb, n_keep, KV_WIDTH), F32),
                 jax.ShapeDtypeStruct((nb, carry_rows, CONV_DIM), F32)]
    kern = functools.partial(_inproj_kernel, tm=tm, carry_rows=carry_rows, shift=shift, n_keep=n_keep,
                             has_state=has_state)
    return pl.pallas_call(
        kern, grid=(nb, n_l), in_specs=in_specs, out_specs=out_specs, out_shape=out_shape,
        scratch_shapes=[pltpu.VMEM((carry_rows + tm, CONV_DIM), F32)],
        compiler_params=pltpu.CompilerParams(dimension_semantics=("arbitrary", "arbitrary"),
                                             vmem_limit_bytes=VMEM_LIMIT),
        name="inproj",
    )(*args)


def _attn_kernel(sinks_ref, q_ref, kc_ref, kp_ref, vc_ref, vp_ref, o_ref, *, tq):
    i = pl.program_id(1)
    nblk = tq // WINDOW
    lane = lax.broadcasted_iota(jnp.int32, (WINDOW, LANES), 1)
    low = lane < HEAD_DIM
    qi = lax.broadcasted_iota(jnp.int32, (WINDOW, 2 * WINDOW), 0)
    kj = lax.broadcasted_iota(jnp.int32, (WINDOW, 2 * WINDOW), 1)
    diff = qi + WINDOW - kj
    band = (diff >= 0) & (diff < WINDOW)
    zero_bf = jnp.zeros((WINDOW, LANES), BF16)
    for blk in range(nblk):
        r0 = blk * WINDOW
        if blk == 0:
            k_prev, v_prev = kp_ref[0], vp_ref[0]
            mask = band & ((i > 0) | (kj >= WINDOW))
        else:
            k_prev = kc_ref[0, r0 - WINDOW:r0, :]
            v_prev = vc_ref[0, r0 - WINDOW:r0, :]
            mask = band
        kk = jnp.concatenate([k_prev, kc_ref[0, r0:r0 + WINDOW, :]], axis=0)
        vv = jnp.concatenate([v_prev, vc_ref[0, r0:r0 + WINDOW, :]], axis=0)
        for pair in range(N_Q_HEADS // 2):
            g = (2 * pair) // (N_Q_HEADS // N_KV_HEADS)
            qp = q_ref[0, r0:r0 + WINDOW, pair * LANES:(pair + 1) * LANES]
            outs = []
            for e in range(2):
                h = 2 * pair + e
                var = 0 if g == e else 1
                qh = jnp.where(low if e == 0 else ~low, qp, zero_bf)
                s = _dot_nt(qh, kk[:, var * LANES:(var + 1) * LANES])
                s = jnp.where(mask, s, -jnp.inf)
                sk = sinks_ref[h]
                m = jnp.maximum(jnp.max(s, axis=-1, keepdims=True), sk)
                p = jnp.exp(s - m)
                den = jnp.sum(p, axis=-1, keepdims=True) + jnp.exp(sk - m)
                o = _dot(p.astype(BF16), vv[:, var * LANES:(var + 1) * LANES])
                outs.append(o / den)
            o_ref[0, r0:r0 + WINDOW, pair * LANES:(pair + 1) * LANES] = (
                jnp.where(low, outs[0], outs[1]).astype(BF16))


def _attention(sinks, q, k2, v2, *, tq):
    nb, seq, _ = q.shape
    r = tq // WINDOW
    cur = lambda w: pl.BlockSpec((1, tq, w), lambda b, i: (b, i, 0))
    prev = lambda w: pl.BlockSpec((1, WINDOW, w), lambda b, i: (b, jnp.maximum(i * r - 1, 0), 0))
    return pl.pallas_call(
        functools.partial(_attn_kernel, tq=tq),
        grid=(nb, seq // tq),
        in_specs=[pl.BlockSpec(memory_space=pltpu.SMEM),
                  cur(ATTN_WIDTH), cur(2 * KV_WIDTH), prev(2 * KV_WIDTH), cur(2 * KV_WIDTH), prev(2 * KV_WIDTH)],
        out_specs=cur(ATTN_WIDTH),
        out_shape=jax.ShapeDtypeStruct((nb, seq, ATTN_WIDTH), BF16),
        compiler_params=pltpu.CompilerParams(dimension_semantics=("arbitrary", "arbitrary"),
                                             vmem_limit_bytes=VMEM_LIMIT),
        name="swa_attn",
    )(sinks, q, k2, k2, v2, v2)


def _ssd_kernel(xs_ref, bc_ref, dt_ref, a_ref, dskip_ref, tri_ref,
                y_ref, hT_ref, h_s, *, tc):
    l = pl.program_id(1)
    n_l = pl.num_programs(1)
    gw = SSM_WIDTH // N_SSM_GROUPS
    hpg = N_SSM_HEADS // N_SSM_GROUPS

    @pl.when(l == 0)
    def _():
        h_s[...] = jnp.zeros(h_s.shape, F32)

    lane = lax.broadcasted_iota(jnp.int32, (SSD_CHUNK, LANES), 1)
    low = lane < SSM_HEAD_DIM
    qi = lax.broadcasted_iota(jnp.int32, (SSD_CHUNK, SSD_CHUNK), 0)
    sj = lax.broadcasted_iota(jnp.int32, (SSD_CHUNK, SSD_CHUNK), 1)
    causal = sj <= qi
    a_neg = -jnp.exp(a_ref[...])
    tri = tri_ref[...]
    zero_bf = jnp.zeros((SSD_CHUNK, LANES), BF16)

    for c in range(tc // SSD_CHUNK):
        rows = slice(c * SSD_CHUNK, (c + 1) * SSD_CHUNK)
        dt = dt_ref[0, rows, :]
        hi, mid, lo = _split3(dt * a_neg)
        acum = (_dot(tri, hi) + _dot(tri, mid) + _dot(tri, lo)) * LOG2E
        acum_t = acum.T
        dt_t = dt.T
        a_end_t = acum_t[:, SSD_CHUNK - 1:SSD_CHUNK]
        w_end_t = dt_t * jnp.exp2(a_end_t - acum_t)
        dec_t = jnp.exp2(a_end_t)
        dec_t = jnp.broadcast_to(dec_t, (LANES, LANES))
        row_t = acum_t - jnp.log2(dt_t)
        for g in range(N_SSM_GROUPS):
            b_g = bc_ref[0, rows, g * D_STATE:(g + 1) * D_STATE]
            c_g = bc_ref[0, rows, (N_SSM_GROUPS + g) * D_STATE:(N_SSM_GROUPS + g + 1) * D_STATE]
            cb = _dot_nt(c_g, b_g)
            b_gt = b_g.astype(F32).T
            y_inter = _dot(c_g, h_s[g].astype(BF16))
            for pr in range(hpg // 2):
                lhs_y, lhs_s, escale, dsc = [], [], [], []
                for e in range(2):
                    h = g * hpg + 2 * pr + e
                    col = jnp.broadcast_to(acum[:, h:h + 1], (SSD_CHUNK, SSD_CHUNK))
                    decay_dt = jnp.where(causal, jnp.exp2(col - row_t[h:h + 1, :]), 0.0)
                    lhs_y.append((cb * decay_dt).astype(BF16))
                    lhs_s.append((b_gt * w_end_t[h:h + 1, :]).astype(BF16))
                    escale.append(jnp.exp2(col))
                    dsc.append(dec_t[h:h + 1, :])
                c0 = g * gw + pr * LANES
                x_pair = xs_ref[0, rows, c0:c0 + LANES]
                x_bf = x_pair.astype(BF16)
                rhs = jnp.concatenate([jnp.where(low, x_bf, zero_bf), jnp.where(low, zero_bf, x_bf)], axis=0)
                y_in = _dot(jnp.concatenate(lhs_y, axis=1), rhs)
                st = _dot(jnp.concatenate(lhs_s, axis=1), rhs)
                lc = pr * LANES
                y = y_in + y_inter[:, lc:lc + LANES] * jnp.where(low, escale[0], escale[1])
                y_ref[0, rows, c0:c0 + LANES] = y + dskip_ref[:, c0:c0 + LANES] * x_pair
                h_s[g, :, lc:lc + LANES] = h_s[g, :, lc:lc + LANES] * jnp.where(low, dsc[0], dsc[1]) + st

    @pl.when(l == n_l - 1)
    def _():
        hT_ref[0] = h_s[...]


def _ssd(xs, bc, dt, a_pad, dskip, tri, *, tc):
    nb, seq, _ = xs.shape
    row = lambda w: pl.BlockSpec((1, tc, w), lambda b, l: (b, l, 0))
    const = lambda shape: pl.BlockSpec(shape, lambda b, l: (0,) * len(shape))
    gw = SSM_WIDTH // N_SSM_GROUPS
    return pl.pallas_call(
        functools.partial(_ssd_kernel, tc=tc),
        grid=(nb, seq // tc),
        in_specs=[row(SSM_WIDTH), row(CONV_DIM - SSM_WIDTH), row(LANES),
                  const((1, LANES)), const((1, SSM_WIDTH)), const((SSD_CHUNK, SSD_CHUNK))],
        out_specs=[row(SSM_WIDTH),
                   pl.BlockSpec((1, N_SSM_GROUPS, D_STATE, gw), lambda b, l: (b, 0, 0, 0))],
        out_shape=[jax.ShapeDtypeStruct((nb, seq, SSM_WIDTH), F32),
                   jax.ShapeDtypeStruct((nb, N_SSM_GROUPS, D_STATE, gw), F32)],
        scratch_shapes=[pltpu.VMEM((N_SSM_GROUPS, D_STATE, gw), F32)],
        compiler_params=pltpu.CompilerParams(dimension_semantics=("arbitrary", "arbitrary"),
                                             vmem_limit_bytes=VMEM_LIMIT),
        name="ssd_scan",
    )(xs, bc, dt, a_pad, dskip, tri)


def _outffn_kernel(*refs, ffc, gated):
    if gated:
        x_ref, oa_ref, ys_ref, z_ref, gnorm_ref, wo_ref, gffn_ref, wgu_ref, wd_ref, o_ref = refs
        gw = SSM_WIDTH // N_SSM_GROUPS
        parts = []
        for g in range(N_SSM_GROUPS):
            ls = slice(g * gw, (g + 1) * gw)
            yg = ys_ref[:, ls] * _silu(z_ref[:, ls])
            yg = yg * lax.rsqrt(jnp.mean(yg * yg, axis=-1, keepdims=True) + EPS)
            parts.append((yg * gnorm_ref[:, ls]).astype(BF16))
        ys = jnp.concatenate(parts, axis=1)
    else:
        x_ref, oa_ref, ys_ref, wo_ref, gffn_ref, wgu_ref, wd_ref, o_ref = refs
        ys = ys_ref[...]
    x = x_ref[...]
    xm = x + _dot(oa_ref[...], wo_ref[0:ATTN_WIDTH, :]) + _dot(ys, wo_ref[ATTN_WIDTH:, :])
    hn = _rms_rows(xm, gffn_ref[...]).astype(BF16)
    acc = jnp.zeros_like(xm)
    c0 = 0
    for width in ffc:
        g = _dot(hn, wgu_ref[:, c0:c0 + width])
        up = _dot(hn, wgu_ref[:, D_FF + c0:D_FF + c0 + width])
        acc = acc + _dot((_silu(g) * up).astype(BF16), wd_ref[c0:c0 + width, :])
        c0 += width
    o_ref[...] = xm + acc


def _outffn(x, oa, ys, gate, wo, gffn, wgu, wd, *, tm, ffc):
    rows = x.shape[0]
    row = lambda w: pl.BlockSpec((tm, w), lambda i: (i, 0))
    const = lambda shape: pl.BlockSpec(shape, lambda i: (0,) * len(shape), pipeline_mode=pl.Buffered(1))
    gated = gate is not None
    in_specs = [row(D_MODEL), row(ATTN_WIDTH), row(SSM_WIDTH)]
    args = [x, oa, ys]
    if gated:
        in_specs += [row(SSM_WIDTH), const((1, SSM_WIDTH))]
        args += list(gate)
    in_specs += [const((ATTN_WIDTH + SSM_WIDTH, D_MODEL)), const((1, D_MODEL)),
                 const((D_MODEL, 2 * D_FF)), const((D_FF, D_MODEL))]
    args += [wo, gffn, wgu, wd]
    return pl.pallas_call(
        functools.partial(_outffn_kernel, ffc=ffc, gated=gated),
        grid=(rows // tm,),
        in_specs=in_specs,
        out_specs=row(D_MODEL),
        out_shape=jax.ShapeDtypeStruct((rows, D_MODEL), F32),
        compiler_params=pltpu.CompilerParams(dimension_semantics=("arbitrary",),
                                             vmem_limit_bytes=VMEM_LIMIT),
        name="outproj_ffn",
    )(*args)


T_PAD = SUBLANES


def _attn_dec_kernel(q_ref, kc_ref, kn_ref, vc_ref, vn_ref, sink_ref, o_ref, *, t_new):
    q = q_ref[...]
    nbat, nq, _ = q.shape
    qpk = N_Q_HEADS // N_KV_HEADS
    s_c = jnp.einsum('bqd,bkd->bqk', q, kc_ref[...], preferred_element_type=F32)
    s_n = jnp.einsum('bqd,bkd->bqk', q, kn_ref[...], preferred_element_type=F32)
    n_c = s_c.shape[-1]
    t_c = lax.broadcasted_iota(jnp.int32, (nbat, nq, n_c), 1) // qpk
    j_c = lax.broadcasted_iota(jnp.int32, (nbat, nq, n_c), 2)
    s_c = jnp.where(j_c > t_c + (n_c - WINDOW), s_c, -jnp.inf)
    t_n = lax.broadcasted_iota(jnp.int32, (nbat, nq, T_PAD), 1) // qpk
    j_n = lax.broadcasted_iota(jnp.int32, (nbat, nq, T_PAD), 2)
    s_n = jnp.where((j_n <= t_n) & (j_n < t_new), s_n, -jnp.inf)
    sk = sink_ref[...]
    m = jnp.maximum(jnp.maximum(jnp.max(s_c, axis=-1, keepdims=True), jnp.max(s_n, axis=-1, keepdims=True)), sk)
    p_c = jnp.exp(s_c - m)
    p_n = jnp.exp(s_n - m)
    den = jnp.sum(p_c, axis=-1, keepdims=True) + jnp.sum(p_n, axis=-1, keepdims=True) + jnp.exp(sk - m)
    o = (jnp.einsum('bqk,bkd->bqd', p_c.astype(BF16), vc_ref[...], preferred_element_type=F32)
         + jnp.einsum('bqk,bkd->bqd', p_n.astype(BF16), vn_ref[...], preferred_element_type=F32))
    o_ref[...] = (o / den).astype(BF16)


def _attention_dec(q, kc, kn, vc, vn, sink_rows, *, t_new, bb):
    nbat, nq, hd = q.shape
    n_c = kc.shape[1]
    blk = lambda r, w: pl.BlockSpec((bb, r, w), lambda i: (i, 0, 0))
    return pl.pallas_call(
        functools.partial(_attn_dec_kernel, t_new=t_new),
        grid=(nbat // bb,),
        in_specs=[blk(nq, hd), blk(n_c, hd), blk(T_PAD, hd), blk(n_c, hd), blk(T_PAD, hd), blk(nq, 1)],
        out_specs=blk(nq, hd),
        out_shape=jax.ShapeDtypeStruct((nbat, nq, hd), BF16),
        compiler_params=pltpu.CompilerParams(dimension_semantics=("arbitrary",), vmem_limit_bytes=VMEM_LIMIT),
        name="swa_attn_decode",
    )(q, kc, kn, vc, vn, sink_rows)


def _ssd_dec_kernel(*refs, bb, t_new, n_prev):
    (xs_ref, bc_ref, dt_ref, z_ref, h0_ref) = refs[:5]
    prev_refs = refs[5:5 + n_prev]
    (a_ref, dskip_ref, gnorm_ref, tri_ref, ones_ref, exp_ref,
     y_ref, h_ref, aw_s, xdt_s, xw_s, ea_s, dec_s) = refs[5 + n_prev:]
    d = pl.program_id(0)
    gw = SSM_WIDTH // N_SSM_GROUPS
    hpg = N_SSM_HEADS // N_SSM_GROUPS
    rows = bb * T_PAD

    for j in range(n_prev):
        @pl.when(d == j)
        def _(j=j):
            h_ref[0] = prev_refs[j][0]

    @pl.when(d == n_prev)
    def _():
        expand = exp_ref[...]

        def widen(v):
            hi, mid, lo = _split3(v)
            return _dot(hi, expand) + _dot(mid, expand) + _dot(lo, expand)

        def seq_sum(m, v):
            hi, mid, lo = _split3(v)
            return _dot(m, hi) + _dot(m, mid) + _dot(m, lo)

        dt = dt_ref[...].reshape(rows, LANES)
        dta = dt * (-jnp.exp(a_ref[...]))
        a_w = widen(seq_sum(tri_ref[...], dta))
        a_end = seq_sum(ones_ref[...], dta)
        xdt = xs_ref[...].reshape(rows, SSM_WIDTH) * widen(dt)
        aw_s[...] = a_w
        xdt_s[...] = xdt
        xw_s[...] = xdt * jnp.exp(widen(a_end) - a_w)
        ea_s[...] = jnp.exp(a_w)
        dec_s[...] = jnp.exp(a_end)

        t_row = lax.broadcasted_iota(jnp.int32, (T_PAD, gw), 0)

        def seq_body(i, carry):
            r0 = pl.multiple_of(i * T_PAD, T_PAD)
            rs = pl.ds(r0, T_PAD)
            dec = jnp.broadcast_to(dec_s[pl.ds(r0, 1), :], (LANES, LANES)).T
            bc = bc_ref[i]
            for g in range(N_SSM_GROUPS):
                b_g = bc[:, g * D_STATE:(g + 1) * D_STATE]
                c_g = bc[:, (N_SSM_GROUPS + g) * D_STATE:(N_SSM_GROUPS + g + 1) * D_STATE]
                cb = _dot_nt(c_g, b_g)
                ls = slice(g * gw, (g + 1) * gw)
                a_g = aw_s[rs, ls]
                xdt_g = xdt_s[rs, ls]
                y = jnp.zeros((T_PAD, gw), F32)
                for s in range(t_new):
                    w = jnp.where(t_row >= s, jnp.exp(a_g - a_g[s:s + 1, :]), 0.0)
                    y = y + (w * cb[:, s:s + 1]) * xdt_g[s:s + 1, :]
                h0g = h0_ref[0, i, g * hpg:(g + 1) * hpg].reshape(gw, D_STATE)
                y = y + _dot_nt(c_g, h0g.astype(BF16)) * ea_s[rs, ls]
                y = y + dskip_ref[:, ls] * xs_ref[i, :, ls]
                y = y * _silu(z_ref[i, :, ls])
                y = y * lax.rsqrt(jnp.mean(y * y, axis=-1, keepdims=True) + EPS)
                y_ref[i, :, ls] = (y * gnorm_ref[:, ls]).astype(BF16)
                delta = lax.dot_general(xw_s[rs, ls].astype(BF16), b_g, (((0,), (0,)), ((), ())),
                                        preferred_element_type=F32)
                for hl in range(hpg):
                    h = g * hpg + hl
                    h_ref[0, i, h] = (h0_ref[0, i, h] * dec[h:h + 1, :]
                                      + delta[hl * SSM_HEAD_DIM:(hl + 1) * SSM_HEAD_DIM, :])
            return carry

        lax.fori_loop(0, bb, seq_body, 0)


def _ssd_dec(xs3, bc3, dt3, z3, state_all, layer, prev_states, a_pad, dskip, gnorm, tri_b, ones_b, expand,
             *, bb, t_new):
    nb = xs3.shape[0]
    nblk = nb // bb
    n_prev = len(prev_states)
    rows = bb * T_PAD

    def phase_block(j):
        return lambda d, i: jnp.clip(i + (d - j) * nblk, 0, nblk - 1)

    cur = phase_block(n_prev)
    blk = lambda w: pl.BlockSpec((bb, T_PAD, w), lambda d, i: (cur(d, i), 0, 0))
    const = lambda shape: pl.BlockSpec(shape, lambda d, i: (0,) * len(shape))
    st_shape = (1, bb, N_SSM_HEADS, SSM_HEAD_DIM, D_STATE)
    in_specs = [blk(SSM_WIDTH), blk(CONV_DIM - SSM_WIDTH), blk(LANES), blk(SSM_WIDTH),
                pl.BlockSpec(st_shape, lambda d, i: (layer, cur(d, i), 0, 0, 0))]
    for j in range(n_prev):
        in_specs.append(pl.BlockSpec(st_shape, lambda d, i, j=j: (0, phase_block(j)(d, i), 0, 0, 0)))
    in_specs += [const((1, LANES)), const((1, SSM_WIDTH)), const((1, SSM_WIDTH)),
                 const((rows, rows)), const((rows, rows)), const((LANES, SSM_WIDTH))]
    return pl.pallas_call(
        functools.partial(_ssd_dec_kernel, bb=bb, t_new=t_new, n_prev=n_prev),
        grid=(n_prev + 1, nblk),
        in_specs=in_specs,
        out_specs=[blk(SSM_WIDTH), pl.BlockSpec(st_shape, lambda d, i: (d, i, 0, 0, 0))],
        out_shape=[jax.ShapeDtypeStruct((nb, T_PAD, SSM_WIDTH), BF16),
                   jax.ShapeDtypeStruct((n_prev + 1,) + state_all.shape[1:], F32)],
        scratch_shapes=[pltpu.VMEM((rows, SSM_WIDTH), F32), pltpu.VMEM((rows, SSM_WIDTH), F32),
                        pltpu.VMEM((rows, SSM_WIDTH), F32), pltpu.VMEM((rows, SSM_WIDTH), F32),
                        pltpu.VMEM((rows, LANES), F32)],
        compiler_params=pltpu.CompilerParams(dimension_semantics=("arbitrary", "arbitrary"),
                                             vmem_limit_bytes=VMEM_LIMIT),
        name="ssd_decode",
    )(xs3, bc3, dt3, z3, state_all, *prev_states, a_pad, dskip, gnorm, tri_b, ones_b, expand)


def _rope_tables(pos):
    half = HEAD_DIM // 2
    inv = ROPE_THETA ** (-jnp.arange(half, dtype=F32) / half)
    ang = pos.astype(F32)[:, None] * inv[None, :]
    cos = jnp.tile(jnp.cos(ang), (1, LANES // half))
    sin = jnp.sin(ang)
    sin_signed = jnp.tile(jnp.concatenate([-sin, sin], axis=1), (1, LANES // HEAD_DIM))
    return cos, sin_signed


def _layer_consts(l, p):
    w = p['w_in'][l]
    o_z = ATTN_WIDTH + 2 * KV_WIDTH
    o_xbc = o_z + SSM_WIDTH
    o_dt = o_xbc + CONV_DIM
    w_in = jnp.concatenate(
        [w[:, :o_z], jnp.pad(w[:, o_dt:], ((0, 0), (0, LANES - N_SSM_HEADS))), w[:, o_z:o_dt]], axis=1).astype(BF16)
    return dict(
        gmix=p['norm_mix'][l][None, :],
        win=w_in,
        qn=jnp.tile(p['q_norm'][l], LANES // HEAD_DIM)[None, :],
        kn=jnp.tile(p['k_norm'][l], LANES // HEAD_DIM)[None, :],
        convw=p['conv_w'][l],
        convb=p['conv_b'][l][None, :],
        dtb=jnp.pad(p['dt_bias'][l], (0, LANES - N_SSM_HEADS))[None, :],
        a_pad=jnp.pad(p['a_log'][l], (0, LANES - N_SSM_HEADS))[None, :],
        dskip=jnp.repeat(p['d_skip'][l], SSM_HEAD_DIM)[None, :],
        gnorm=p['ssm_norm'][l][None, :],
        sinks=p['sinks'][l],
        wo=p['w_out'][l].astype(BF16),
        gffn=p['norm_ffn'][l][None, :],
        wgu=p['w_gate_up'][l].astype(BF16),
        wd=p['w_down'][l].astype(BF16),
    )


def _state_from_t(h_t):
    nb = h_t.shape[0]
    hpg = N_SSM_HEADS // N_SSM_GROUPS
    h = h_t.reshape(nb, N_SSM_GROUPS, D_STATE, hpg, SSM_HEAD_DIM)
    return jnp.transpose(h, (0, 1, 3, 4, 2)).reshape(nb, N_SSM_HEADS, SSM_HEAD_DIM, D_STATE)


def _prompt_layer(x, c, shared, *, tm):
    nb, seq, _ = x.shape
    (q, k2, v2, z, xs, bc, dt, knew, vnew, convst) = _inproj(
        x, c['gmix'], c['win'], shared['m128'], c['qn'], c['kn'], shared['cos_p'], shared['sin_p'],
        c['convw'], c['convb'], c['dtb'], None, tm=tm, carry_rows=SUBLANES, shift=1, n_keep=min(WINDOW, seq))
    oa = _attention(c['sinks'], q, k2, v2, tq=tm)
    ys, h_t = _ssd(xs, bc, dt, c['a_pad'], c['dskip'], shared['tri'], tc=tm)
    rows = nb * seq
    xo = _outffn(x.reshape(rows, D_MODEL), oa.reshape(rows, ATTN_WIDTH), ys.reshape(rows, SSM_WIDTH),
                 (z.reshape(rows, SSM_WIDTH), c['gnorm']), c['wo'], c['gffn'], c['wgu'], c['wd'], tm=tm,
                 ffc=FF_CHUNKS)
    n_keep = knew.shape[1]
    return (xo.reshape(nb, seq, D_MODEL),
            knew.reshape(nb, n_keep, N_KV_HEADS, HEAD_DIM), vnew.reshape(nb, n_keep, N_KV_HEADS, HEAD_DIM),
            convst[:, SUBLANES - (CONV_W - 1):, :], _state_from_t(h_t))


def _sample_layer(x_t, cache_k, cache_v, conv0, state_all, layer, prev_states, c, shared, *, n_seq, t_new):
    rows = t_new * n_seq
    qpk = N_Q_HEADS // N_KV_HEADS
    n_c = cache_k.shape[1]
    conv0_t = jnp.transpose(conv0, (1, 0, 2)).reshape((CONV_W - 1) * n_seq, CONV_DIM)
    (q, _, _, z, xs, bc, dt, knew, vnew, convst) = _inproj(
        x_t, c['gmix'], c['win'], shared['m128'], c['qn'], c['kn'], shared['cos_s'], shared['sin_s'],
        c['convw'], c['convb'], c['dtb'], conv0_t, tm=rows, carry_rows=(CONV_W - 1) * n_seq, shift=n_seq,
        n_keep=rows)

    def to_bg(a, t, inner):
        a = a.reshape(t, n_seq, N_KV_HEADS, inner, HEAD_DIM)
        return jnp.transpose(a, (1, 2, 0, 3, 4)).reshape(n_seq * N_KV_HEADS, t * inner, HEAD_DIM)

    def pad_t(a):
        return jnp.pad(a, ((0, 0), (0, T_PAD - a.shape[1]), (0, 0)))

    def cache_bg(a):
        return jnp.transpose(a, (0, 2, 1, 3)).reshape(n_seq * N_KV_HEADS, n_c, HEAD_DIM).astype(BF16)

    q_bg = to_bg(q[0], t_new, qpk)
    kn_bg = pad_t(to_bg(knew[0], t_new, 1)).astype(BF16)
    vn_bg = pad_t(to_bg(vnew[0], t_new, 1)).astype(BF16)
    sink_rows = jnp.tile(c['sinks'].reshape(N_KV_HEADS, 1, qpk), (n_seq, t_new, 1)).reshape(
        n_seq * N_KV_HEADS, t_new * qpk, 1)
    bb_a = min(64, n_seq * N_KV_HEADS)
    o_bg = _attention_dec(q_bg, cache_bg(cache_k), kn_bg, cache_bg(cache_v), vn_bg, sink_rows,
                          t_new=t_new, bb=bb_a)
    oa = jnp.transpose(o_bg.reshape(n_seq, N_KV_HEADS, t_new, qpk, HEAD_DIM), (2, 0, 1, 3, 4)).reshape(
        rows, ATTN_WIDTH)

    def to_seq(a):
        a = jnp.transpose(a[0].reshape(t_new, n_seq, a.shape[-1]), (1, 0, 2))
        return pad_t(a)

    y3, h_new = _ssd_dec(to_seq(xs), to_seq(bc), to_seq(dt), to_seq(z), state_all, layer, prev_states,
                         c['a_pad'], c['dskip'], c['gnorm'], shared['tri_b'], shared['ones_b'], shared['expand'],
                         bb=shared['bb_s'], t_new=t_new)
    ys = jnp.transpose(y3[:, :t_new], (1, 0, 2)).reshape(rows, SSM_WIDTH)

    xo = _outffn(x_t[0], oa, ys, None, c['wo'], c['gffn'], c['wgu'], c['wd'], tm=min(512, rows), ffc=FF_CHUNKS)

    def new_rows(a):
        return jnp.transpose(a.reshape(t_new, n_seq, N_KV_HEADS, HEAD_DIM), (1, 0, 2, 3))

    win_k = jnp.concatenate([cache_k, new_rows(knew[0])], axis=1)[:, t_new:]
    win_v = jnp.concatenate([cache_v, new_rows(vnew[0])], axis=1)[:, t_new:]
    conv_new = jnp.transpose(convst[0].reshape(CONV_W - 1, n_seq, CONV_DIM), (1, 0, 2))
    return xo[None], win_k, win_v, conv_new, h_new


def kernel(x_prompt, x_sample, cache_win_k, cache_win_v, state_conv, state_ssm,
           norm_mix, w_in, q_norm, k_norm, sinks, conv_w, conv_b, dt_bias, a_log,
           d_skip, ssm_norm, w_out, norm_ffn, w_gate_up, w_down):
    p = dict(norm_mix=norm_mix, w_in=w_in, q_norm=q_norm, k_norm=k_norm, sinks=sinks, conv_w=conv_w,
             conv_b=conv_b, dt_bias=dt_bias, a_log=a_log, d_skip=d_skip, ssm_norm=ssm_norm, w_out=w_out,
             norm_ffn=norm_ffn, w_gate_up=w_gate_up, w_down=w_down)
    depth = w_in.shape[0]
    seq = x_prompt.shape[1]
    cos_p, sin_p = _rope_tables(jnp.arange(seq, dtype=jnp.int32))
    half_blk = jnp.arange(LANES) // HEAD_DIM
    m128 = (jnp.where(half_blk[:, None] == half_blk[None, :], 1.0 / HEAD_DIM, 0.0)).astype(BF16)
    m128 = jnp.concatenate([m128, m128], axis=0)
    tri = (jnp.arange(SSD_CHUNK)[:, None] >= jnp.arange(SSD_CHUNK)[None, :]).astype(BF16)
    n_seq, t_new, _ = x_sample.shape
    pos_s = PAST_LEN + jnp.repeat(jnp.arange(t_new, dtype=jnp.int32), n_seq)
    cos_s, sin_s = _rope_tables(pos_s)
    bb_s = min(8, n_seq)
    r_idx = jnp.arange(bb_s * T_PAD)
    same_seq = (r_idx[:, None] // T_PAD) == (r_idx[None, :] // T_PAD)
    tri_b = (same_seq & (r_idx[:, None] >= r_idx[None, :])).astype(BF16)
    ones_b = same_seq.astype(BF16)
    expand = (jnp.arange(LANES)[:, None] == (jnp.arange(SSM_WIDTH) // SSM_HEAD_DIM)[None, :]).astype(BF16)
    shared = dict(cos_p=cos_p, sin_p=sin_p, cos_s=cos_s, sin_s=sin_s, m128=m128, tri=tri, tri_b=tri_b,
                  ones_b=ones_b, bb_s=bb_s, expand=expand)
    tm = min(512, seq)

    xp = x_prompt
    xs = jnp.transpose(x_sample, (1, 0, 2)).reshape(1, t_new * n_seq, D_MODEL)
    pk, pv, pc, ph = [], [], [], []
    sk, sv, sc, sh = [], [], [], []
    for l in range(depth):
        c = _layer_consts(l, p)
        xp, k1, v1, c1, h1 = _prompt_layer(xp, c, shared, tm=tm)
        prev = sh if l == depth - 1 else []
        xs, k2, v2, c2, h2 = _sample_layer(xs, cache_win_k[l], cache_win_v[l], state_conv[l], state_ssm, l, prev,
                                           c, shared, n_seq=n_seq, t_new=t_new)
        pk.append(k1); pv.append(v1); pc.append(c1); ph.append(h1)
        sk.append(k2); sv.append(v2); sc.append(c2); sh.append(h2)
    ys = jnp.transpose(xs.reshape(t_new, n_seq, D_MODEL), (1, 0, 2))
    return (xp, ys, jnp.stack(pk), jnp.stack(pv), jnp.stack(pc), jnp.stack(ph),
            jnp.stack(sk), jnp.stack(sv), jnp.stack(sc), sh[-1])
```

```python
import functools
import math

import numpy as np
import jax
import jax.numpy as jnp
from jax import lax
from jax.experimental import pallas as pl
from jax.experimental.pallas import tpu as pltpu

F32 = jnp.float32
BF16 = jnp.bfloat16

D_MODEL = 1024
HEAD_DIM = 64
N_Q_HEADS = 8
N_KV_HEADS = 2
ATTN_WIDTH = N_Q_HEADS * HEAD_DIM
KV_WIDTH = N_KV_HEADS * HEAD_DIM
WINDOW = 128
ROPE_THETA = 10000.0
ATTN_SCALE = HEAD_DIM ** -0.5
SSM_HEAD_DIM = 64
N_SSM_HEADS = 16
SSM_WIDTH = N_SSM_HEADS * SSM_HEAD_DIM
N_SSM_GROUPS = 2
D_STATE = 128
CONV_W = 4
CONV_DIM = SSM_WIDTH + 2 * N_SSM_GROUPS * D_STATE
SSD_CHUNK = 128
D_FF = 2816
EPS = 1e-6
LOG2E = math.log2(math.e)
PAST_LEN = 16384

LANES = 128
SUBLANES = 8

COL_Q = 0
COL_K = COL_Q + ATTN_WIDTH
COL_V = COL_K + KV_WIDTH
COL_DT = COL_V + KV_WIDTH
COL_Z = COL_DT + LANES
COL_XBC = COL_Z + SSM_WIDTH
PROJ_PAD = COL_XBC + CONV_DIM
HEAD_COLS = COL_Z

VMEM_LIMIT = 56 * 1024 * 1024
FF_CHUNKS = (6 * 256, 5 * 256)


def _dot(a, b):
    return jnp.dot(a, b, preferred_element_type=F32)


def _dot_nt(a, b):
    return lax.dot_general(a, b, (((1,), (1,)), ((), ())), preferred_element_type=F32)


def _silu(x):
    return x / (1.0 + jnp.exp(-x))


def _softplus(x):
    return jnp.maximum(x, 0.0) + jnp.log1p(jnp.exp(-jnp.abs(x)))


def _rms_rows(x, g):
    return x * lax.rsqrt(jnp.mean(x * x, axis=-1, keepdims=True) + EPS) * g


def _split2(x):
    hi = x.astype(BF16)
    lo = (x - hi.astype(F32)).astype(BF16)
    return hi, lo


def _split3(x):
    hi = x.astype(BF16)
    r1 = x - hi.astype(F32)
    mid = r1.astype(BF16)
    lo = (r1 - mid.astype(F32)).astype(BF16)
    return hi, mid, lo


def _head_norm_rope(x, m128, g, cos, sin_signed, low_half):
    hi, lo = _split2(x * x)
    ms = _dot(jnp.concatenate([hi, lo], axis=1), m128)
    xn = x * lax.rsqrt(ms + EPS) * g
    rot = jnp.where(low_half, pltpu.roll(xn, LANES - HEAD_DIM // 2, 1),
                    pltpu.roll(xn, HEAD_DIM // 2, 1))
    return xn * cos + rot * sin_signed


def _inproj_kernel(*refs, tm, carry_rows, shift, n_keep, has_state, n_parts):
    if has_state:
        (x_ref, gmix_ref, win_ref, m128_ref, qn_ref, kn_ref, cos_ref, sin_ref,
         convw_ref, convb_ref, dtb_ref, conv0_ref,
         q_ref, k2_ref, v2_ref, z_ref, xs_ref, bc_ref, dt_ref, knew_ref, vnew_ref, convst_ref,
         xbc_s) = refs
    else:
        (x_ref, gmix_ref, win_ref, m128_ref, qn_ref, kn_ref, cos_ref, sin_ref,
         convw_ref, convb_ref, dtb_ref,
         q_ref, k2_ref, v2_ref, z_ref, xs_ref, bc_ref, dt_ref, knew_ref, vnew_ref, convst_ref,
         xbc_s) = refs
        conv0_ref = None
    l = pl.program_id(1)

    @pl.when(l == 0)
    def _():
        if has_state:
            xbc_s[0:carry_rows, :] = conv0_ref[...]
        else:
            xbc_s[0:carry_rows, :] = jnp.zeros((carry_rows, CONV_DIM), F32)

    m128 = m128_ref[...]
    cw = 512
    nr = tm // n_parts
    lane = lax.broadcasted_iota(jnp.int32, (nr, LANES), 1)
    low_half = (lane % HEAD_DIM) < (HEAD_DIM // 2)
    for part in range(n_parts):
        r0 = part * nr
        rs = slice(r0, r0 + nr)
        base = carry_rows + r0
        u = _rms_rows(x_ref[0, rs, :], gmix_ref[...]).astype(BF16)
        for j in range(CONV_DIM // cw):
            c0 = COL_XBC + j * cw
            xbc_s[base:base + nr, j * cw:(j + 1) * cw] = _dot(u, win_ref[:, c0:c0 + cw])
        z_ref[0, rs, :] = _dot(u, win_ref[:, COL_Z:COL_Z + SSM_WIDTH])
        heads = _dot(u, win_ref[:, 0:HEAD_COLS])
        cos = cos_ref[rs, :]
        sin = sin_ref[rs, :]

        for j in range(ATTN_WIDTH // LANES):
            qj = heads[:, COL_Q + j * LANES:COL_Q + (j + 1) * LANES]
            qj = _head_norm_rope(qj, m128, qn_ref[...], cos, sin, low_half)
            q_ref[0, rs, j * LANES:(j + 1) * LANES] = (qj * ATTN_SCALE).astype(BF16)

        k = _head_norm_rope(heads[:, COL_K:COL_K + KV_WIDTH], m128, kn_ref[...], cos, sin, low_half)
        v = heads[:, COL_V:COL_V + KV_WIDTH]
        k2_ref[0, rs, 0:LANES] = k.astype(BF16)
        k2_ref[0, rs, LANES:2 * LANES] = pltpu.roll(k, HEAD_DIM, 1).astype(BF16)
        v2_ref[0, rs, 0:LANES] = v.astype(BF16)
        v2_ref[0, rs, LANES:2 * LANES] = pltpu.roll(v, HEAD_DIM, 1).astype(BF16)
        if part == n_parts - 1:
            knew_ref[0] = k[nr - n_keep:, :]
            vnew_ref[0] = v[nr - n_keep:, :]

        dt_ref[0, rs, :] = _softplus(heads[:, COL_DT:COL_DT + LANES] + dtb_ref[...])

        for j in range(CONV_DIM // cw):
            cs = slice(j * cw, (j + 1) * cw)
            w = [convw_ref[i:i + 1, cs] for i in range(CONV_W)]
            bias = convb_ref[:, cs]
            if shift % SUBLANES == 0:
                acc = bias
                for i in range(CONV_W):
                    off = base - (CONV_W - 1 - i) * shift
                    acc = acc + xbc_s[off:off + nr, cs] * w[i]
                act = _silu(acc)
            else:
                cur = xbc_s[base:base + nr, cs]
                acc = cur * w[0]
                for i in range(1, CONV_W):
                    acc = pltpu.roll(acc, shift, 0) + cur * w[i]
                head = bias
                for i in range(CONV_W):
                    off = base - (CONV_W - 1 - i) * shift
                    head = head + xbc_s[off:off + SUBLANES, cs] * w[i]
                act = _silu(jnp.concatenate([head, acc[SUBLANES:, :] + bias], axis=0))
            if j * cw < SSM_WIDTH:
                xs_ref[0, rs, cs] = act
            else:
                bc_ref[0, rs, :] = act.astype(BF16)

    convst_ref[0] = xbc_s[tm:tm + carry_rows, :]
    xbc_s[0:carry_rows, :] = xbc_s[tm:tm + carry_rows, :]


def _inproj(x, gmix, win, m128, qn, kn, cos, sin, convw, convb, dtb, conv0, *, tm, carry_rows, shift, n_keep):
    nb, seq, _ = x.shape
    n_l = seq // tm
    n_parts = 1
    has_state = conv0 is not None
    const = lambda shape: pl.BlockSpec(shape, lambda b, l: (0,) * len(shape))
    row = lambda w: pl.BlockSpec((1, tm, w), lambda b, l: (b, l, 0))
    in_specs = [row(D_MODEL), const((1, D_MODEL)), const((D_MODEL, PROJ_PAD)), const((2 * LANES, LANES)),
                const((1, LANES)), const((1, LANES)),
                pl.BlockSpec((tm, LANES), lambda b, l: (l, 0)), pl.BlockSpec((tm, LANES), lambda b, l: (l, 0)),
                const((CONV_W, CONV_DIM)), const((1, CONV_DIM)), const((1, LANES))]
    args = [x, gmix, win, m128, qn, kn, cos, sin, convw, convb, dtb]
    if has_state:
        in_specs.append(const((carry_rows, CONV_DIM)))
        args.append(conv0)
    last = lambda rows, w: pl.BlockSpec((1, rows, w), lambda b, l: (b, 0, 0))
    out_specs = [row(ATTN_WIDTH), row(2 * KV_WIDTH), row(2 * KV_WIDTH), row(SSM_WIDTH), row(SSM_WIDTH),
                 row(CONV_DIM - SSM_WIDTH), row(LANES),
                 last(n_keep, KV_WIDTH), last(n_keep, KV_WIDTH), last(carry_rows, CONV_DIM)]
    out_shape = [jax.ShapeDtypeStruct((nb, seq, ATTN_WIDTH), BF16),
                 jax.ShapeDtypeStruct((nb, seq, 2 * KV_WIDTH), BF16),
                 jax.ShapeDtypeStruct((nb, seq, 2 * KV_WIDTH), BF16),
                 jax.ShapeDtypeStruct((nb, seq, SSM_WIDTH), F32),
                 jax.ShapeDtypeStruct((nb, seq, SSM_WIDTH), F32),
                 jax.ShapeDtypeStruct((nb, seq, CONV_DIM - SSM_WIDTH), BF16),
                 jax.ShapeDtypeStruct((nb, seq, LANES), F32),
                 jax.ShapeDtypeStruct((nb, n_keep, KV_WIDTH), F32),
                 jax.ShapeDtypeStruct((nb, n_keep, KV_WIDTH), F32),
                 jax.ShapeDtypeStruct((nb, carry_rows, CONV_DIM), F32)]
    kern = functools.partial(_inproj_kernel, tm=tm, carry_rows=carry_rows, shift=shift, n_keep=n_keep,
                             has_state=has_state, n_parts=n_parts)
    return pl.pallas_call(
        kern, grid=(nb, n_l), in_specs=in_specs, out_specs=out_specs, out_shape=out_shape,
        scratch_shapes=[pltpu.VMEM((carry_rows + tm, CONV_DIM), F32)],
        compiler_params=pltpu.CompilerParams(dimension_semantics=("arbitrary", "arbitrary"),
                                             vmem_limit_bytes=VMEM_LIMIT),
        name="inproj",
    )(*args)


def _attn_kernel(sinks_ref, q_ref, kc_ref, kp_ref, vc_ref, vp_ref, o_ref, *, tq):
    i = pl.program_id(1)
    nblk = tq // WINDOW
    lane = lax.broadcasted_iota(jnp.int32, (WINDOW, LANES), 1)
    low = lane < HEAD_DIM
    qi = lax.broadcasted_iota(jnp.int32, (WINDOW, 2 * WINDOW), 0)
    kj = lax.broadcasted_iota(jnp.int32, (WINDOW, 2 * WINDOW), 1)
    diff = qi + WINDOW - kj
    band = (diff >= 0) & (diff < WINDOW)
    zero_bf = jnp.zeros((WINDOW, LANES), BF16)
    for blk in range(nblk):
        r0 = blk * WINDOW
        if blk == 0:
            k_prev, v_prev = kp_ref[0], vp_ref[0]
            mask = band & ((i > 0) | (kj >= WINDOW))
        else:
            k_prev = kc_ref[0, r0 - WINDOW:r0, :]
            v_prev = vc_ref[0, r0 - WINDOW:r0, :]
            mask = band
        kk = jnp.concatenate([k_prev, kc_ref[0, r0:r0 + WINDOW, :]], axis=0)
        vv = jnp.concatenate([v_prev, vc_ref[0, r0:r0 + WINDOW, :]], axis=0)
        for pair in range(N_Q_HEADS // 2):
            g = (2 * pair) // (N_Q_HEADS // N_KV_HEADS)
            qp = q_ref[0, r0:r0 + WINDOW, pair * LANES:(pair + 1) * LANES]
            outs = []
            for e in range(2):
                h = 2 * pair + e
                var = 0 if g == e else 1
                qh = jnp.where(low if e == 0 else ~low, qp, zero_bf)
                s = _dot_nt(qh, kk[:, var * LANES:(var + 1) * LANES])
                s = jnp.where(mask, s, -jnp.inf)
                sk = sinks_ref[h]
                m = jnp.maximum(jnp.max(s, axis=-1, keepdims=True), sk)
                p = jnp.exp(s - m)
                den = jnp.sum(p, axis=-1, keepdims=True) + jnp.exp(sk - m)
                o = _dot(p.astype(BF16), vv[:, var * LANES:(var + 1) * LANES])
                outs.append(o / den)
            o_ref[0, r0:r0 + WINDOW, pair * LANES:(pair + 1) * LANES] = (
                jnp.where(low, outs[0], outs[1]).astype(BF16))


def _attention(sinks, q, k2, v2, *, tq):
    nb, seq, _ = q.shape
    r = tq // WINDOW
    cur = lambda w: pl.BlockSpec((1, tq, w), lambda b, i: (b, i, 0))
    prev = lambda w: pl.BlockSpec((1, WINDOW, w), lambda b, i: (b, jnp.maximum(i * r - 1, 0), 0))
    return pl.pallas_call(
        functools.partial(_attn_kernel, tq=tq),
        grid=(nb, seq // tq),
        in_specs=[pl.BlockSpec(memory_space=pltpu.SMEM),
                  cur(ATTN_WIDTH), cur(2 * KV_WIDTH), prev(2 * KV_WIDTH), cur(2 * KV_WIDTH), prev(2 * KV_WIDTH)],
        out_specs=cur(ATTN_WIDTH),
        out_shape=jax.ShapeDtypeStruct((nb, seq, ATTN_WIDTH), BF16),
        compiler_params=pltpu.CompilerParams(dimension_semantics=("arbitrary", "arbitrary"),
                                             vmem_limit_bytes=VMEM_LIMIT),
        name="swa_attn",
    )(sinks, q, k2, k2, v2, v2)


def _ssd_kernel(xs_ref, bc_ref, dt_ref, a_ref, dskip_ref, tri_ref,
                y_ref, hT_ref, h_s, *, tc):
    l = pl.program_id(1)
    n_l = pl.num_programs(1)
    gw = SSM_WIDTH // N_SSM_GROUPS
    hpg = N_SSM_HEADS // N_SSM_GROUPS

    @pl.when(l == 0)
    def _():
        h_s[...] = jnp.zeros(h_s.shape, F32)

    lane = lax.broadcasted_iota(jnp.int32, (SSD_CHUNK, LANES), 1)
    low = lane < SSM_HEAD_DIM
    qi = lax.broadcasted_iota(jnp.int32, (SSD_CHUNK, SSD_CHUNK), 0)
    sj = lax.broadcasted_iota(jnp.int32, (SSD_CHUNK, SSD_CHUNK), 1)
    causal = sj <= qi
    a_neg = -jnp.exp(a_ref[...])
    tri = tri_ref[...]
    zero_bf = jnp.zeros((SSD_CHUNK, LANES), BF16)

    for c in range(tc // SSD_CHUNK):
        rows = slice(c * SSD_CHUNK, (c + 1) * SSD_CHUNK)
        dt = dt_ref[0, rows, :]
        hi, mid, lo = _split3(dt * a_neg)
        acum = (_dot(tri, hi) + _dot(tri, mid) + _dot(tri, lo)) * LOG2E
        acum_t = acum.T
        dt_t = dt.T
        a_end_t = acum_t[:, SSD_CHUNK - 1:SSD_CHUNK]
        w_end_t = dt_t * jnp.exp2(a_end_t - acum_t)
        dec_t = jnp.exp2(a_end_t)
        dec_t = jnp.broadcast_to(dec_t, (LANES, LANES))
        row_t = acum_t - jnp.log2(dt_t)
        for g in range(N_SSM_GROUPS):
            b_g = bc_ref[0, rows, g * D_STATE:(g + 1) * D_STATE]
            c_g = bc_ref[0, rows, (N_SSM_GROUPS + g) * D_STATE:(N_SSM_GROUPS + g + 1) * D_STATE]
            cb = _dot_nt(c_g, b_g)
            b_gt = b_g.astype(F32).T
            y_inter = _dot(c_g, h_s[g].astype(BF16))
            for pr in range(hpg // 2):
                lhs_y, lhs_s, escale, dsc = [], [], [], []
                for e in range(2):
                    h = g * hpg + 2 * pr + e
                    col = jnp.broadcast_to(acum[:, h:h + 1], (SSD_CHUNK, SSD_CHUNK))
                    decay_dt = jnp.where(causal, jnp.exp2(col - row_t[h:h + 1, :]), 0.0)
                    lhs_y.append((cb * decay_dt).astype(BF16))
                    lhs_s.append((b_gt * w_end_t[h:h + 1, :]).astype(BF16))
                    escale.append(jnp.exp2(col))
                    dsc.append(dec_t[h:h + 1, :])
                c0 = g * gw + pr * LANES
                x_pair = xs_ref[0, rows, c0:c0 + LANES]
                x_bf = x_pair.astype(BF16)
                rhs = jnp.concatenate([jnp.where(low, x_bf, zero_bf), jnp.where(low, zero_bf, x_bf)], axis=0)
                y_in = _dot(jnp.concatenate(lhs_y, axis=1), rhs)
                st = _dot(jnp.concatenate(lhs_s, axis=1), rhs)
                lc = pr * LANES
                y = y_in + y_inter[:, lc:lc + LANES] * jnp.where(low, escale[0], escale[1])
                y_ref[0, rows, c0:c0 + LANES] = y + dskip_ref[:, c0:c0 + LANES] * x_pair
                h_s[g, :, lc:lc + LANES] = h_s[g, :, lc:lc + LANES] * jnp.where(low, dsc[0], dsc[1]) + st

    @pl.when(l == n_l - 1)
    def _():
        hT_ref[0] = h_s[...]


def _ssd(xs, bc, dt, a_pad, dskip, tri, *, tc):
    nb, seq, _ = xs.shape
    row = lambda w: pl.BlockSpec((1, tc, w), lambda b, l: (b, l, 0))
    const = lambda shape: pl.BlockSpec(shape, lambda b, l: (0,) * len(shape))
    gw = SSM_WIDTH // N_SSM_GROUPS
    return pl.pallas_call(
        functools.partial(_ssd_kernel, tc=tc),
        grid=(nb, seq // tc),
        in_specs=[row(SSM_WIDTH), row(CONV_DIM - SSM_WIDTH), row(LANES),
                  const((1, LANES)), const((1, SSM_WIDTH)), const((SSD_CHUNK, SSD_CHUNK))],
        out_specs=[row(SSM_WIDTH),
                   pl.BlockSpec((1, N_SSM_GROUPS, D_STATE, gw), lambda b, l: (b, 0, 0, 0))],
        out_shape=[jax.ShapeDtypeStruct((nb, seq, SSM_WIDTH), F32),
                   jax.ShapeDtypeStruct((nb, N_SSM_GROUPS, D_STATE, gw), F32)],
        scratch_shapes=[pltpu.VMEM((N_SSM_GROUPS, D_STATE, gw), F32)],
        compiler_params=pltpu.CompilerParams(dimension_semantics=("arbitrary", "arbitrary"),
                                             vmem_limit_bytes=VMEM_LIMIT),
        name="ssd_scan",
    )(xs, bc, dt, a_pad, dskip, tri)


def _outffn_kernel(*refs, ffc, gated):
    if gated:
        x_ref, oa_ref, ys_ref, z_ref, gnorm_ref, wo_ref, gffn_ref, wgu_ref, wd_ref, o_ref = refs
        gw = SSM_WIDTH // N_SSM_GROUPS
        parts = []
        for g in range(N_SSM_GROUPS):
            ls = slice(g * gw, (g + 1) * gw)
            yg = ys_ref[:, ls] * _silu(z_ref[:, ls])
            yg = yg * lax.rsqrt(jnp.mean(yg * yg, axis=-1, keepdims=True) + EPS)
            parts.append((yg * gnorm_ref[:, ls]).astype(BF16))
        ys = jnp.concatenate(parts, axis=1)
    else:
        x_ref, oa_ref, ys_ref, wo_ref, gffn_ref, wgu_ref, wd_ref, o_ref = refs
        ys = ys_ref[...]
    x = x_ref[...]
    xm = x + _dot(oa_ref[...], wo_ref[0:ATTN_WIDTH, :]) + _dot(ys, wo_ref[ATTN_WIDTH:, :])
    hn = _rms_rows(xm, gffn_ref[...]).astype(BF16)
    acc = jnp.zeros_like(xm)
    c0 = 0
    for width in ffc:
        g = _dot(hn, wgu_ref[:, c0:c0 + width])
        up = _dot(hn, wgu_ref[:, D_FF + c0:D_FF + c0 + width])
        acc = acc + _dot((_silu(g) * up).astype(BF16), wd_ref[c0:c0 + width, :])
        c0 += width
    o_ref[...] = xm + acc


def _outffn(x, oa, ys, gate, wo, gffn, wgu, wd, *, tm, ffc):
    rows = x.shape[0]
    row = lambda w: pl.BlockSpec((tm, w), lambda i: (i, 0))
    const = lambda shape: pl.BlockSpec(shape, lambda i: (0,) * len(shape), pipeline_mode=pl.Buffered(1))
    gated = gate is not None
    in_specs = [row(D_MODEL), row(ATTN_WIDTH), row(SSM_WIDTH)]
    args = [x, oa, ys]
    if gated:
        in_specs += [row(SSM_WIDTH), const((1, SSM_WIDTH))]
        args += list(gate)
    in_specs += [const((ATTN_WIDTH + SSM_WIDTH, D_MODEL)), const((1, D_MODEL)),
                 const((D_MODEL, 2 * D_FF)), const((D_FF, D_MODEL))]
    args += [wo, gffn, wgu, wd]
    return pl.pallas_call(
        functools.partial(_outffn_kernel, ffc=ffc, gated=gated),
        grid=(rows // tm,),
        in_specs=in_specs,
        out_specs=row(D_MODEL),
        out_shape=jax.ShapeDtypeStruct((rows, D_MODEL), F32),
        compiler_params=pltpu.CompilerParams(dimension_semantics=("arbitrary",),
                                             vmem_limit_bytes=VMEM_LIMIT),
        name="outproj_ffn",
    )(*args)


T_PAD = SUBLANES


def _attn_dec_kernel(q_ref, kc_ref, kn_ref, vc_ref, vn_ref, sink_ref, o_ref, *, t_new):
    q = q_ref[...]
    nbat, nq, _ = q.shape
    qpk = N_Q_HEADS // N_KV_HEADS
    s_c = jnp.einsum('bqd,bkd->bqk', q, kc_ref[...], preferred_element_type=F32)
    s_n = jnp.einsum('bqd,bkd->bqk', q, kn_ref[...], preferred_element_type=F32)
    n_c = s_c.shape[-1]
    t_c = lax.broadcasted_iota(jnp.int32, (nbat, nq, n_c), 1) // qpk
    j_c = lax.broadcasted_iota(jnp.int32, (nbat, nq, n_c), 2)
    s_c = jnp.where(j_c > t_c + (n_c - WINDOW), s_c, -jnp.inf)
    t_n = lax.broadcasted_iota(jnp.int32, (nbat, nq, T_PAD), 1) // qpk
    j_n = lax.broadcasted_iota(jnp.int32, (nbat, nq, T_PAD), 2)
    s_n = jnp.where((j_n <= t_n) & (j_n < t_new), s_n, -jnp.inf)
    sk = sink_ref[...]
    m = jnp.maximum(jnp.maximum(jnp.max(s_c, axis=-1, keepdims=True), jnp.max(s_n, axis=-1, keepdims=True)), sk)
    p_c = jnp.exp(s_c - m)
    p_n = jnp.exp(s_n - m)
    den = jnp.sum(p_c, axis=-1, keepdims=True) + jnp.sum(p_n, axis=-1, keepdims=True) + jnp.exp(sk - m)
    o = (jnp.einsum('bqk,bkd->bqd', p_c.astype(BF16), vc_ref[...], preferred_element_type=F32)
         + jnp.einsum('bqk,bkd->bqd', p_n.astype(BF16), vn_ref[...], preferred_element_type=F32))
    o_ref[...] = (o / den).astype(BF16)


def _attention_dec(q, kc, kn, vc, vn, sink_rows, *, t_new, bb):
    nbat, nq, hd = q.shape
    n_c = kc.shape[1]
    blk = lambda r, w: pl.BlockSpec((bb, r, w), lambda i: (i, 0, 0))
    return pl.pallas_call(
        functools.partial(_attn_dec_kernel, t_new=t_new),
        grid=(nbat // bb,),
        in_specs=[blk(nq, hd), blk(n_c, hd), blk(T_PAD, hd), blk(n_c, hd), blk(T_PAD, hd), blk(nq, 1)],
        out_specs=blk(nq, hd),
        out_shape=jax.ShapeDtypeStruct((nbat, nq, hd), BF16),
        compiler_params=pltpu.CompilerParams(dimension_semantics=("arbitrary",), vmem_limit_bytes=VMEM_LIMIT),
        name="swa_attn_decode",
    )(q, kc, kn, vc, vn, sink_rows)


def _ssd_dec_kernel(*refs, bb, t_new, n_prev):
    (xs_ref, bc_ref, dt_ref, z_ref, h0_ref) = refs[:5]
    prev_refs = refs[5:5 + n_prev]
    (a_ref, dskip_ref, gnorm_ref, tri_ref, ones_ref, exp_ref,
     y_ref, h_ref, aw_s, xdt_s, xw_s, ea_s, dec_s) = refs[5 + n_prev:]
    d = pl.program_id(0)
    gw = SSM_WIDTH // N_SSM_GROUPS
    hpg = N_SSM_HEADS // N_SSM_GROUPS
    rows = bb * T_PAD

    for j in range(n_prev):
        @pl.when(d == j)
        def _(j=j):
            h_ref[0] = prev_refs[j][0]

    @pl.when(d == n_prev)
    def _():
        expand = exp_ref[...]

        def widen(v):
            hi, mid, lo = _split3(v)
            return _dot(hi, expand) + _dot(mid, expand) + _dot(lo, expand)

        def seq_sum(m, v):
            hi, mid, lo = _split3(v)
            return _dot(m, hi) + _dot(m, mid) + _dot(m, lo)

        dt = dt_ref[...].reshape(rows, LANES)
        dta = dt * (-jnp.exp(a_ref[...]))
        a_w = widen(seq_sum(tri_ref[...], dta))
        a_end = seq_sum(ones_ref[...], dta)
        xdt = xs_ref[...].reshape(rows, SSM_WIDTH) * widen(dt)
        aw_s[...] = a_w
        xdt_s[...] = xdt
        xw_s[...] = xdt * jnp.exp(widen(a_end) - a_w)
        ea_s[...] = jnp.exp(a_w)
        dec_s[...] = jnp.exp(a_end)

        t_row = lax.broadcasted_iota(jnp.int32, (T_PAD, gw), 0)

        def seq_body(i, carry):
            r0 = pl.multiple_of(i * T_PAD, T_PAD)
            rs = pl.ds(r0, T_PAD)
            dec = jnp.broadcast_to(dec_s[pl.ds(r0, 1), :], (LANES, LANES)).T
            bc = bc_ref[i]
            for g in range(N_SSM_GROUPS):
                b_g = bc[:, g * D_STATE:(g + 1) * D_STATE]
                c_g = bc[:, (N_SSM_GROUPS + g) * D_STATE:(N_SSM_GROUPS + g + 1) * D_STATE]
                cb = _dot_nt(c_g, b_g)
                ls = slice(g * gw, (g + 1) * gw)
                a_g = aw_s[rs, ls]
                xdt_g = xdt_s[rs, ls]
                y = jnp.zeros((T_PAD, gw), F32)
                for s in range(t_new):
                    w = jnp.where(t_row >= s, jnp.exp(a_g - a_g[s:s + 1, :]), 0.0)
                    y = y + (w * cb[:, s:s + 1]) * xdt_g[s:s + 1, :]
                h0g = h0_ref[0, i, g * hpg:(g + 1) * hpg].reshape(gw, D_STATE)
                y = y + _dot_nt(c_g, h0g.astype(BF16)) * ea_s[rs, ls]
                y = y + dskip_ref[:, ls] * xs_ref[i, :, ls]
                y = y * _silu(z_ref[i, :, ls])
                y = y * lax.rsqrt(jnp.mean(y * y, axis=-1, keepdims=True) + EPS)
                y_ref[i, :, ls] = (y * gnorm_ref[:, ls]).astype(BF16)
                delta = lax.dot_general(xw_s[rs, ls].astype(BF16), b_g, (((0,), (0,)), ((), ())),
                                        preferred_element_type=F32)
                for hl in range(hpg):
                    h = g * hpg + hl
                    h_ref[0, i, h] = (h0_ref[0, i, h] * dec[h:h + 1, :]
                                      + delta[hl * SSM_HEAD_DIM:(hl + 1) * SSM_HEAD_DIM, :])
            return carry

        lax.fori_loop(0, bb, seq_body, 0)


def _ssd_dec(xs3, bc3, dt3, z3, state_all, layer, prev_states, a_pad, dskip, gnorm, tri_b, ones_b, expand,
             *, bb, t_new):
    nb = xs3.shape[0]
    nblk = nb // bb
    n_prev = len(prev_states)
    rows = bb * T_PAD

    def phase_block(j):
        return lambda d, i: jnp.clip(i + (d - j) * nblk, 0, nblk - 1)

    cur = phase_block(n_prev)
    blk = lambda w: pl.BlockSpec((bb, T_PAD, w), lambda d, i: (cur(d, i), 0, 0))
    const = lambda shape: pl.BlockSpec(shape, lambda d, i: (0,) * len(shape))
    st_shape = (1, bb, N_SSM_HEADS, SSM_HEAD_DIM, D_STATE)
    in_specs = [blk(SSM_WIDTH), blk(CONV_DIM - SSM_WIDTH), blk(LANES), blk(SSM_WIDTH),
                pl.BlockSpec(st_shape, lambda d, i: (layer, cur(d, i), 0, 0, 0))]
    for j in range(n_prev):
        in_specs.append(pl.BlockSpec(st_shape, lambda d, i, j=j: (0, phase_block(j)(d, i), 0, 0, 0)))
    in_specs += [const((1, LANES)), const((1, SSM_WIDTH)), const((1, SSM_WIDTH)),
                 const((rows, rows)), const((rows, rows)), const((LANES, SSM_WIDTH))]
    return pl.pallas_call(
        functools.partial(_ssd_dec_kernel, bb=bb, t_new=t_new, n_prev=n_prev),
        grid=(n_prev + 1, nblk),
        in_specs=in_specs,
        out_specs=[blk(SSM_WIDTH), pl.BlockSpec(st_shape, lambda d, i: (d, i, 0, 0, 0))],
        out_shape=[jax.ShapeDtypeStruct((nb, T_PAD, SSM_WIDTH), BF16),
                   jax.ShapeDtypeStruct((n_prev + 1,) + state_all.shape[1:], F32)],
        scratch_shapes=[pltpu.VMEM((rows, SSM_WIDTH), F32), pltpu.VMEM((rows, SSM_WIDTH), F32),
                        pltpu.VMEM((rows, SSM_WIDTH), F32), pltpu.VMEM((rows, SSM_WIDTH), F32),
                        pltpu.VMEM((rows, LANES), F32)],
        compiler_params=pltpu.CompilerParams(dimension_semantics=("arbitrary", "arbitrary"),
                                             vmem_limit_bytes=VMEM_LIMIT),
        name="ssd_decode",
    )(xs3, bc3, dt3, z3, state_all, *prev_states, a_pad, dskip, gnorm, tri_b, ones_b, expand)


def _rope_tables(pos):
    half = HEAD_DIM // 2
    inv = ROPE_THETA ** (-np.arange(half, dtype=np.float64) / half)
    ang = np.asarray(pos, np.float64)[:, None] * inv[None, :]
    cos = np.tile(np.cos(ang), (1, LANES // half))
    sin = np.sin(ang)
    sin_signed = np.tile(np.concatenate([-sin, sin], axis=1), (1, LANES // HEAD_DIM))
    return jnp.asarray(cos, F32), jnp.asarray(sin_signed, F32)


def _layer_consts(l, p):
    w = p['w_in'][l]
    o_z = ATTN_WIDTH + 2 * KV_WIDTH
    o_xbc = o_z + SSM_WIDTH
    o_dt = o_xbc + CONV_DIM
    w_in = jnp.concatenate(
        [w[:, :o_z], jnp.pad(w[:, o_dt:], ((0, 0), (0, LANES - N_SSM_HEADS))), w[:, o_z:o_dt]], axis=1).astype(BF16)
    return dict(
        gmix=p['norm_mix'][l][None, :],
        win=w_in,
        qn=jnp.tile(p['q_norm'][l], LANES // HEAD_DIM)[None, :],
        kn=jnp.tile(p['k_norm'][l], LANES // HEAD_DIM)[None, :],
        convw=p['conv_w'][l],
        convb=p['conv_b'][l][None, :],
        dtb=jnp.pad(p['dt_bias'][l], (0, LANES - N_SSM_HEADS))[None, :],
        a_pad=jnp.pad(p['a_log'][l], (0, LANES - N_SSM_HEADS))[None, :],
        dskip=jnp.repeat(p['d_skip'][l], SSM_HEAD_DIM)[None, :],
        gnorm=p['ssm_norm'][l][None, :],
        sinks=p['sinks'][l],
        wo=p['w_out'][l].astype(BF16),
        gffn=p['norm_ffn'][l][None, :],
        wgu=p['w_gate_up'][l].astype(BF16),
        wd=p['w_down'][l].astype(BF16),
    )


def _state_from_t(h_t):
    nb = h_t.shape[0]
    hpg = N_SSM_HEADS // N_SSM_GROUPS
    h = h_t.reshape(nb, N_SSM_GROUPS, D_STATE, hpg, SSM_HEAD_DIM)
    return jnp.transpose(h, (0, 1, 3, 4, 2)).reshape(nb, N_SSM_HEADS, SSM_HEAD_DIM, D_STATE)


def _prompt_layer(x, c, shared, *, tm):
    nb, seq, _ = x.shape
    (q, k2, v2, z, xs, bc, dt, knew, vnew, convst) = _inproj(
        x, c['gmix'], c['win'], shared['m128'], c['qn'], c['kn'], shared['cos_p'], shared['sin_p'],
        c['convw'], c['convb'], c['dtb'], None, tm=tm, carry_rows=SUBLANES, shift=1, n_keep=min(WINDOW, seq))
    oa = _attention(c['sinks'], q, k2, v2, tq=tm)
    ys, h_t = _ssd(xs, bc, dt, c['a_pad'], c['dskip'], shared['tri'], tc=tm)
    rows = nb * seq
    xo = _outffn(x.reshape(rows, D_MODEL), oa.reshape(rows, ATTN_WIDTH), ys.reshape(rows, SSM_WIDTH),
                 (z.reshape(rows, SSM_WIDTH), c['gnorm']), c['wo'], c['gffn'], c['wgu'], c['wd'], tm=tm,
                 ffc=FF_CHUNKS)
    n_keep = knew.shape[1]
    return (xo.reshape(nb, seq, D_MODEL),
            knew.reshape(nb, n_keep, N_KV_HEADS, HEAD_DIM), vnew.reshape(nb, n_keep, N_KV_HEADS, HEAD_DIM),
            convst[:, SUBLANES - (CONV_W - 1):, :], _state_from_t(h_t))


def _sample_layer(x_t, cache_k, cache_v, conv0, state_all, layer, prev_states, c, shared, *, n_seq, t_new):
    rows = t_new * n_seq
    qpk = N_Q_HEADS // N_KV_HEADS
    n_c = cache_k.shape[1]
    conv0_t = jnp.transpose(conv0, (1, 0, 2)).reshape((CONV_W - 1) * n_seq, CONV_DIM)
    (q, _, _, z, xs, bc, dt, knew, vnew, convst) = _inproj(
        x_t, c['gmix'], c['win'], shared['m128'], c['qn'], c['kn'], shared['cos_s'], shared['sin_s'],
        c['convw'], c['convb'], c['dtb'], conv0_t, tm=rows, carry_rows=(CONV_W - 1) * n_seq, shift=n_seq,
        n_keep=rows)

    def to_bg(a, t, inner):
        a = a.reshape(t, n_seq, N_KV_HEADS, inner, HEAD_DIM)
        return jnp.transpose(a, (1, 2, 0, 3, 4)).reshape(n_seq * N_KV_HEADS, t * inner, HEAD_DIM)

    def pad_t(a):
        return jnp.pad(a, ((0, 0), (0, T_PAD - a.shape[1]), (0, 0)))

    def cache_bg(a):
        return jnp.transpose(a, (0, 2, 1, 3)).reshape(n_seq * N_KV_HEADS, n_c, HEAD_DIM).astype(BF16)

    q_bg = to_bg(q[0], t_new, qpk)
    kn_bg = pad_t(to_bg(knew[0], t_new, 1)).astype(BF16)
    vn_bg = pad_t(to_bg(vnew[0], t_new, 1)).astype(BF16)
    sink_rows = jnp.tile(c['sinks'].reshape(N_KV_HEADS, 1, qpk), (n_seq, t_new, 1)).reshape(
        n_seq * N_KV_HEADS, t_new * qpk, 1)
    bb_a = min(64, n_seq * N_KV_HEADS)
    o_bg = _attention_dec(q_bg, cache_bg(cache_k), kn_bg, cache_bg(cache_v), vn_bg, sink_rows,
                          t_new=t_new, bb=bb_a)
    oa = jnp.transpose(o_bg.reshape(n_seq, N_KV_HEADS, t_new, qpk, HEAD_DIM), (2, 0, 1, 3, 4)).reshape(
        rows, ATTN_WIDTH)

    def to_seq(a):
        a = jnp.transpose(a[0].reshape(t_new, n_seq, a.shape[-1]), (1, 0, 2))
        return pad_t(a)

    y3, h_new = _ssd_dec(to_seq(xs), to_seq(bc), to_seq(dt), to_seq(z), state_all, layer, prev_states,
                         c['a_pad'], c['dskip'], c['gnorm'], shared['tri_b'], shared['ones_b'], shared['expand'],
                         bb=shared['bb_s'], t_new=t_new)
    ys = jnp.transpose(y3[:, :t_new], (1, 0, 2)).reshape(rows, SSM_WIDTH)

    xo = _outffn(x_t[0], oa, ys, None, c['wo'], c['gffn'], c['wgu'], c['wd'], tm=min(512, rows), ffc=FF_CHUNKS)

    def new_rows(a):
        return jnp.transpose(a.reshape(t_new, n_seq, N_KV_HEADS, HEAD_DIM), (1, 0, 2, 3))

    win_k = jnp.concatenate([cache_k, new_rows(knew[0])], axis=1)[:, t_new:]
    win_v = jnp.concatenate([cache_v, new_rows(vnew[0])], axis=1)[:, t_new:]
    conv_new = jnp.transpose(convst[0].reshape(CONV_W - 1, n_seq, CONV_DIM), (1, 0, 2))
    return xo[None], win_k, win_v, conv_new, h_new


def kernel(x_prompt, x_sample, cache_win_k, cache_win_v, state_conv, state_ssm,
           norm_mix, w_in, q_norm, k_norm, sinks, conv_w, conv_b, dt_bias, a_log,
           d_skip, ssm_norm, w_out, norm_ffn, w_gate_up, w_down):
    p = dict(norm_mix=norm_mix, w_in=w_in, q_norm=q_norm, k_norm=k_norm, sinks=sinks, conv_w=conv_w,
             conv_b=conv_b, dt_bias=dt_bias, a_log=a_log, d_skip=d_skip, ssm_norm=ssm_norm, w_out=w_out,
             norm_ffn=norm_ffn, w_gate_up=w_gate_up, w_down=w_down)
    depth = w_in.shape[0]
    seq = x_prompt.shape[1]
    cos_p, sin_p = _rope_tables(np.arange(seq))
    half_blk = jnp.arange(LANES) // HEAD_DIM
    m128 = (jnp.where(half_blk[:, None] == half_blk[None, :], 1.0 / HEAD_DIM, 0.0)).astype(BF16)
    m128 = jnp.concatenate([m128, m128], axis=0)
    tri = (jnp.arange(SSD_CHUNK)[:, None] >= jnp.arange(SSD_CHUNK)[None, :]).astype(BF16)
    n_seq, t_new, _ = x_sample.shape
    pos_s = PAST_LEN + np.repeat(np.arange(t_new), n_seq)
    cos_s, sin_s = _rope_tables(pos_s)
    bb_s = min(8, n_seq)
    r_idx = jnp.arange(bb_s * T_PAD)
    same_seq = (r_idx[:, None] // T_PAD) == (r_idx[None, :] // T_PAD)
    tri_b = (same_seq & (r_idx[:, None] >= r_idx[None, :])).astype(BF16)
    ones_b = same_seq.astype(BF16)
    expand = (jnp.arange(LANES)[:, None] == (jnp.arange(SSM_WIDTH) // SSM_HEAD_DIM)[None, :]).astype(BF16)
    shared = dict(cos_p=cos_p, sin_p=sin_p, cos_s=cos_s, sin_s=sin_s, m128=m128, tri=tri, tri_b=tri_b,
                  ones_b=ones_b, bb_s=bb_s, expand=expand)
    tm = min(512, seq)

    xp = x_prompt
    xs = jnp.transpose(x_sample, (1, 0, 2)).reshape(1, t_new * n_seq, D_MODEL)
    pk, pv, pc, ph = [], [], [], []
    sk, sv, sc, sh = [], [], [], []
    for l in range(depth):
        c = _layer_consts(l, p)
        xp, k1, v1, c1, h1 = _prompt_layer(xp, c, shared, tm=tm)
        prev = sh if l == depth - 1 else []
        xs, k2, v2, c2, h2 = _sample_layer(xs, cache_win_k[l], cache_win_v[l], state_conv[l], state_ssm, l, prev,
                                           c, shared, n_seq=n_seq, t_new=t_new)
        pk.append(k1); pv.append(v1); pc.append(c1); ph.append(h1)
        sk.append(k2); sv.append(v2); sc.append(c2); sh.append(h2)
    ys = jnp.transpose(xs.reshape(t_new, n_seq, D_MODEL), (1, 0, 2))
    return (xp, ys, jnp.stack(pk), jnp.stack(pv), jnp.stack(pc), jnp.stack(ph),
            jnp.stack(sk), jnp.stack(sv), jnp.stack(sc), sh[-1])
```

```python
import functools
import math

import numpy as np
import jax
import jax.numpy as jnp
from jax import lax
from jax.experimental import pallas as pl
from jax.experimental.pallas import tpu as pltpu

F32 = jnp.float32
BF16 = jnp.bfloat16

D_MODEL = 1024
HEAD_DIM = 64
N_Q_HEADS = 8
N_KV_HEADS = 2
ATTN_WIDTH = N_Q_HEADS * HEAD_DIM
KV_WIDTH = N_KV_HEADS * HEAD_DIM
WINDOW = 128
ROPE_THETA = 10000.0
ATTN_SCALE = HEAD_DIM ** -0.5
SSM_HEAD_DIM = 64
N_SSM_HEADS = 16
SSM_WIDTH = N_SSM_HEADS * SSM_HEAD_DIM
N_SSM_GROUPS = 2
D_STATE = 128
CONV_W = 4
CONV_DIM = SSM_WIDTH + 2 * N_SSM_GROUPS * D_STATE
SSD_CHUNK = 128
D_FF = 2816
EPS = 1e-6
LOG2E = math.log2(math.e)
PAST_LEN = 16384

LANES = 128
SUBLANES = 8

COL_Q = 0
COL_K = COL_Q + ATTN_WIDTH
COL_V = COL_K + KV_WIDTH
COL_DT = COL_V + KV_WIDTH
COL_Z = COL_DT + LANES
COL_XBC = COL_Z + SSM_WIDTH
PROJ_PAD = COL_XBC + CONV_DIM
HEAD_COLS = COL_Z

VMEM_LIMIT = 56 * 1024 * 1024
FF_CHUNKS = (6 * 256, 5 * 256)


def _dot(a, b):
    return jnp.dot(a, b, preferred_element_type=F32)


def _dot_nt(a, b):
    return lax.dot_general(a, b, (((1,), (1,)), ((), ())), preferred_element_type=F32)


def _silu(x):
    return x / (1.0 + jnp.exp(-x))


def _softplus(x):
    return jnp.maximum(x, 0.0) + jnp.log1p(jnp.exp(-jnp.abs(x)))


def _rms_rows(x, g):
    return x * lax.rsqrt(jnp.mean(x * x, axis=-1, keepdims=True) + EPS) * g


def _split2(x):
    hi = x.astype(BF16)
    lo = (x - hi.astype(F32)).astype(BF16)
    return hi, lo


def _split3(x):
    hi = x.astype(BF16)
    r1 = x - hi.astype(F32)
    mid = r1.astype(BF16)
    lo = (r1 - mid.astype(F32)).astype(BF16)
    return hi, mid, lo


def _head_norm_rope(x, m128, g, cos, sin_signed, low_half):
    hi, lo = _split2(x * x)
    ms = _dot(jnp.concatenate([hi, lo], axis=1), m128)
    xn = x * lax.rsqrt(ms + EPS) * g
    rot = jnp.where(low_half, pltpu.roll(xn, LANES - HEAD_DIM // 2, 1),
                    pltpu.roll(xn, HEAD_DIM // 2, 1))
    return xn * cos + rot * sin_signed


def _inproj_kernel(*refs, tm, carry_rows, shift, n_keep, has_state, n_parts):
    if has_state:
        (x_ref, gmix_ref, win_ref, m128_ref, qn_ref, kn_ref, cos_ref, sin_ref,
         convw_ref, convb_ref, dtb_ref, conv0_ref,
         q_ref, k2_ref, v2_ref, z_ref, xs_ref, bc_ref, dt_ref, knew_ref, vnew_ref, convst_ref,
         xbc_s) = refs
    else:
        (x_ref, gmix_ref, win_ref, m128_ref, qn_ref, kn_ref, cos_ref, sin_ref,
         convw_ref, convb_ref, dtb_ref,
         q_ref, k2_ref, v2_ref, z_ref, xs_ref, bc_ref, dt_ref, knew_ref, vnew_ref, convst_ref,
         xbc_s) = refs
        conv0_ref = None
    l = pl.program_id(1)

    @pl.when(l == 0)
    def _():
        if has_state:
            xbc_s[0:carry_rows, :] = conv0_ref[...]
        else:
            xbc_s[0:carry_rows, :] = jnp.zeros((carry_rows, CONV_DIM), F32)

    m128 = m128_ref[...]
    cw = 512
    nr = tm // n_parts
    lane = lax.broadcasted_iota(jnp.int32, (nr, LANES), 1)
    low_half = (lane % HEAD_DIM) < (HEAD_DIM // 2)
    for part in range(n_parts):
        r0 = part * nr
        rs = slice(r0, r0 + nr)
        base = carry_rows + r0
        u = _rms_rows(x_ref[0, rs, :], gmix_ref[...]).astype(BF16)
        for j in range(CONV_DIM // cw):
            c0 = COL_XBC + j * cw
            xbc_s[base:base + nr, j * cw:(j + 1) * cw] = _dot(u, win_ref[:, c0:c0 + cw])
        z_ref[0, rs, :] = _dot(u, win_ref[:, COL_Z:COL_Z + SSM_WIDTH])
        heads = _dot(u, win_ref[:, 0:HEAD_COLS])
        cos = cos_ref[rs, :]
        sin = sin_ref[rs, :]

        for j in range(ATTN_WIDTH // LANES):
            qj = heads[:, COL_Q + j * LANES:COL_Q + (j + 1) * LANES]
            qj = _head_norm_rope(qj, m128, qn_ref[...], cos, sin, low_half)
            q_ref[0, rs, j * LANES:(j + 1) * LANES] = (qj * ATTN_SCALE).astype(BF16)

        k = _head_norm_rope(heads[:, COL_K:COL_K + KV_WIDTH], m128, kn_ref[...], cos, sin, low_half)
        v = heads[:, COL_V:COL_V + KV_WIDTH]
        k2_ref[0, rs, 0:LANES] = k.astype(BF16)
        k2_ref[0, rs, LANES:2 * LANES] = pltpu.roll(k, HEAD_DIM, 1).astype(BF16)
        v2_ref[0, rs, 0:LANES] = v.astype(BF16)
        v2_ref[0, rs, LANES:2 * LANES] = pltpu.roll(v, HEAD_DIM, 1).astype(BF16)
        if part == n_parts - 1:
            knew_ref[0] = k[nr - n_keep:, :]
            vnew_ref[0] = v[nr - n_keep:, :]

        dt_ref[0, rs, :] = _softplus(heads[:, COL_DT:COL_DT + LANES] + dtb_ref[...])

        for j in range(CONV_DIM // cw):
            cs = slice(j * cw, (j + 1) * cw)
            w = [convw_ref[i:i + 1, cs] for i in range(CONV_W)]
            bias = convb_ref[:, cs]
            if shift % SUBLANES == 0:
                acc = bias
                for i in range(CONV_W):
                    off = base - (CONV_W - 1 - i) * shift
                    acc = acc + xbc_s[off:off + nr, cs] * w[i]
                act = _silu(acc)
            else:
                cur = xbc_s[base:base + nr, cs]
                acc = cur * w[0]
                for i in range(1, CONV_W):
                    acc = pltpu.roll(acc, shift, 0) + cur * w[i]
                head = bias
                for i in range(CONV_W):
                    off = base - (CONV_W - 1 - i) * shift
                    head = head + xbc_s[off:off + SUBLANES, cs] * w[i]
                act = _silu(jnp.concatenate([head, acc[SUBLANES:, :] + bias], axis=0))
            if j * cw < SSM_WIDTH:
                xs_ref[0, rs, cs] = act
            else:
                bc_ref[0, rs, :] = act.astype(BF16)

    convst_ref[0] = xbc_s[tm:tm + carry_rows, :]
    xbc_s[0:carry_rows, :] = xbc_s[tm:tm + carry_rows, :]


def _inproj(x, gmix, win, m128, qn, kn, cos, sin, convw, convb, dtb, conv0, *, tm, carry_rows, shift, n_keep):
    nb, seq, _ = x.shape
    n_l = seq // tm
    n_parts = 1
    has_state = conv0 is not None
    const = lambda shape: pl.BlockSpec(shape, lambda b, l: (0,) * len(shape))
    row = lambda w: pl.BlockSpec((1, tm, w), lambda b, l: (b, l, 0))
    in_specs = [row(D_MODEL), const((1, D_MODEL)), const((D_MODEL, PROJ_PAD)), const((2 * LANES, LANES)),
                const((1, LANES)), const((1, LANES)),
                pl.BlockSpec((tm, LANES), lambda b, l: (l, 0)), pl.BlockSpec((tm, LANES), lambda b, l: (l, 0)),
                const((CONV_W, CONV_DIM)), const((1, CONV_DIM)), const((1, LANES))]
    args = [x, gmix, win, m128, qn, kn, cos, sin, convw, convb, dtb]
    if has_state:
        in_specs.append(const((carry_rows, CONV_DIM)))
        args.append(conv0)
    last = lambda rows, w: pl.BlockSpec((1, rows, w), lambda b, l: (b, 0, 0))
    out_specs = [row(ATTN_WIDTH), row(2 * KV_WIDTH), row(2 * KV_WIDTH), row(SSM_WIDTH), row(SSM_WIDTH),
                 row(CONV_DIM - SSM_WIDTH), row(LANES),
                 last(n_keep, KV_WIDTH), last(n_keep, KV_WIDTH), last(carry_rows, CONV_DIM)]
    out_shape = [jax.ShapeDtypeStruct((nb, seq, ATTN_WIDTH), BF16),
                 jax.ShapeDtypeStruct((nb, seq, 2 * KV_WIDTH), BF16),
                 jax.ShapeDtypeStruct((nb, seq, 2 * KV_WIDTH), BF16),
                 jax.ShapeDtypeStruct((nb, seq, SSM_WIDTH), F32),
                 jax.ShapeDtypeStruct((nb, seq, SSM_WIDTH), F32),
                 jax.ShapeDtypeStruct((nb, seq, CONV_DIM - SSM_WIDTH), BF16),
                 jax.ShapeDtypeStruct((nb, seq, LANES), F32),
                 jax.ShapeDtypeStruct((nb, n_keep, KV_WIDTH), F32),
                 jax.ShapeDtypeStruct((nb, n_keep, KV_WIDTH), F32),
                 jax.ShapeDtypeStruct((nb, carry_rows, CONV_DIM), F32)]
    kern = functools.partial(_inproj_kernel, tm=tm, carry_rows=carry_rows, shift=shift, n_keep=n_keep,
                             has_state=has_state, n_parts=n_parts)
    return pl.pallas_call(
        kern, grid=(nb, n_l), in_specs=in_specs, out_specs=out_specs, out_shape=out_shape,
        scratch_shapes=[pltpu.VMEM((carry_rows + tm, CONV_DIM), F32)],
        compiler_params=pltpu.CompilerParams(dimension_semantics=("arbitrary", "arbitrary"),
                                             vmem_limit_bytes=VMEM_LIMIT),
        name="inproj",
    )(*args)


def _attn_kernel(sinks_ref, q_ref, kc_ref, kp_ref, vc_ref, vp_ref, o_ref, *, tq):
    i = pl.program_id(1)
    nblk = tq // WINDOW
    lane = lax.broadcasted_iota(jnp.int32, (WINDOW, LANES), 1)
    low = lane < HEAD_DIM
    qi = lax.broadcasted_iota(jnp.int32, (WINDOW, 2 * WINDOW), 0)
    kj = lax.broadcasted_iota(jnp.int32, (WINDOW, 2 * WINDOW), 1)
    diff = qi + WINDOW - kj
    band = (diff >= 0) & (diff < WINDOW)
    zero_bf = jnp.zeros((WINDOW, LANES), BF16)
    for blk in range(nblk):
        r0 = blk * WINDOW
        if blk == 0:
            k_prev, v_prev = kp_ref[0], vp_ref[0]
            mask = band & ((i > 0) | (kj >= WINDOW))
        else:
            k_prev = kc_ref[0, r0 - WINDOW:r0, :]
            v_prev = vc_ref[0, r0 - WINDOW:r0, :]
            mask = band
        kk = jnp.concatenate([k_prev, kc_ref[0, r0:r0 + WINDOW, :]], axis=0)
        vv = jnp.concatenate([v_prev, vc_ref[0, r0:r0 + WINDOW, :]], axis=0)
        for pair in range(N_Q_HEADS // 2):
            g = (2 * pair) // (N_Q_HEADS // N_KV_HEADS)
            qp = q_ref[0, r0:r0 + WINDOW, pair * LANES:(pair + 1) * LANES]
            outs = []
            for e in range(2):
                h = 2 * pair + e
                var = 0 if g == e else 1
                qh = jnp.where(low if e == 0 else ~low, qp, zero_bf)
                s = _dot_nt(qh, kk[:, var * LANES:(var + 1) * LANES])
                s = jnp.where(mask, s, -jnp.inf)
                sk = sinks_ref[h]
                m = jnp.maximum(jnp.max(s, axis=-1, keepdims=True), sk)
                p = jnp.exp(s - m)
                den = jnp.sum(p, axis=-1, keepdims=True) + jnp.exp(sk - m)
                o = _dot(p.astype(BF16), vv[:, var * LANES:(var + 1) * LANES])
                outs.append(o / den)
            o_ref[0, r0:r0 + WINDOW, pair * LANES:(pair + 1) * LANES] = (
                jnp.where(low, outs[0], outs[1]).astype(BF16))


def _attention(sinks, q, k2, v2, *, tq):
    nb, seq, _ = q.shape
    r = tq // WINDOW
    cur = lambda w: pl.BlockSpec((1, tq, w), lambda b, i: (b, i, 0))
    prev = lambda w: pl.BlockSpec((1, WINDOW, w), lambda b, i: (b, jnp.maximum(i * r - 1, 0), 0))
    return pl.pallas_call(
        functools.partial(_attn_kernel, tq=tq),
        grid=(nb, seq // tq),
        in_specs=[pl.BlockSpec(memory_space=pltpu.SMEM),
                  cur(ATTN_WIDTH), cur(2 * KV_WIDTH), prev(2 * KV_WIDTH), cur(2 * KV_WIDTH), prev(2 * KV_WIDTH)],
        out_specs=cur(ATTN_WIDTH),
        out_shape=jax.ShapeDtypeStruct((nb, seq, ATTN_WIDTH), BF16),
        compiler_params=pltpu.CompilerParams(dimension_semantics=("arbitrary", "arbitrary"),
                                             vmem_limit_bytes=VMEM_LIMIT),
        name="swa_attn",
    )(sinks, q, k2, k2, v2, v2)


def _ssd_kernel(xs_ref, bc_ref, dt_ref, a_ref, dskip_ref, tri_ref,
                y_ref, hT_ref, h_s, *, tc):
    l = pl.program_id(1)
    n_l = pl.num_programs(1)
    gw = SSM_WIDTH // N_SSM_GROUPS
    hpg = N_SSM_HEADS // N_SSM_GROUPS

    @pl.when(l == 0)
    def _():
        h_s[...] = jnp.zeros(h_s.shape, F32)

    lane = lax.broadcasted_iota(jnp.int32, (SSD_CHUNK, LANES), 1)
    low = lane < SSM_HEAD_DIM
    qi = lax.broadcasted_iota(jnp.int32, (SSD_CHUNK, SSD_CHUNK), 0)
    sj = lax.broadcasted_iota(jnp.int32, (SSD_CHUNK, SSD_CHUNK), 1)
    causal = sj <= qi
    a_neg = -jnp.exp(a_ref[...])
    tri = tri_ref[...]
    zero_bf = jnp.zeros((SSD_CHUNK, LANES), BF16)

    for c in range(tc // SSD_CHUNK):
        rows = slice(c * SSD_CHUNK, (c + 1) * SSD_CHUNK)
        dt = dt_ref[0, rows, :]
        hi, mid, lo = _split3(dt * a_neg)
        acum = (_dot(tri, hi) + _dot(tri, mid) + _dot(tri, lo)) * LOG2E
        acum_t = acum.T
        dt_t = dt.T
        a_end_t = acum_t[:, SSD_CHUNK - 1:SSD_CHUNK]
        w_end_t = dt_t * jnp.exp2(a_end_t - acum_t)
        dec_t = jnp.exp2(a_end_t)
        dec_t = jnp.broadcast_to(dec_t, (LANES, LANES))
        row_t = acum_t - jnp.log2(dt_t)
        for g in range(N_SSM_GROUPS):
            b_g = bc_ref[0, rows, g * D_STATE:(g + 1) * D_STATE]
            c_g = bc_ref[0, rows, (N_SSM_GROUPS + g) * D_STATE:(N_SSM_GROUPS + g + 1) * D_STATE]
            cb = _dot_nt(c_g, b_g)
            b_gt = b_g.astype(F32).T
            y_inter = _dot(c_g, h_s[g].astype(BF16))
            for pr in range(hpg // 2):
                lhs_y, lhs_s, escale, dsc = [], [], [], []
                for e in range(2):
                    h = g * hpg + 2 * pr + e
                    col = jnp.broadcast_to(acum[:, h:h + 1], (SSD_CHUNK, SSD_CHUNK))
                    decay_dt = jnp.where(causal, jnp.exp2(col - row_t[h:h + 1, :]), 0.0)
                    lhs_y.append((cb * decay_dt).astype(BF16))
                    lhs_s.append((b_gt * w_end_t[h:h + 1, :]).astype(BF16))
                    escale.append(jnp.exp2(col))
                    dsc.append(dec_t[h:h + 1, :])
                c0 = g * gw + pr * LANES
                x_pair = xs_ref[0, rows, c0:c0 + LANES]
                x_bf = x_pair.astype(BF16)
                rhs = jnp.concatenate([jnp.where(low, x_bf, zero_bf), jnp.where(low, zero_bf, x_bf)], axis=0)
                y_in = _dot(jnp.concatenate(lhs_y, axis=1), rhs)
                st = _dot(jnp.concatenate(lhs_s, axis=1), rhs)
                lc = pr * LANES
                y = y_in + y_inter[:, lc:lc + LANES] * jnp.where(low, escale[0], escale[1])
                y_ref[0, rows, c0:c0 + LANES] = y + dskip_ref[:, c0:c0 + LANES] * x_pair
                h_s[g, :, lc:lc + LANES] = h_s[g, :, lc:lc + LANES] * jnp.where(low, dsc[0], dsc[1]) + st

    @pl.when(l == n_l - 1)
    def _():
        hT_ref[0] = h_s[...]


def _ssd(xs, bc, dt, a_pad, dskip, tri, *, tc):
    nb, seq, _ = xs.shape
    row = lambda w: pl.BlockSpec((1, tc, w), lambda b, l: (b, l, 0))
    const = lambda shape: pl.BlockSpec(shape, lambda b, l: (0,) * len(shape))
    gw = SSM_WIDTH // N_SSM_GROUPS
    return pl.pallas_call(
        functools.partial(_ssd_kernel, tc=tc),
        grid=(nb, seq // tc),
        in_specs=[row(SSM_WIDTH), row(CONV_DIM - SSM_WIDTH), row(LANES),
                  const((1, LANES)), const((1, SSM_WIDTH)), const((SSD_CHUNK, SSD_CHUNK))],
        out_specs=[row(SSM_WIDTH),
                   pl.BlockSpec((1, N_SSM_GROUPS, D_STATE, gw), lambda b, l: (b, 0, 0, 0))],
        out_shape=[jax.ShapeDtypeStruct((nb, seq, SSM_WIDTH), F32),
                   jax.ShapeDtypeStruct((nb, N_SSM_GROUPS, D_STATE, gw), F32)],
        scratch_shapes=[pltpu.VMEM((N_SSM_GROUPS, D_STATE, gw), F32)],
        compiler_params=pltpu.CompilerParams(dimension_semantics=("arbitrary", "arbitrary"),
                                             vmem_limit_bytes=VMEM_LIMIT),
        name="ssd_scan",
    )(xs, bc, dt, a_pad, dskip, tri)


def _outffn_kernel(*refs, ffc, gated):
    if gated:
        x_ref, oa_ref, ys_ref, z_ref, gnorm_ref, wo_ref, gffn_ref, wgu_ref, wd_ref, o_ref = refs
        gw = SSM_WIDTH // N_SSM_GROUPS
        parts = []
        for g in range(N_SSM_GROUPS):
            ls = slice(g * gw, (g + 1) * gw)
            yg = ys_ref[:, ls] * _silu(z_ref[:, ls])
            yg = yg * lax.rsqrt(jnp.mean(yg * yg, axis=-1, keepdims=True) + EPS)
            parts.append((yg * gnorm_ref[:, ls]).astype(BF16))
        ys = jnp.concatenate(parts, axis=1)
    else:
        x_ref, oa_ref, ys_ref, wo_ref, gffn_ref, wgu_ref, wd_ref, o_ref = refs
        ys = ys_ref[...]
    x = x_ref[...]
    xm = x + _dot(oa_ref[...], wo_ref[0:ATTN_WIDTH, :]) + _dot(ys, wo_ref[ATTN_WIDTH:, :])
    hn = _rms_rows(xm, gffn_ref[...]).astype(BF16)
    acc = jnp.zeros_like(xm)
    c0 = 0
    for width in ffc:
        g = _dot(hn, wgu_ref[:, c0:c0 + width])
        up = _dot(hn, wgu_ref[:, D_FF + c0:D_FF + c0 + width])
        acc = acc + _dot((_silu(g) * up).astype(BF16), wd_ref[c0:c0 + width, :])
        c0 += width
    o_ref[...] = xm + acc


def _outffn(x, oa, ys, gate, wo, gffn, wgu, wd, *, tm, ffc):
    rows = x.shape[0]
    row = lambda w: pl.BlockSpec((tm, w), lambda i: (i, 0))
    const = lambda shape: pl.BlockSpec(shape, lambda i: (0,) * len(shape), pipeline_mode=pl.Buffered(1))
    gated = gate is not None
    in_specs = [row(D_MODEL), row(ATTN_WIDTH), row(SSM_WIDTH)]
    args = [x, oa, ys]
    if gated:
        in_specs += [row(SSM_WIDTH), const((1, SSM_WIDTH))]
        args += list(gate)
    in_specs += [const((ATTN_WIDTH + SSM_WIDTH, D_MODEL)), const((1, D_MODEL)),
                 const((D_MODEL, 2 * D_FF)), const((D_FF, D_MODEL))]
    args += [wo, gffn, wgu, wd]
    return pl.pallas_call(
        functools.partial(_outffn_kernel, ffc=ffc, gated=gated),
        grid=(rows // tm,),
        in_specs=in_specs,
        out_specs=row(D_MODEL),
        out_shape=jax.ShapeDtypeStruct((rows, D_MODEL), F32),
        compiler_params=pltpu.CompilerParams(dimension_semantics=("arbitrary",),
                                             vmem_limit_bytes=VMEM_LIMIT),
        name="outproj_ffn",
    )(*args)


T_PAD = SUBLANES


def _attn_dec_kernel(q_ref, kc_ref, kn_ref, vc_ref, vn_ref, sink_ref, o_ref, wk_ref, wv_ref, *, t_new):
    q = q_ref[...]
    bb, nq, _ = q.shape
    qpk = N_Q_HEADS // N_KV_HEADS
    n_c = kc_ref.shape[1]
    kc, vc, kn, vn = kc_ref[...], vc_ref[...], kn_ref[...], vn_ref[...]
    s_c = jnp.einsum('bqd,bkd->bqk', q, kc.astype(BF16), preferred_element_type=F32)
    s_n = jnp.einsum('bqd,bkd->bqk', q, kn.astype(BF16), preferred_element_type=F32)
    t_c = (lax.broadcasted_iota(jnp.int32, (bb, nq, n_c), 1) % (t_new * qpk)) // qpk
    j_c = lax.broadcasted_iota(jnp.int32, (bb, nq, n_c), 2)
    s_c = jnp.where(j_c > t_c + (n_c - WINDOW), s_c, -jnp.inf)
    t_n = (lax.broadcasted_iota(jnp.int32, (bb, nq, T_PAD), 1) % (t_new * qpk)) // qpk
    j_n = lax.broadcasted_iota(jnp.int32, (bb, nq, T_PAD), 2)
    s_n = jnp.where((j_n <= t_n) & (j_n < t_new), s_n, -jnp.inf)
    sk = sink_ref[...]
    m = jnp.maximum(jnp.maximum(jnp.max(s_c, axis=-1, keepdims=True), jnp.max(s_n, axis=-1, keepdims=True)), sk)
    p_c = jnp.exp(s_c - m)
    p_n = jnp.exp(s_n - m)
    den = jnp.sum(p_c, axis=-1, keepdims=True) + jnp.sum(p_n, axis=-1, keepdims=True) + jnp.exp(sk - m)
    o = (jnp.einsum('bqk,bkd->bqd', p_c.astype(BF16), vc.astype(BF16), preferred_element_type=F32)
         + jnp.einsum('bqk,bkd->bqd', p_n.astype(BF16), vn.astype(BF16), preferred_element_type=F32))
    o = o / den
    rows_g = nq // N_KV_HEADS
    lane = lax.broadcasted_iota(jnp.int32, (bb, rows_g, LANES), 2)
    merged = o[:, 0:rows_g, :]
    for g in range(1, N_KV_HEADS):
        merged = jnp.where(lane // HEAD_DIM == g, o[:, g * rows_g:(g + 1) * rows_g, :], merged)
    o_ref[...] = merged.astype(BF16)

    sub = lax.broadcasted_iota(jnp.int32, (bb, T_PAD, LANES), 1)
    for c, n, w_ref in ((kc, kn, wk_ref), (vc, vn, wv_ref)):
        shifted = pltpu.roll(c.reshape(bb * n_c, LANES), bb * n_c - t_new, 0).reshape(bb, n_c, LANES)
        tail = pltpu.roll(n.reshape(bb * T_PAD, LANES), T_PAD - t_new, 0).reshape(bb, T_PAD, LANES)
        w_ref[:, 0:n_c - T_PAD, :] = shifted[:, 0:n_c - T_PAD, :]
        w_ref[:, n_c - T_PAD:, :] = jnp.where(sub < T_PAD - t_new, shifted[:, n_c - T_PAD:, :], tail)


def _attention_dec(q, kc, kn, vc, vn, sink_rows, *, t_new, bb):
    n_seq, nq, w = q.shape
    n_c = kc.shape[1]
    blk = lambda r: pl.BlockSpec((bb, r, w), lambda i: (i, 0, 0))
    win = jax.ShapeDtypeStruct((n_seq, n_c, w), F32)
    return pl.pallas_call(
        functools.partial(_attn_dec_kernel, t_new=t_new),
        grid=(n_seq // bb,),
        in_specs=[blk(nq), blk(n_c), blk(T_PAD), blk(n_c), blk(T_PAD), pl.BlockSpec((nq, 1), lambda i: (0, 0))],
        out_specs=[blk(nq // N_KV_HEADS), blk(n_c), blk(n_c)],
        out_shape=[jax.ShapeDtypeStruct((n_seq, nq // N_KV_HEADS, w), BF16), win, win],
        compiler_params=pltpu.CompilerParams(dimension_semantics=("arbitrary",), vmem_limit_bytes=VMEM_LIMIT),
        name="swa_attn_decode",
    )(q, kc, kn, vc, vn, sink_rows)


def _ssd_dec_kernel(*refs, bb, t_new, n_prev):
    (xs_ref, bc_ref, dt_ref, z_ref, h0_ref) = refs[:5]
    prev_refs = refs[5:5 + n_prev]
    (a_ref, dskip_ref, gnorm_ref, tri_ref, ones_ref, exp_ref,
     y_ref, h_ref, aw_s, xdt_s, xw_s, ea_s, dec_s) = refs[5 + n_prev:]
    d = pl.program_id(0)
    gw = SSM_WIDTH // N_SSM_GROUPS
    hpg = N_SSM_HEADS // N_SSM_GROUPS
    rows = bb * T_PAD

    for j in range(n_prev):
        @pl.when(d == j)
        def _(j=j):
            h_ref[0] = prev_refs[j][0]

    @pl.when(d == n_prev)
    def _():
        expand = exp_ref[...]

        def widen(v):
            hi, mid, lo = _split3(v)
            return _dot(hi, expand) + _dot(mid, expand) + _dot(lo, expand)

        def seq_sum(m, v):
            hi, mid, lo = _split3(v)
            return _dot(m, hi) + _dot(m, mid) + _dot(m, lo)

        dt = dt_ref[...].reshape(rows, LANES)
        dta = dt * (-jnp.exp(a_ref[...]))
        a_w = widen(seq_sum(tri_ref[...], dta))
        a_end = seq_sum(ones_ref[...], dta)
        xdt = xs_ref[...].reshape(rows, SSM_WIDTH) * widen(dt)
        aw_s[...] = a_w
        xdt_s[...] = xdt
        xw_s[...] = xdt * jnp.exp(widen(a_end) - a_w)
        ea_s[...] = jnp.exp(a_w)
        dec_s[...] = jnp.exp(a_end)

        t_row = lax.broadcasted_iota(jnp.int32, (T_PAD, gw), 0)

        def seq_body(i, carry):
            r0 = pl.multiple_of(i * T_PAD, T_PAD)
            rs = pl.ds(r0, T_PAD)
            dec = jnp.broadcast_to(dec_s[pl.ds(r0, 1), :], (LANES, LANES)).T
            bc = bc_ref[i]
            for g in range(N_SSM_GROUPS):
                b_g = bc[:, g * D_STATE:(g + 1) * D_STATE]
                c_g = bc[:, (N_SSM_GROUPS + g) * D_STATE:(N_SSM_GROUPS + g + 1) * D_STATE]
                cb = _dot_nt(c_g, b_g)
                ls = slice(g * gw, (g + 1) * gw)
                a_g = aw_s[rs, ls]
                xdt_g = xdt_s[rs, ls]
                y = jnp.zeros((T_PAD, gw), F32)
                for s in range(t_new):
                    w = jnp.where(t_row >= s, jnp.exp(a_g - a_g[s:s + 1, :]), 0.0)
                    y = y + (w * cb[:, s:s + 1]) * xdt_g[s:s + 1, :]
                h0g = h0_ref[0, i, g * hpg:(g + 1) * hpg].reshape(gw, D_STATE)
                y = y + _dot_nt(c_g, h0g.astype(BF16)) * ea_s[rs, ls]
                y = y + dskip_ref[:, ls] * xs_ref[i, :, ls]
                y = y * _silu(z_ref[i, :, ls])
                y = y * lax.rsqrt(jnp.mean(y * y, axis=-1, keepdims=True) + EPS)
                y_ref[i, :, ls] = (y * gnorm_ref[:, ls]).astype(BF16)
                delta = lax.dot_general(xw_s[rs, ls].astype(BF16), b_g, (((0,), (0,)), ((), ())),
                                        preferred_element_type=F32)
                for hl in range(hpg):
                    h = g * hpg + hl
                    h_ref[0, i, h] = (h0_ref[0, i, h] * dec[h:h + 1, :]
                                      + delta[hl * SSM_HEAD_DIM:(hl + 1) * SSM_HEAD_DIM, :])
            return carry

        lax.fori_loop(0, bb, seq_body, 0)


def _ssd_dec(xs3, bc3, dt3, z3, state_all, layer, prev_states, a_pad, dskip, gnorm, tri_b, ones_b, expand,
             *, bb, t_new):
    nb = xs3.shape[0]
    nblk = nb // bb
    n_prev = len(prev_states)
    rows = bb * T_PAD

    def phase_block(j):
        return lambda d, i: jnp.clip(i + (d - j) * nblk, 0, nblk - 1)

    cur = phase_block(n_prev)
    blk = lambda w: pl.BlockSpec((bb, T_PAD, w), lambda d, i: (cur(d, i), 0, 0))
    const = lambda shape: pl.BlockSpec(shape, lambda d, i: (0,) * len(shape))
    st_shape = (1, bb, N_SSM_HEADS, SSM_HEAD_DIM, D_STATE)
    in_specs = [blk(SSM_WIDTH), blk(CONV_DIM - SSM_WIDTH), blk(LANES), blk(SSM_WIDTH),
                pl.BlockSpec(st_shape, lambda d, i: (layer, cur(d, i), 0, 0, 0))]
    for j in range(n_prev):
        in_specs.append(pl.BlockSpec(st_shape, lambda d, i, j=j: (0, phase_block(j)(d, i), 0, 0, 0)))
    in_specs += [const((1, LANES)), const((1, SSM_WIDTH)), const((1, SSM_WIDTH)),
                 const((rows, rows)), const((rows, rows)), const((LANES, SSM_WIDTH))]
    return pl.pallas_call(
        functools.partial(_ssd_dec_kernel, bb=bb, t_new=t_new, n_prev=n_prev),
        grid=(n_prev + 1, nblk),
        in_specs=in_specs,
        out_specs=[blk(SSM_WIDTH), pl.BlockSpec(st_shape, lambda d, i: (d, i, 0, 0, 0))],
        out_shape=[jax.ShapeDtypeStruct((nb, T_PAD, SSM_WIDTH), BF16),
                   jax.ShapeDtypeStruct((n_prev + 1,) + state_all.shape[1:], F32)],
        scratch_shapes=[pltpu.VMEM((rows, SSM_WIDTH), F32), pltpu.VMEM((rows, SSM_WIDTH), F32),
                        pltpu.VMEM((rows, SSM_WIDTH), F32), pltpu.VMEM((rows, SSM_WIDTH), F32),
                        pltpu.VMEM((rows, LANES), F32)],
        compiler_params=pltpu.CompilerParams(dimension_semantics=("arbitrary", "arbitrary"),
                                             vmem_limit_bytes=VMEM_LIMIT),
        name="ssd_decode",
    )(xs3, bc3, dt3, z3, state_all, *prev_states, a_pad, dskip, gnorm, tri_b, ones_b, expand)


def _rope_tables(pos):
    half = HEAD_DIM // 2
    inv = ROPE_THETA ** (-np.arange(half, dtype=np.float64) / half)
    ang = np.asarray(pos, np.float64)[:, None] * inv[None, :]
    cos = np.tile(np.cos(ang), (1, LANES // half))
    sin = np.sin(ang)
    sin_signed = np.tile(np.concatenate([-sin, sin], axis=1), (1, LANES // HEAD_DIM))
    return jnp.asarray(cos, F32), jnp.asarray(sin_signed, F32)


def _layer_consts(l, p):
    w = p['w_in'][l]
    o_z = ATTN_WIDTH + 2 * KV_WIDTH
    o_xbc = o_z + SSM_WIDTH
    o_dt = o_xbc + CONV_DIM
    w_in = jnp.concatenate(
        [w[:, :o_z], jnp.pad(w[:, o_dt:], ((0, 0), (0, LANES - N_SSM_HEADS))), w[:, o_z:o_dt]], axis=1).astype(BF16)
    return dict(
        gmix=p['norm_mix'][l][None, :],
        win=w_in,
        qn=jnp.tile(p['q_norm'][l], LANES // HEAD_DIM)[None, :],
        kn=jnp.tile(p['k_norm'][l], LANES // HEAD_DIM)[None, :],
        convw=p['conv_w'][l],
        convb=p['conv_b'][l][None, :],
        dtb=jnp.pad(p['dt_bias'][l], (0, LANES - N_SSM_HEADS))[None, :],
        a_pad=jnp.pad(p['a_log'][l], (0, LANES - N_SSM_HEADS))[None, :],
        dskip=jnp.repeat(p['d_skip'][l], SSM_HEAD_DIM)[None, :],
        gnorm=p['ssm_norm'][l][None, :],
        sinks=p['sinks'][l],
        wo=p['w_out'][l].astype(BF16),
        gffn=p['norm_ffn'][l][None, :],
        wgu=p['w_gate_up'][l].astype(BF16),
        wd=p['w_down'][l].astype(BF16),
    )


def _state_from_t(h_t):
    nb = h_t.shape[0]
    hpg = N_SSM_HEADS // N_SSM_GROUPS
    h = h_t.reshape(nb, N_SSM_GROUPS, D_STATE, hpg, SSM_HEAD_DIM)
    return jnp.transpose(h, (0, 1, 3, 4, 2)).reshape(nb, N_SSM_HEADS, SSM_HEAD_DIM, D_STATE)


def _prompt_layer(x, c, shared, *, tm):
    nb, seq, _ = x.shape
    (q, k2, v2, z, xs, bc, dt, knew, vnew, convst) = _inproj(
        x, c['gmix'], c['win'], shared['m128'], c['qn'], c['kn'], shared['cos_p'], shared['sin_p'],
        c['convw'], c['convb'], c['dtb'], None, tm=tm, carry_rows=SUBLANES, shift=1, n_keep=min(WINDOW, seq))
    oa = _attention(c['sinks'], q, k2, v2, tq=tm)
    ys, h_t = _ssd(xs, bc, dt, c['a_pad'], c['dskip'], shared['tri'], tc=tm)
    rows = nb * seq
    xo = _outffn(x.reshape(rows, D_MODEL), oa.reshape(rows, ATTN_WIDTH), ys.reshape(rows, SSM_WIDTH),
                 (z.reshape(rows, SSM_WIDTH), c['gnorm']), c['wo'], c['gffn'], c['wgu'], c['wd'], tm=tm,
                 ffc=FF_CHUNKS)
    n_keep = knew.shape[1]
    return (xo.reshape(nb, seq, D_MODEL),
            knew.reshape(nb, n_keep, N_KV_HEADS, HEAD_DIM), vnew.reshape(nb, n_keep, N_KV_HEADS, HEAD_DIM),
            convst[:, SUBLANES - (CONV_W - 1):, :], _state_from_t(h_t))


def _sample_layer(x_t, cache_k, cache_v, conv0, state_all, layer, prev_states, c, shared, *, n_seq, t_new):
    rows = t_new * n_seq
    qpk = N_Q_HEADS // N_KV_HEADS
    n_c = cache_k.shape[1]
    conv0_t = jnp.transpose(conv0, (1, 0, 2)).reshape((CONV_W - 1) * n_seq, CONV_DIM)
    (q, _, _, z, xs, bc, dt, knew, vnew, convst) = _inproj(
        x_t, c['gmix'], c['win'], shared['m128'], c['qn'], c['kn'], shared['cos_s'], shared['sin_s'],
        c['convw'], c['convb'], c['dtb'], conv0_t, tm=rows, carry_rows=(CONV_W - 1) * n_seq, shift=n_seq,
        n_keep=rows)

    def to_seq(a):
        a = jnp.transpose(a[0].reshape(t_new, n_seq, a.shape[-1]), (1, 0, 2))
        return jnp.pad(a, ((0, 0), (0, T_PAD - t_new), (0, 0)))

    q5 = jnp.transpose(q[0].reshape(t_new, n_seq, N_KV_HEADS, qpk, HEAD_DIM), (1, 2, 0, 3, 4))
    q5 = q5.reshape(n_seq, N_KV_HEADS, t_new * qpk, HEAD_DIM)
    q_m = jnp.concatenate(
        [jnp.pad(q5[:, g], ((0, 0), (0, 0), (g * HEAD_DIM, (N_KV_HEADS - 1 - g) * HEAD_DIM)))
         for g in range(N_KV_HEADS)], axis=1)
    sink_rows = jnp.tile(c['sinks'].reshape(N_KV_HEADS, 1, qpk), (1, t_new, 1)).reshape(-1, 1)
    o_m, win_k, win_v = _attention_dec(
        q_m, cache_k.reshape(n_seq, n_c, KV_WIDTH), to_seq(knew), cache_v.reshape(n_seq, n_c, KV_WIDTH),
        to_seq(vnew), sink_rows, t_new=t_new, bb=min(32, n_seq))
    o5 = o_m.reshape(n_seq, t_new, qpk, N_KV_HEADS, HEAD_DIM)
    oa = jnp.transpose(o5, (1, 0, 3, 2, 4)).reshape(rows, ATTN_WIDTH)
    win_k = win_k.reshape(cache_k.shape)
    win_v = win_v.reshape(cache_v.shape)

    y3, h_new = _ssd_dec(to_seq(xs), to_seq(bc), to_seq(dt), to_seq(z), state_all, layer, prev_states,
                         c['a_pad'], c['dskip'], c['gnorm'], shared['tri_b'], shared['ones_b'], shared['expand'],
                         bb=shared['bb_s'], t_new=t_new)
    ys = jnp.transpose(y3[:, :t_new], (1, 0, 2)).reshape(rows, SSM_WIDTH)

    xo = _outffn(x_t[0], oa, ys, None, c['wo'], c['gffn'], c['wgu'], c['wd'], tm=min(512, rows), ffc=FF_CHUNKS)

    conv_new = jnp.transpose(convst[0].reshape(CONV_W - 1, n_seq, CONV_DIM), (1, 0, 2))
    return xo[None], win_k, win_v, conv_new, h_new


def kernel(x_prompt, x_sample, cache_win_k, cache_win_v, state_conv, state_ssm,
           norm_mix, w_in, q_norm, k_norm, sinks, conv_w, conv_b, dt_bias, a_log,
           d_skip, ssm_norm, w_out, norm_ffn, w_gate_up, w_down):
    p = dict(norm_mix=norm_mix, w_in=w_in, q_norm=q_norm, k_norm=k_norm, sinks=sinks, conv_w=conv_w,
             conv_b=conv_b, dt_bias=dt_bias, a_log=a_log, d_skip=d_skip, ssm_norm=ssm_norm, w_out=w_out,
             norm_ffn=norm_ffn, w_gate_up=w_gate_up, w_down=w_down)
    depth = w_in.shape[0]
    seq = x_prompt.shape[1]
    cos_p, sin_p = _rope_tables(np.arange(seq))
    half_blk = jnp.arange(LANES) // HEAD_DIM
    m128 = (jnp.where(half_blk[:, None] == half_blk[None, :], 1.0 / HEAD_DIM, 0.0)).astype(BF16)
    m128 = jnp.concatenate([m128, m128], axis=0)
    tri = (jnp.arange(SSD_CHUNK)[:, None] >= jnp.arange(SSD_CHUNK)[None, :]).astype(BF16)
    n_seq, t_new, _ = x_sample.shape
    pos_s = PAST_LEN + np.repeat(np.arange(t_new), n_seq)
    cos_s, sin_s = _rope_tables(pos_s)
    bb_s = min(8, n_seq)
    r_idx = jnp.arange(bb_s * T_PAD)
    same_seq = (r_idx[:, None] // T_PAD) == (r_idx[None, :] // T_PAD)
    tri_b = (same_seq & (r_idx[:, None] >= r_idx[None, :])).astype(BF16)
    ones_b = same_seq.astype(BF16)
    expand = (jnp.arange(LANES)[:, None] == (jnp.arange(SSM_WIDTH) // SSM_HEAD_DIM)[None, :]).astype(BF16)
    shared = dict(cos_p=cos_p, sin_p=sin_p, cos_s=cos_s, sin_s=sin_s, m128=m128, tri=tri, tri_b=tri_b,
                  ones_b=ones_b, bb_s=bb_s, expand=expand)
    tm = min(512, seq)

    xp = x_prompt
    xs = jnp.transpose(x_sample, (1, 0, 2)).reshape(1, t_new * n_seq, D_MODEL)
    pk, pv, pc, ph = [], [], [], []
    sk, sv, sc, sh = [], [], [], []
    for l in range(depth):
        c = _layer_consts(l, p)
        xp, k1, v1, c1, h1 = _prompt_layer(xp, c, shared, tm=tm)
        prev = sh if l == depth - 1 else []
        xs, k2, v2, c2, h2 = _sample_layer(xs, cache_win_k[l], cache_win_v[l], state_conv[l], state_ssm, l, prev,
                                           c, shared, n_seq=n_seq, t_new=t_new)
        pk.append(k1); pv.append(v1); pc.append(c1); ph.append(h1)
        sk.append(k2); sv.append(v2); sc.append(c2); sh.append(h2)
    ys = jnp.transpose(xs.reshape(t_new, n_seq, D_MODEL), (1, 0, 2))
    return (xp, ys, jnp.stack(pk), jnp.stack(pv), jnp.stack(pc), jnp.stack(ph),
            jnp.stack(sk), jnp.stack(sv), jnp.stack(sc), sh[-1])
```

```python
import functools
import math

import numpy as np
import jax
import jax.numpy as jnp
from jax import lax
from jax.experimental import pallas as pl
from jax.experimental.pallas import tpu as pltpu

F32 = jnp.float32
BF16 = jnp.bfloat16

D_MODEL = 1024
HEAD_DIM = 64
N_Q_HEADS = 8
N_KV_HEADS = 2
ATTN_WIDTH = N_Q_HEADS * HEAD_DIM
KV_WIDTH = N_KV_HEADS * HEAD_DIM
WINDOW = 128
ROPE_THETA = 10000.0
ATTN_SCALE = HEAD_DIM ** -0.5
SSM_HEAD_DIM = 64
N_SSM_HEADS = 16
SSM_WIDTH = N_SSM_HEADS * SSM_HEAD_DIM
N_SSM_GROUPS = 2
D_STATE = 128
CONV_W = 4
CONV_DIM = SSM_WIDTH + 2 * N_SSM_GROUPS * D_STATE
SSD_CHUNK = 128
D_FF = 2816
EPS = 1e-6
LOG2E = math.log2(math.e)
PAST_LEN = 16384

LANES = 128
SUBLANES = 8

COL_Q = 0
COL_K = COL_Q + ATTN_WIDTH
COL_V = COL_K + KV_WIDTH
COL_Z = COL_V + KV_WIDTH
COL_XBC = COL_Z + SSM_WIDTH
COL_DT = COL_XBC + CONV_DIM
IN_PROJ_WIDTH = COL_DT + N_SSM_HEADS
PROJ_PAD = COL_DT + LANES
QKV_COLS = COL_Z

VMEM_LIMIT = 56 * 1024 * 1024
FF_CHUNKS = (6 * 256, 5 * 256)


def _dot(a, b):
    return jnp.dot(a, b, preferred_element_type=F32)


def _dot_nt(a, b):
    return lax.dot_general(a, b, (((1,), (1,)), ((), ())), preferred_element_type=F32)


def _silu(x):
    return x / (1.0 + jnp.exp(-x))


def _softplus(x):
    return jnp.maximum(x, 0.0) + jnp.log1p(jnp.exp(-jnp.abs(x)))


def _rms_rows(x, g):
    return x * lax.rsqrt(jnp.mean(x * x, axis=-1, keepdims=True) + EPS) * g


def _split2(x):
    hi = x.astype(BF16)
    lo = (x - hi.astype(F32)).astype(BF16)
    return hi, lo


def _split3(x):
    hi = x.astype(BF16)
    r1 = x - hi.astype(F32)
    mid = r1.astype(BF16)
    lo = (r1 - mid.astype(F32)).astype(BF16)
    return hi, mid, lo


def _head_norm_rope(x, m128, g, cos, sin_signed, low_half):
    hi, lo = _split2(x * x)
    ms = _dot(jnp.concatenate([hi, lo], axis=1), m128)
    xn = x * lax.rsqrt(ms + EPS) * g
    rot = jnp.where(low_half, pltpu.roll(xn, LANES - HEAD_DIM // 2, 1),
                    pltpu.roll(xn, HEAD_DIM // 2, 1))
    return xn * cos + rot * sin_signed


def _inproj_kernel(*refs, tm, carry_rows, shift, n_keep, has_state, n_parts):
    if has_state:
        (x_ref, gmix_ref, win_ref, m128_ref, qn_ref, kn_ref, cos_ref, sin_ref,
         convw_ref, convb_ref, dtb_ref, conv0_ref,
         q_ref, k2_ref, v2_ref, z_ref, xs_ref, bc_ref, dt_ref, knew_ref, vnew_ref, convst_ref,
         xbc_s) = refs
    else:
        (x_ref, gmix_ref, win_ref, m128_ref, qn_ref, kn_ref, cos_ref, sin_ref,
         convw_ref, convb_ref, dtb_ref,
         q_ref, k2_ref, v2_ref, z_ref, xs_ref, bc_ref, dt_ref, knew_ref, vnew_ref, convst_ref,
         xbc_s) = refs
        conv0_ref = None
    l = pl.program_id(1)

    @pl.when(l == 0)
    def _():
        if has_state:
            xbc_s[0:carry_rows, :] = conv0_ref[...]
        else:
            xbc_s[0:carry_rows, :] = jnp.zeros((carry_rows, CONV_DIM), F32)

    m128 = m128_ref[...]
    cw = 512
    nr = tm // n_parts
    lane = lax.broadcasted_iota(jnp.int32, (nr, LANES), 1)
    low_half = (lane % HEAD_DIM) < (HEAD_DIM // 2)
    for part in range(n_parts):
        r0 = part * nr
        rs = slice(r0, r0 + nr)
        base = carry_rows + r0
        u = _rms_rows(x_ref[0, rs, :], gmix_ref[...]).astype(BF16)
        for j in range(CONV_DIM // cw):
            c0 = COL_XBC + j * cw
            xbc_s[base:base + nr, j * cw:(j + 1) * cw] = _dot(u, win_ref[0, :, c0:c0 + cw])
        z_ref[0, rs, :] = _dot(u, win_ref[0, :, COL_Z:COL_Z + SSM_WIDTH])
        heads = _dot(u, win_ref[0, :, 0:QKV_COLS])
        dt_raw = _dot(u, win_ref[0, :, COL_DT:COL_DT + LANES])
        cos = cos_ref[rs, :]
        sin = sin_ref[rs, :]

        for j in range(ATTN_WIDTH // LANES):
            qj = heads[:, COL_Q + j * LANES:COL_Q + (j + 1) * LANES]
            qj = _head_norm_rope(qj, m128, qn_ref[...], cos, sin, low_half)
            q_ref[0, rs, j * LANES:(j + 1) * LANES] = (qj * ATTN_SCALE).astype(BF16)

        k = _head_norm_rope(heads[:, COL_K:COL_K + KV_WIDTH], m128, kn_ref[...], cos, sin, low_half)
        v = heads[:, COL_V:COL_V + KV_WIDTH]
        k2_ref[0, rs, 0:LANES] = k.astype(BF16)
        k2_ref[0, rs, LANES:2 * LANES] = pltpu.roll(k, HEAD_DIM, 1).astype(BF16)
        v2_ref[0, rs, 0:LANES] = v.astype(BF16)
        v2_ref[0, rs, LANES:2 * LANES] = pltpu.roll(v, HEAD_DIM, 1).astype(BF16)
        if part == n_parts - 1:
            knew_ref[0] = k[nr - n_keep:, :]
            vnew_ref[0] = v[nr - n_keep:, :]

        dt_ref[0, rs, :] = _softplus(dt_raw + dtb_ref[...])

        for j in range(CONV_DIM // cw):
            cs = slice(j * cw, (j + 1) * cw)
            w = [convw_ref[i:i + 1, cs] for i in range(CONV_W)]
            bias = convb_ref[:, cs]
            if shift % SUBLANES == 0:
                acc = bias
                for i in range(CONV_W):
                    off = base - (CONV_W - 1 - i) * shift
                    acc = acc + xbc_s[off:off + nr, cs] * w[i]
                act = _silu(acc)
            else:
                cur = xbc_s[base:base + nr, cs]
                acc = cur * w[0]
                for i in range(1, CONV_W):
                    acc = pltpu.roll(acc, shift, 0) + cur * w[i]
                head = bias
                for i in range(CONV_W):
                    off = base - (CONV_W - 1 - i) * shift
                    head = head + xbc_s[off:off + SUBLANES, cs] * w[i]
                act = _silu(jnp.concatenate([head, acc[SUBLANES:, :] + bias], axis=0))
            if j * cw < SSM_WIDTH:
                xs_ref[0, rs, cs] = act
            else:
                bc_ref[0, rs, :] = act.astype(BF16)

    convst_ref[0] = xbc_s[tm:tm + carry_rows, :]
    xbc_s[0:carry_rows, :] = xbc_s[tm:tm + carry_rows, :]


def _inproj(x, gmix, win, m128, qn, kn, cos, sin, convw, convb, dtb, conv0, *, layer, tm, carry_rows, shift,
            n_keep):
    nb, seq, _ = x.shape
    n_l = seq // tm
    n_parts = 1
    has_state = conv0 is not None
    const = lambda shape: pl.BlockSpec(shape, lambda b, l: (0,) * len(shape))
    row = lambda w: pl.BlockSpec((1, tm, w), lambda b, l: (b, l, 0))
    in_specs = [row(D_MODEL), const((1, D_MODEL)),
                pl.BlockSpec((1, D_MODEL, PROJ_PAD), lambda b, l: (layer, 0, 0)), const((2 * LANES, LANES)),
                const((1, LANES)), const((1, LANES)),
                pl.BlockSpec((tm, LANES), lambda b, l: (l, 0)), pl.BlockSpec((tm, LANES), lambda b, l: (l, 0)),
                const((CONV_W, CONV_DIM)), const((1, CONV_DIM)), const((1, LANES))]
    args = [x, gmix, win, m128, qn, kn, cos, sin, convw, convb, dtb]
    if has_state:
        in_specs.append(const((carry_rows, CONV_DIM)))
        args.append(conv0)
    last = lambda rows, w: pl.BlockSpec((1, rows, w), lambda b, l: (b, 0, 0))
    out_specs = [row(ATTN_WIDTH), row(2 * KV_WIDTH), row(2 * KV_WIDTH), row(SSM_WIDTH), row(SSM_WIDTH),
                 row(CONV_DIM - SSM_WIDTH), row(LANES),
                 last(n_keep, KV_WIDTH), last(n_keep, KV_WIDTH), last(carry_rows, CONV_DIM)]
    out_shape = [jax.ShapeDtypeStruct((nb, seq, ATTN_WIDTH), BF16),
                 jax.ShapeDtypeStruct((nb, seq, 2 * KV_WIDTH), BF16),
                 jax.ShapeDtypeStruct((nb, seq, 2 * KV_WIDTH), BF16),
                 jax.ShapeDtypeStruct((nb, seq, SSM_WIDTH), F32),
                 jax.ShapeDtypeStruct((nb, seq, SSM_WIDTH), F32),
                 jax.ShapeDtypeStruct((nb, seq, CONV_DIM - SSM_WIDTH), BF16),
                 jax.ShapeDtypeStruct((nb, seq, LANES), F32),
                 jax.ShapeDtypeStruct((nb, n_keep, KV_WIDTH), F32),
                 jax.ShapeDtypeStruct((nb, n_keep, KV_WIDTH), F32),
                 jax.ShapeDtypeStruct((nb, carry_rows, CONV_DIM), F32)]
    kern = functools.partial(_inproj_kernel, tm=tm, carry_rows=carry_rows, shift=shift, n_keep=n_keep,
                             has_state=has_state, n_parts=n_parts)
    return pl.pallas_call(
        kern, grid=(nb, n_l), in_specs=in_specs, out_specs=out_specs, out_shape=out_shape,
        scratch_shapes=[pltpu.VMEM((carry_rows + tm, CONV_DIM), F32)],
        compiler_params=pltpu.CompilerParams(dimension_semantics=("arbitrary", "arbitrary"),
                                             vmem_limit_bytes=VMEM_LIMIT),
        name="inproj",
    )(*args)


def _attn_kernel(sinks_ref, q_ref, kc_ref, kp_ref, vc_ref, vp_ref, o_ref, *, tq):
    i = pl.program_id(1)
    nblk = tq // WINDOW
    lane = lax.broadcasted_iota(jnp.int32, (WINDOW, LANES), 1)
    low = lane < HEAD_DIM
    qi = lax.broadcasted_iota(jnp.int32, (WINDOW, 2 * WINDOW), 0)
    kj = lax.broadcasted_iota(jnp.int32, (WINDOW, 2 * WINDOW), 1)
    diff = qi + WINDOW - kj
    band = (diff >= 0) & (diff < WINDOW)
    zero_bf = jnp.zeros((WINDOW, LANES), BF16)
    for blk in range(nblk):
        r0 = blk * WINDOW
        if blk == 0:
            k_prev, v_prev = kp_ref[0], vp_ref[0]
            mask = band & ((i > 0) | (kj >= WINDOW))
        else:
            k_prev = kc_ref[0, r0 - WINDOW:r0, :]
            v_prev = vc_ref[0, r0 - WINDOW:r0, :]
            mask = band
        kk = jnp.concatenate([k_prev, kc_ref[0, r0:r0 + WINDOW, :]], axis=0)
        vv = jnp.concatenate([v_prev, vc_ref[0, r0:r0 + WINDOW, :]], axis=0)
        for pair in range(N_Q_HEADS // 2):
            g = (2 * pair) // (N_Q_HEADS // N_KV_HEADS)
            qp = q_ref[0, r0:r0 + WINDOW, pair * LANES:(pair + 1) * LANES]
            outs = []
            for e in range(2):
                h = 2 * pair + e
                var = 0 if g == e else 1
                qh = jnp.where(low if e == 0 else ~low, qp, zero_bf)
                s = _dot_nt(qh, kk[:, var * LANES:(var + 1) * LANES])
                s = jnp.where(mask, s, -jnp.inf)
                sk = sinks_ref[h]
                m = jnp.maximum(jnp.max(s, axis=-1, keepdims=True), sk)
                p = jnp.exp(s - m)
                den = jnp.sum(p, axis=-1, keepdims=True) + jnp.exp(sk - m)
                o = _dot(p.astype(BF16), vv[:, var * LANES:(var + 1) * LANES])
                outs.append(o / den)
            o_ref[0, r0:r0 + WINDOW, pair * LANES:(pair + 1) * LANES] = (
                jnp.where(low, outs[0], outs[1]).astype(BF16))


def _attention(sinks, q, k2, v2, *, tq):
    nb, seq, _ = q.shape
    r = tq // WINDOW
    cur = lambda w: pl.BlockSpec((1, tq, w), lambda b, i: (b, i, 0))
    prev = lambda w: pl.BlockSpec((1, WINDOW, w), lambda b, i: (b, jnp.maximum(i * r - 1, 0), 0))
    return pl.pallas_call(
        functools.partial(_attn_kernel, tq=tq),
        grid=(nb, seq // tq),
        in_specs=[pl.BlockSpec(memory_space=pltpu.SMEM),
                  cur(ATTN_WIDTH), cur(2 * KV_WIDTH), prev(2 * KV_WIDTH), cur(2 * KV_WIDTH), prev(2 * KV_WIDTH)],
        out_specs=cur(ATTN_WIDTH),
        out_shape=jax.ShapeDtypeStruct((nb, seq, ATTN_WIDTH), BF16),
        compiler_params=pltpu.CompilerParams(dimension_semantics=("arbitrary", "arbitrary"),
                                             vmem_limit_bytes=VMEM_LIMIT),
        name="swa_attn",
    )(sinks, q, k2, k2, v2, v2)


def _ssd_kernel(xs_ref, bc_ref, dt_ref, a_ref, dskip_ref, tri_ref,
                y_ref, hT_ref, h_s, *, tc):
    l = pl.program_id(1)
    n_l = pl.num_programs(1)
    gw = SSM_WIDTH // N_SSM_GROUPS
    hpg = N_SSM_HEADS // N_SSM_GROUPS

    @pl.when(l == 0)
    def _():
        h_s[...] = jnp.zeros(h_s.shape, F32)

    lane = lax.broadcasted_iota(jnp.int32, (SSD_CHUNK, LANES), 1)
    low = lane < SSM_HEAD_DIM
    qi = lax.broadcasted_iota(jnp.int32, (SSD_CHUNK, SSD_CHUNK), 0)
    sj = lax.broadcasted_iota(jnp.int32, (SSD_CHUNK, SSD_CHUNK), 1)
    causal = sj <= qi
    a_neg = -jnp.exp(a_ref[...])
    tri = tri_ref[...]
    zero_bf = jnp.zeros((SSD_CHUNK, LANES), BF16)

    for c in range(tc // SSD_CHUNK):
        rows = slice(c * SSD_CHUNK, (c + 1) * SSD_CHUNK)
        dt = dt_ref[0, rows, :]
        hi, mid, lo = _split3(dt * a_neg)
        acum = (_dot(tri, hi) + _dot(tri, mid) + _dot(tri, lo)) * LOG2E
        acum_t = acum.T
        dt_t = dt.T
        a_end_t = acum_t[:, SSD_CHUNK - 1:SSD_CHUNK]
        w_end_t = dt_t * jnp.exp2(a_end_t - acum_t)
        dec_t = jnp.exp2(a_end_t)
        dec_t = jnp.broadcast_to(dec_t, (LANES, LANES))
        row_t = acum_t - jnp.log2(dt_t)
        for g in range(N_SSM_GROUPS):
            b_g = bc_ref[0, rows, g * D_STATE:(g + 1) * D_STATE]
            c_g = bc_ref[0, rows, (N_SSM_GROUPS + g) * D_STATE:(N_SSM_GROUPS + g + 1) * D_STATE]
            cb = _dot_nt(c_g, b_g)
            b_gt = b_g.astype(F32).T
            y_inter = _dot(c_g, h_s[g].astype(BF16))
            for pr in range(hpg // 2):
                lhs_y, lhs_s, escale, dsc = [], [], [], []
                for e in range(2):
                    h = g * hpg + 2 * pr + e
                    col = jnp.broadcast_to(acum[:, h:h + 1], (SSD_CHUNK, SSD_CHUNK))
                    decay_dt = jnp.where(causal, jnp.exp2(col - row_t[h:h + 1, :]), 0.0)
                    lhs_y.append((cb * decay_dt).astype(BF16))
                    lhs_s.append((b_gt * w_end_t[h:h + 1, :]).astype(BF16))
                    escale.append(jnp.exp2(col))
                    dsc.append(dec_t[h:h + 1, :])
                c0 = g * gw + pr * LANES
                x_pair = xs_ref[0, rows, c0:c0 + LANES]
                x_bf = x_pair.astype(BF16)
                rhs = jnp.concatenate([jnp.where(low, x_bf, zero_bf), jnp.where(low, zero_bf, x_bf)], axis=0)
                y_in = _dot(jnp.concatenate(lhs_y, axis=1), rhs)
                st = _dot(jnp.concatenate(lhs_s, axis=1), rhs)
                lc = pr * LANES
                y = y_in + y_inter[:, lc:lc + LANES] * jnp.where(low, escale[0], escale[1])
                y_ref[0, rows, c0:c0 + LANES] = y + dskip_ref[:, c0:c0 + LANES] * x_pair
                h_s[g, :, lc:lc + LANES] = h_s[g, :, lc:lc + LANES] * jnp.where(low, dsc[0], dsc[1]) + st

    @pl.when(l == n_l - 1)
    def _():
        hT_ref[0] = h_s[...]


def _ssd(xs, bc, dt, a_pad, dskip, tri, *, tc):
    nb, seq, _ = xs.shape
    row = lambda w: pl.BlockSpec((1, tc, w), lambda b, l: (b, l, 0))
    const = lambda shape: pl.BlockSpec(shape, lambda b, l: (0,) * len(shape))
    gw = SSM_WIDTH // N_SSM_GROUPS
    return pl.pallas_call(
        functools.partial(_ssd_kernel, tc=tc),
        grid=(nb, seq // tc),
        in_specs=[row(SSM_WIDTH), row(CONV_DIM - SSM_WIDTH), row(LANES),
                  const((1, LANES)), const((1, SSM_WIDTH)), const((SSD_CHUNK, SSD_CHUNK))],
        out_specs=[row(SSM_WIDTH),
                   pl.BlockSpec((1, N_SSM_GROUPS, D_STATE, gw), lambda b, l: (b, 0, 0, 0))],
        out_shape=[jax.ShapeDtypeStruct((nb, seq, SSM_WIDTH), F32),
                   jax.ShapeDtypeStruct((nb, N_SSM_GROUPS, D_STATE, gw), F32)],
        scratch_shapes=[pltpu.VMEM((N_SSM_GROUPS, D_STATE, gw), F32)],
        compiler_params=pltpu.CompilerParams(dimension_semantics=("arbitrary", "arbitrary"),
                                             vmem_limit_bytes=VMEM_LIMIT),
        name="ssd_scan",
    )(xs, bc, dt, a_pad, dskip, tri)


def _outffn_kernel(*refs, ffc, gated):
    if gated:
        x_ref, oa_ref, ys_ref, z_ref, gnorm_ref, wo_ref, gffn_ref, wgu_ref, wd_ref, o_ref = refs
        gw = SSM_WIDTH // N_SSM_GROUPS
        parts = []
        for g in range(N_SSM_GROUPS):
            ls = slice(g * gw, (g + 1) * gw)
            yg = ys_ref[:, ls] * _silu(z_ref[:, ls])
            yg = yg * lax.rsqrt(jnp.mean(yg * yg, axis=-1, keepdims=True) + EPS)
            parts.append((yg * gnorm_ref[:, ls]).astype(BF16))
        ys = jnp.concatenate(parts, axis=1)
    else:
        x_ref, oa_ref, ys_ref, wo_ref, gffn_ref, wgu_ref, wd_ref, o_ref = refs
        ys = ys_ref[...]
    x = x_ref[...]
    xm = x + _dot(oa_ref[...], wo_ref[0, 0:ATTN_WIDTH, :]) + _dot(ys, wo_ref[0, ATTN_WIDTH:, :])
    hn = _rms_rows(xm, gffn_ref[...]).astype(BF16)
    acc = jnp.zeros_like(xm)
    c0 = 0
    for width in ffc:
        g = _dot(hn, wgu_ref[0, :, c0:c0 + width])
        up = _dot(hn, wgu_ref[0, :, D_FF + c0:D_FF + c0 + width])
        acc = acc + _dot((_silu(g) * up).astype(BF16), wd_ref[0, c0:c0 + width, :])
        c0 += width
    o_ref[...] = xm + acc


def _outffn(x, oa, ys, gate, wo, gffn, wgu, wd, *, layer, tm, ffc):
    rows = x.shape[0]
    row = lambda w: pl.BlockSpec((tm, w), lambda i: (i, 0))
    const = lambda shape: pl.BlockSpec(shape, lambda i: (0,) * len(shape), pipeline_mode=pl.Buffered(1))
    gated = gate is not None
    in_specs = [row(D_MODEL), row(ATTN_WIDTH), row(SSM_WIDTH)]
    args = [x, oa, ys]
    if gated:
        in_specs += [row(SSM_WIDTH), const((1, SSM_WIDTH))]
        args += list(gate)
    wblk = lambda r, w: pl.BlockSpec((1, r, w), lambda i: (layer, 0, 0), pipeline_mode=pl.Buffered(1))
    in_specs += [wblk(ATTN_WIDTH + SSM_WIDTH, D_MODEL), const((1, D_MODEL)),
                 wblk(D_MODEL, 2 * D_FF), wblk(D_FF, D_MODEL)]
    args += [wo, gffn, wgu, wd]
    return pl.pallas_call(
        functools.partial(_outffn_kernel, ffc=ffc, gated=gated),
        grid=(rows // tm,),
        in_specs=in_specs,
        out_specs=row(D_MODEL),
        out_shape=jax.ShapeDtypeStruct((rows, D_MODEL), F32),
        compiler_params=pltpu.CompilerParams(dimension_semantics=("arbitrary",),
                                             vmem_limit_bytes=VMEM_LIMIT),
        name="outproj_ffn",
    )(*args)


T_PAD = SUBLANES


def _attn_dec_kernel(q_ref, kc_ref, kn_ref, vc_ref, vn_ref, sink_ref, o_ref, wk_ref, wv_ref, *, t_new):
    q = q_ref[...]
    bb, nq, _ = q.shape
    qpk = N_Q_HEADS // N_KV_HEADS
    n_c = kc_ref.shape[1]
    kc, vc, kn, vn = kc_ref[...], vc_ref[...], kn_ref[...], vn_ref[...]
    s_c = jnp.einsum('bqd,bkd->bqk', q, kc.astype(BF16), preferred_element_type=F32)
    s_n = jnp.einsum('bqd,bkd->bqk', q, kn.astype(BF16), preferred_element_type=F32)
    t_c = (lax.broadcasted_iota(jnp.int32, (bb, nq, n_c), 1) % (t_new * qpk)) // qpk
    j_c = lax.broadcasted_iota(jnp.int32, (bb, nq, n_c), 2)
    s_c = jnp.where(j_c > t_c + (n_c - WINDOW), s_c, -jnp.inf)
    t_n = (lax.broadcasted_iota(jnp.int32, (bb, nq, T_PAD), 1) % (t_new * qpk)) // qpk
    j_n = lax.broadcasted_iota(jnp.int32, (bb, nq, T_PAD), 2)
    s_n = jnp.where((j_n <= t_n) & (j_n < t_new), s_n, -jnp.inf)
    sk = sink_ref[...]
    m = jnp.maximum(jnp.maximum(jnp.max(s_c, axis=-1, keepdims=True), jnp.max(s_n, axis=-1, keepdims=True)), sk)
    p_c = jnp.exp(s_c - m)
    p_n = jnp.exp(s_n - m)
    den = jnp.sum(p_c, axis=-1, keepdims=True) + jnp.sum(p_n, axis=-1, keepdims=True) + jnp.exp(sk - m)
    o = (jnp.einsum('bqk,bkd->bqd', p_c.astype(BF16), vc.astype(BF16), preferred_element_type=F32)
         + jnp.einsum('bqk,bkd->bqd', p_n.astype(BF16), vn.astype(BF16), preferred_element_type=F32))
    o = o / den
    rows_g = nq // N_KV_HEADS
    lane = lax.broadcasted_iota(jnp.int32, (bb, rows_g, LANES), 2)
    merged = o[:, 0:rows_g, :]
    for g in range(1, N_KV_HEADS):
        merged = jnp.where(lane // HEAD_DIM == g, o[:, g * rows_g:(g + 1) * rows_g, :], merged)
    o_ref[...] = merged.astype(BF16)

    sub = lax.broadcasted_iota(jnp.int32, (bb, T_PAD, LANES), 1)
    for c, n, w_ref in ((kc, kn, wk_ref), (vc, vn, wv_ref)):
        shifted = pltpu.roll(c.reshape(bb * n_c, LANES), bb * n_c - t_new, 0).reshape(bb, n_c, LANES)
        tail = pltpu.roll(n.reshape(bb * T_PAD, LANES), T_PAD - t_new, 0).reshape(bb, T_PAD, LANES)
        w_ref[:, 0:n_c - T_PAD, :] = shifted[:, 0:n_c - T_PAD, :]
        w_ref[:, n_c - T_PAD:, :] = jnp.where(sub < T_PAD - t_new, shifted[:, n_c - T_PAD:, :], tail)


def _attention_dec(q, kc, kn, vc, vn, sink_rows, *, t_new, bb):
    n_seq, nq, w = q.shape
    n_c = kc.shape[1]
    blk = lambda r: pl.BlockSpec((bb, r, w), lambda i: (i, 0, 0))
    win = jax.ShapeDtypeStruct((n_seq, n_c, w), F32)
    return pl.pallas_call(
        functools.partial(_attn_dec_kernel, t_new=t_new),
        grid=(n_seq // bb,),
        in_specs=[blk(nq), blk(n_c), blk(T_PAD), blk(n_c), blk(T_PAD), pl.BlockSpec((nq, 1), lambda i: (0, 0))],
        out_specs=[blk(nq // N_KV_HEADS), blk(n_c), blk(n_c)],
        out_shape=[jax.ShapeDtypeStruct((n_seq, nq // N_KV_HEADS, w), BF16), win, win],
        compiler_params=pltpu.CompilerParams(dimension_semantics=("arbitrary",), vmem_limit_bytes=VMEM_LIMIT),
        name="swa_attn_decode",
    )(q, kc, kn, vc, vn, sink_rows)


def _ssd_dec_kernel(*refs, bb, t_new, n_prev):
    (xs_ref, bc_ref, dt_ref, z_ref, h0_ref) = refs[:5]
    prev_refs = refs[5:5 + n_prev]
    (a_ref, dskip_ref, gnorm_ref, tri_ref, ones_ref, exp_ref,
     y_ref, h_ref, aw_s, xdt_s, xw_s, ea_s, dec_s) = refs[5 + n_prev:]
    d = pl.program_id(0)
    gw = SSM_WIDTH // N_SSM_GROUPS
    hpg = N_SSM_HEADS // N_SSM_GROUPS
    rows = bb * T_PAD

    for j in range(n_prev):
        @pl.when(d == j)
        def _(j=j):
            h_ref[0] = prev_refs[j][0]

    @pl.when(d == n_prev)
    def _():
        expand = exp_ref[...]

        def widen(v):
            hi, mid, lo = _split3(v)
            return _dot(hi, expand) + _dot(mid, expand) + _dot(lo, expand)

        def seq_sum(m, v):
            hi, mid, lo = _split3(v)
            return _dot(m, hi) + _dot(m, mid) + _dot(m, lo)

        dt = dt_ref[...].reshape(rows, LANES)
        dta = dt * (-jnp.exp(a_ref[...]))
        a_w = widen(seq_sum(tri_ref[...], dta))
        a_end = seq_sum(ones_ref[...], dta)
        xdt = xs_ref[...].reshape(rows, SSM_WIDTH) * widen(dt)
        aw_s[...] = a_w
        xdt_s[...] = xdt
        xw_s[...] = xdt * jnp.exp(widen(a_end) - a_w)
        ea_s[...] = jnp.exp(a_w)
        dec_s[...] = jnp.exp(a_end)

        t_row = lax.broadcasted_iota(jnp.int32, (T_PAD, gw), 0)

        def seq_body(i, carry):
            r0 = pl.multiple_of(i * T_PAD, T_PAD)
            rs = pl.ds(r0, T_PAD)
            dec = jnp.broadcast_to(dec_s[pl.ds(r0, 1), :], (LANES, LANES)).T
            bc = bc_ref[i]
            for g in range(N_SSM_GROUPS):
                b_g = bc[:, g * D_STATE:(g + 1) * D_STATE]
                c_g = bc[:, (N_SSM_GROUPS + g) * D_STATE:(N_SSM_GROUPS + g + 1) * D_STATE]
                cb = _dot_nt(c_g, b_g)
                ls = slice(g * gw, (g + 1) * gw)
                a_g = aw_s[rs, ls]
                xdt_g = xdt_s[rs, ls]
                y = jnp.zeros((T_PAD, gw), F32)
                for s in range(t_new):
                    w = jnp.where(t_row >= s, jnp.exp(a_g - a_g[s:s + 1, :]), 0.0)
                    y = y + (w * cb[:, s:s + 1]) * xdt_g[s:s + 1, :]
                h0g = h0_ref[0, i, g * hpg:(g + 1) * hpg].reshape(gw, D_STATE)
                y = y + _dot_nt(c_g, h0g.astype(BF16)) * ea_s[rs, ls]
                y = y + dskip_ref[:, ls] * xs_ref[i, :, ls]
                y = y * _silu(z_ref[i, :, ls])
                y = y * lax.rsqrt(jnp.mean(y * y, axis=-1, keepdims=True) + EPS)
                y_ref[i, :, ls] = (y * gnorm_ref[:, ls]).astype(BF16)
                delta = lax.dot_general(xw_s[rs, ls].astype(BF16), b_g, (((0,), (0,)), ((), ())),
                                        preferred_element_type=F32)
                for hl in range(hpg):
                    h = g * hpg + hl
                    h_ref[0, i, h] = (h0_ref[0, i, h] * dec[h:h + 1, :]
                                      + delta[hl * SSM_HEAD_DIM:(hl + 1) * SSM_HEAD_DIM, :])
            return carry

        lax.fori_loop(0, bb, seq_body, 0)


def _ssd_dec(xs3, bc3, dt3, z3, state_all, layer, prev_states, a_pad, dskip, gnorm, tri_b, ones_b, expand,
             *, bb, t_new):
    nb = xs3.shape[0]
    nblk = nb // bb
    n_prev = len(prev_states)
    rows = bb * T_PAD

    def phase_block(j):
        return lambda d, i: jnp.clip(i + (d - j) * nblk, 0, nblk - 1)

    cur = phase_block(n_prev)
    blk = lambda w: pl.BlockSpec((bb, T_PAD, w), lambda d, i: (cur(d, i), 0, 0))
    const = lambda shape: pl.BlockSpec(shape, lambda d, i: (0,) * len(shape))
    st_shape = (1, bb, N_SSM_HEADS, SSM_HEAD_DIM, D_STATE)
    in_specs = [blk(SSM_WIDTH), blk(CONV_DIM - SSM_WIDTH), blk(LANES), blk(SSM_WIDTH),
                pl.BlockSpec(st_shape, lambda d, i: (layer, cur(d, i), 0, 0, 0))]
    for j in range(n_prev):
        in_specs.append(pl.BlockSpec(st_shape, lambda d, i, j=j: (0, phase_block(j)(d, i), 0, 0, 0)))
    in_specs += [const((1, LANES)), const((1, SSM_WIDTH)), const((1, SSM_WIDTH)),
                 const((rows, rows)), const((rows, rows)), const((LANES, SSM_WIDTH))]
    return pl.pallas_call(
        functools.partial(_ssd_dec_kernel, bb=bb, t_new=t_new, n_prev=n_prev),
        grid=(n_prev + 1, nblk),
        in_specs=in_specs,
        out_specs=[blk(SSM_WIDTH), pl.BlockSpec(st_shape, lambda d, i: (d, i, 0, 0, 0))],
        out_shape=[jax.ShapeDtypeStruct((nb, T_PAD, SSM_WIDTH), BF16),
                   jax.ShapeDtypeStruct((n_prev + 1,) + state_all.shape[1:], F32)],
        scratch_shapes=[pltpu.VMEM((rows, SSM_WIDTH), F32), pltpu.VMEM((rows, SSM_WIDTH), F32),
                        pltpu.VMEM((rows, SSM_WIDTH), F32), pltpu.VMEM((rows, SSM_WIDTH), F32),
                        pltpu.VMEM((rows, LANES), F32)],
        compiler_params=pltpu.CompilerParams(dimension_semantics=("arbitrary", "arbitrary"),
                                             vmem_limit_bytes=VMEM_LIMIT),
        name="ssd_decode",
    )(xs3, bc3, dt3, z3, state_all, *prev_states, a_pad, dskip, gnorm, tri_b, ones_b, expand)


def _rope_tables(pos):
    half = HEAD_DIM // 2
    inv = ROPE_THETA ** (-np.arange(half, dtype=np.float64) / half)
    ang = np.asarray(pos, np.float64)[:, None] * inv[None, :]
    cos = np.tile(np.cos(ang), (1, LANES // half))
    sin = np.sin(ang)
    sin_signed = np.tile(np.concatenate([-sin, sin], axis=1), (1, LANES // HEAD_DIM))
    return jnp.asarray(cos, F32), jnp.asarray(sin_signed, F32)


def _layer_consts(l, p):
    return dict(
        layer=l,
        gmix=p['norm_mix'][l][None, :],
        qn=jnp.tile(p['q_norm'][l], LANES // HEAD_DIM)[None, :],
        kn=jnp.tile(p['k_norm'][l], LANES // HEAD_DIM)[None, :],
        convw=p['conv_w'][l],
        convb=p['conv_b'][l][None, :],
        dtb=jnp.pad(p['dt_bias'][l], (0, LANES - N_SSM_HEADS))[None, :],
        a_pad=jnp.pad(p['a_log'][l], (0, LANES - N_SSM_HEADS))[None, :],
        dskip=jnp.repeat(p['d_skip'][l], SSM_HEAD_DIM)[None, :],
        gnorm=p['ssm_norm'][l][None, :],
        sinks=p['sinks'][l],
        gffn=p['norm_ffn'][l][None, :],
    )


def _state_from_t(h_t):
    nb = h_t.shape[0]
    hpg = N_SSM_HEADS // N_SSM_GROUPS
    h = h_t.reshape(nb, N_SSM_GROUPS, D_STATE, hpg, SSM_HEAD_DIM)
    return jnp.transpose(h, (0, 1, 3, 4, 2)).reshape(nb, N_SSM_HEADS, SSM_HEAD_DIM, D_STATE)


def _prompt_layer(x, c, shared, *, tm):
    nb, seq, _ = x.shape
    (q, k2, v2, z, xs, bc, dt, knew, vnew, convst) = _inproj(
        x, c['gmix'], shared['win'], shared['m128'], c['qn'], c['kn'], shared['cos_p'], shared['sin_p'],
        c['convw'], c['convb'], c['dtb'], None, layer=c['layer'], tm=tm, carry_rows=SUBLANES, shift=1, n_keep=min(WINDOW, seq))
    oa = _attention(c['sinks'], q, k2, v2, tq=tm)
    ys, h_t = _ssd(xs, bc, dt, c['a_pad'], c['dskip'], shared['tri'], tc=tm)
    rows = nb * seq
    xo = _outffn(x.reshape(rows, D_MODEL), oa.reshape(rows, ATTN_WIDTH), ys.reshape(rows, SSM_WIDTH),
                 (z.reshape(rows, SSM_WIDTH), c['gnorm']), shared['wo'], c['gffn'], shared['wgu'], shared['wd'],
                 layer=c['layer'], tm=tm,
                 ffc=FF_CHUNKS)
    n_keep = knew.shape[1]
    return (xo.reshape(nb, seq, D_MODEL),
            knew.reshape(nb, n_keep, N_KV_HEADS, HEAD_DIM), vnew.reshape(nb, n_keep, N_KV_HEADS, HEAD_DIM),
            convst[:, SUBLANES - (CONV_W - 1):, :], _state_from_t(h_t))


def _sample_layer(x_t, cache_k, cache_v, conv0, state_all, layer, prev_states, c, shared, *, n_seq, t_new):
    rows = t_new * n_seq
    qpk = N_Q_HEADS // N_KV_HEADS
    n_c = cache_k.shape[1]
    conv0_t = jnp.transpose(conv0, (1, 0, 2)).reshape((CONV_W - 1) * n_seq, CONV_DIM)
    (q, _, _, z, xs, bc, dt, knew, vnew, convst) = _inproj(
        x_t, c['gmix'], shared['win'], shared['m128'], c['qn'], c['kn'], shared['cos_s'], shared['sin_s'],
        c['convw'], c['convb'], c['dtb'], conv0_t, layer=c['layer'], tm=rows, carry_rows=(CONV_W - 1) * n_seq, shift=n_seq,
        n_keep=rows)

    def to_seq(a):
        a = jnp.transpose(a[0].reshape(t_new, n_seq, a.shape[-1]), (1, 0, 2))
        return jnp.pad(a, ((0, 0), (0, T_PAD - t_new), (0, 0)))

    q5 = jnp.transpose(q[0].reshape(t_new, n_seq, N_KV_HEADS, qpk, HEAD_DIM), (1, 2, 0, 3, 4))
    q5 = q5.reshape(n_seq, N_KV_HEADS, t_new * qpk, HEAD_DIM)
    q_m = jnp.concatenate(
        [jnp.pad(q5[:, g], ((0, 0), (0, 0), (g * HEAD_DIM, (N_KV_HEADS - 1 - g) * HEAD_DIM)))
         for g in range(N_KV_HEADS)], axis=1)
    sink_rows = jnp.tile(c['sinks'].reshape(N_KV_HEADS, 1, qpk), (1, t_new, 1)).reshape(-1, 1)
    o_m, win_k, win_v = _attention_dec(
        q_m, cache_k.reshape(n_seq, n_c, KV_WIDTH), to_seq(knew), cache_v.reshape(n_seq, n_c, KV_WIDTH),
        to_seq(vnew), sink_rows, t_new=t_new, bb=min(32, n_seq))
    o5 = o_m.reshape(n_seq, t_new, qpk, N_KV_HEADS, HEAD_DIM)
    oa = jnp.transpose(o5, (1, 0, 3, 2, 4)).reshape(rows, ATTN_WIDTH)
    win_k = win_k.reshape(cache_k.shape)
    win_v = win_v.reshape(cache_v.shape)

    y3, h_new = _ssd_dec(to_seq(xs), to_seq(bc), to_seq(dt), to_seq(z), state_all, layer, prev_states,
                         c['a_pad'], c['dskip'], c['gnorm'], shared['tri_b'], shared['ones_b'], shared['expand'],
                         bb=shared['bb_s'], t_new=t_new)
    ys = jnp.transpose(y3[:, :t_new], (1, 0, 2)).reshape(rows, SSM_WIDTH)

    xo = _outffn(x_t[0], oa, ys, None, shared['wo'], c['gffn'], shared['wgu'], shared['wd'], layer=c['layer'],
                 tm=min(512, rows), ffc=FF_CHUNKS)

    conv_new = jnp.transpose(convst[0].reshape(CONV_W - 1, n_seq, CONV_DIM), (1, 0, 2))
    return xo[None], win_k, win_v, conv_new, h_new


def kernel(x_prompt, x_sample, cache_win_k, cache_win_v, state_conv, state_ssm,
           norm_mix, w_in, q_norm, k_norm, sinks, conv_w, conv_b, dt_bias, a_log,
           d_skip, ssm_norm, w_out, norm_ffn, w_gate_up, w_down):
    p = dict(norm_mix=norm_mix, w_in=w_in, q_norm=q_norm, k_norm=k_norm, sinks=sinks, conv_w=conv_w,
             conv_b=conv_b, dt_bias=dt_bias, a_log=a_log, d_skip=d_skip, ssm_norm=ssm_norm, w_out=w_out,
             norm_ffn=norm_ffn, w_gate_up=w_gate_up, w_down=w_down)
    depth = w_in.shape[0]
    seq = x_prompt.shape[1]
    cos_p, sin_p = _rope_tables(np.arange(seq))
    half_blk = jnp.arange(LANES) // HEAD_DIM
    m128 = (jnp.where(half_blk[:, None] == half_blk[None, :], 1.0 / HEAD_DIM, 0.0)).astype(BF16)
    m128 = jnp.concatenate([m128, m128], axis=0)
    tri = (jnp.arange(SSD_CHUNK)[:, None] >= jnp.arange(SSD_CHUNK)[None, :]).astype(BF16)
    n_seq, t_new, _ = x_sample.shape
    pos_s = PAST_LEN + np.repeat(np.arange(t_new), n_seq)
    cos_s, sin_s = _rope_tables(pos_s)
    bb_s = min(8, n_seq)
    r_idx = jnp.arange(bb_s * T_PAD)
    same_seq = (r_idx[:, None] // T_PAD) == (r_idx[None, :] // T_PAD)
    tri_b = (same_seq & (r_idx[:, None] >= r_idx[None, :])).astype(BF16)
    ones_b = same_seq.astype(BF16)
    expand = (jnp.arange(LANES)[:, None] == (jnp.arange(SSM_WIDTH) // SSM_HEAD_DIM)[None, :]).astype(BF16)
    weights = dict(win=jnp.pad(w_in, ((0, 0), (0, 0), (0, PROJ_PAD - IN_PROJ_WIDTH))).astype(BF16),
                   wo=w_out.astype(BF16), wgu=w_gate_up.astype(BF16), wd=w_down.astype(BF16))
    shared = dict(cos_p=cos_p, sin_p=sin_p, cos_s=cos_s, sin_s=sin_s, m128=m128, tri=tri, tri_b=tri_b, **weights,
                  ones_b=ones_b, bb_s=bb_s, expand=expand)
    tm = min(512, seq)

    xp = x_prompt
    xs = jnp.transpose(x_sample, (1, 0, 2)).reshape(1, t_new * n_seq, D_MODEL)
    pk, pv, pc, ph = [], [], [], []
    sk, sv, sc, sh = [], [], [], []
    for l in range(depth):
        c = _layer_consts(l, p)
        xp, k1, v1, c1, h1 = _prompt_layer(xp, c, shared, tm=tm)
        prev = sh if l == depth - 1 else []
        xs, k2, v2, c2, h2 = _sample_layer(xs, cache_win_k[l], cache_win_v[l], state_conv[l], state_ssm, l, prev,
                                           c, shared, n_seq=n_seq, t_new=t_new)
        pk.append(k1); pv.append(v1); pc.append(c1); ph.append(h1)
        sk.append(k2); sv.append(v2); sc.append(c2); sh.append(h2)
    ys = jnp.transpose(xs.reshape(t_new, n_seq, D_MODEL), (1, 0, 2))
    return (xp, ys, jnp.stack(pk), jnp.stack(pv), jnp.stack(pc), jnp.stack(ph),
            jnp.stack(sk), jnp.stack(sv), jnp.stack(sc), sh[-1])
```

```python
import functools
import math

import numpy as np
import jax
import jax.numpy as jnp
from jax import lax
from jax.experimental import pallas as pl
from jax.experimental.pallas import tpu as pltpu

F32 = jnp.float32
BF16 = jnp.bfloat16

D_MODEL = 1024
HEAD_DIM = 64
N_Q_HEADS = 8
N_KV_HEADS = 2
ATTN_WIDTH = N_Q_HEADS * HEAD_DIM
KV_WIDTH = N_KV_HEADS * HEAD_DIM
WINDOW = 128
ROPE_THETA = 10000.0
ATTN_SCALE = HEAD_DIM ** -0.5
SSM_HEAD_DIM = 64
N_SSM_HEADS = 16
SSM_WIDTH = N_SSM_HEADS * SSM_HEAD_DIM
N_SSM_GROUPS = 2
D_STATE = 128
CONV_W = 4
CONV_DIM = SSM_WIDTH + 2 * N_SSM_GROUPS * D_STATE
SSD_CHUNK = 128
D_FF = 2816
EPS = 1e-6
LOG2E = math.log2(math.e)
PAST_LEN = 16384

LANES = 128
SUBLANES = 8

COL_Q = 0
COL_K = COL_Q + ATTN_WIDTH
COL_V = COL_K + KV_WIDTH
COL_Z = COL_V + KV_WIDTH
COL_XBC = COL_Z + SSM_WIDTH
COL_DT = COL_XBC + CONV_DIM
IN_PROJ_WIDTH = COL_DT + N_SSM_HEADS
PROJ_PAD = COL_DT + LANES
QKV_COLS = COL_Z

VMEM_LIMIT = 56 * 1024 * 1024
FF_CHUNKS = (6 * 256, 5 * 256)


def _dot(a, b):
    return jnp.dot(a, b, preferred_element_type=F32)


def _dot_nt(a, b):
    return lax.dot_general(a, b, (((1,), (1,)), ((), ())), preferred_element_type=F32)


def _silu(x):
    return x / (1.0 + jnp.exp(-x))


def _softplus(x):
    return jnp.maximum(x, 0.0) + jnp.log1p(jnp.exp(-jnp.abs(x)))


def _rms_rows(x, g):
    return x * lax.rsqrt(jnp.mean(x * x, axis=-1, keepdims=True) + EPS) * g


def _split2(x):
    hi = x.astype(BF16)
    lo = (x - hi.astype(F32)).astype(BF16)
    return hi, lo


def _split3(x):
    hi = x.astype(BF16)
    r1 = x - hi.astype(F32)
    mid = r1.astype(BF16)
    lo = (r1 - mid.astype(F32)).astype(BF16)
    return hi, mid, lo


def _head_norm_rope(x, m128, g, cos, sin_signed, low_half):
    hi, lo = _split2(x * x)
    ms = _dot(jnp.concatenate([hi, lo], axis=1), m128)
    xn = x * lax.rsqrt(ms + EPS) * g
    rot = jnp.where(low_half, pltpu.roll(xn, LANES - HEAD_DIM // 2, 1),
                    pltpu.roll(xn, HEAD_DIM // 2, 1))
    return xn * cos + rot * sin_signed


def _inproj_kernel(*refs, tm, carry_rows, shift, n_keep, has_state, n_parts):
    if has_state:
        (x_ref, gmix_ref, win_ref, m128_ref, qn_ref, kn_ref, cos_ref, sin_ref,
         convw_ref, convb_ref, dtb_ref, conv0_ref,
         q_ref, k2_ref, v2_ref, z_ref, xs_ref, bc_ref, dt_ref, knew_ref, vnew_ref, convst_ref,
         xbc_s) = refs
    else:
        (x_ref, gmix_ref, win_ref, m128_ref, qn_ref, kn_ref, cos_ref, sin_ref,
         convw_ref, convb_ref, dtb_ref,
         q_ref, k2_ref, v2_ref, z_ref, xs_ref, bc_ref, dt_ref, knew_ref, vnew_ref, convst_ref,
         xbc_s) = refs
        conv0_ref = None
    l = pl.program_id(1)

    @pl.when(l == 0)
    def _():
        if has_state:
            xbc_s[0:carry_rows, :] = conv0_ref[...]
        else:
            xbc_s[0:carry_rows, :] = jnp.zeros((carry_rows, CONV_DIM), F32)

    m128 = m128_ref[...]
    cw = 512
    nr = tm // n_parts
    lane = lax.broadcasted_iota(jnp.int32, (nr, LANES), 1)
    low_half = (lane % HEAD_DIM) < (HEAD_DIM // 2)
    for part in range(n_parts):
        r0 = part * nr
        rs = slice(r0, r0 + nr)
        base = carry_rows + r0
        u = _rms_rows(x_ref[0, rs, :], gmix_ref[...]).astype(BF16)
        for j in range(CONV_DIM // cw):
            c0 = COL_XBC + j * cw
            xbc_s[base:base + nr, j * cw:(j + 1) * cw] = _dot(u, win_ref[0, :, c0:c0 + cw])
        z_ref[0, rs, :] = _dot(u, win_ref[0, :, COL_Z:COL_Z + SSM_WIDTH])
        heads = _dot(u, win_ref[0, :, 0:QKV_COLS])
        dt_raw = _dot(u, win_ref[0, :, COL_DT:COL_DT + LANES])
        cos = cos_ref[rs, :]
        sin = sin_ref[rs, :]

        for j in range(ATTN_WIDTH // LANES):
            qj = heads[:, COL_Q + j * LANES:COL_Q + (j + 1) * LANES]
            qj = _head_norm_rope(qj, m128, qn_ref[...], cos, sin, low_half)
            q_ref[0, rs, j * LANES:(j + 1) * LANES] = (qj * ATTN_SCALE).astype(BF16)

        k = _head_norm_rope(heads[:, COL_K:COL_K + KV_WIDTH], m128, kn_ref[...], cos, sin, low_half)
        v = heads[:, COL_V:COL_V + KV_WIDTH]
        k2_ref[0, rs, 0:LANES] = k.astype(BF16)
        k2_ref[0, rs, LANES:2 * LANES] = pltpu.roll(k, HEAD_DIM, 1).astype(BF16)
        v2_ref[0, rs, 0:LANES] = v.astype(BF16)
        v2_ref[0, rs, LANES:2 * LANES] = pltpu.roll(v, HEAD_DIM, 1).astype(BF16)
        if part == n_parts - 1:
            knew_ref[0] = k[nr - n_keep:, :]
            vnew_ref[0] = v[nr - n_keep:, :]

        dt_ref[0, rs, :] = _softplus(dt_raw + dtb_ref[...])

        for j in range(CONV_DIM // cw):
            cs = slice(j * cw, (j + 1) * cw)
            w = [convw_ref[i:i + 1, cs] for i in range(CONV_W)]
            bias = convb_ref[:, cs]
            if shift % SUBLANES == 0:
                acc = bias
                for i in range(CONV_W):
                    off = base - (CONV_W - 1 - i) * shift
                    acc = acc + xbc_s[off:off + nr, cs] * w[i]
                act = _silu(acc)
            else:
                cur = xbc_s[base:base + nr, cs]
                acc = cur * w[0]
                for i in range(1, CONV_W):
                    acc = pltpu.roll(acc, shift, 0) + cur * w[i]
                head = bias
                for i in range(CONV_W):
                    off = base - (CONV_W - 1 - i) * shift
                    head = head + xbc_s[off:off + SUBLANES, cs] * w[i]
                act = _silu(jnp.concatenate([head, acc[SUBLANES:, :] + bias], axis=0))
            if j * cw < SSM_WIDTH:
                xs_ref[0, rs, cs] = act
            else:
                bc_ref[0, rs, :] = act.astype(BF16)

    convst_ref[0] = xbc_s[tm:tm + carry_rows, :]
    xbc_s[0:carry_rows, :] = xbc_s[tm:tm + carry_rows, :]


def _inproj(x, gmix, win, m128, qn, kn, cos, sin, convw, convb, dtb, conv0, *, layer, tm, carry_rows, shift,
            n_keep):
    nb, seq, _ = x.shape
    n_l = seq // tm
    n_parts = 1
    has_state = conv0 is not None
    const = lambda shape: pl.BlockSpec(shape, lambda b, l: (0,) * len(shape))
    row = lambda w: pl.BlockSpec((1, tm, w), lambda b, l: (b, l, 0))
    in_specs = [row(D_MODEL), const((1, D_MODEL)),
                pl.BlockSpec((1, D_MODEL, PROJ_PAD), lambda b, l: (layer, 0, 0)), const((2 * LANES, LANES)),
                const((1, LANES)), const((1, LANES)),
                pl.BlockSpec((tm, LANES), lambda b, l: (l, 0)), pl.BlockSpec((tm, LANES), lambda b, l: (l, 0)),
                const((CONV_W, CONV_DIM)), const((1, CONV_DIM)), const((1, LANES))]
    args = [x, gmix, win, m128, qn, kn, cos, sin, convw, convb, dtb]
    if has_state:
        in_specs.append(const((carry_rows, CONV_DIM)))
        args.append(conv0)
    last = lambda rows, w: pl.BlockSpec((1, rows, w), lambda b, l: (b, 0, 0))
    out_specs = [row(ATTN_WIDTH), row(2 * KV_WIDTH), row(2 * KV_WIDTH), row(SSM_WIDTH), row(SSM_WIDTH),
                 row(CONV_DIM - SSM_WIDTH), row(LANES),
                 last(n_keep, KV_WIDTH), last(n_keep, KV_WIDTH), last(carry_rows, CONV_DIM)]
    out_shape = [jax.ShapeDtypeStruct((nb, seq, ATTN_WIDTH), BF16),
                 jax.ShapeDtypeStruct((nb, seq, 2 * KV_WIDTH), BF16),
                 jax.ShapeDtypeStruct((nb, seq, 2 * KV_WIDTH), BF16),
                 jax.ShapeDtypeStruct((nb, seq, SSM_WIDTH), F32),
                 jax.ShapeDtypeStruct((nb, seq, SSM_WIDTH), F32),
                 jax.ShapeDtypeStruct((nb, seq, CONV_DIM - SSM_WIDTH), BF16),
                 jax.ShapeDtypeStruct((nb, seq, LANES), F32),
                 jax.ShapeDtypeStruct((nb, n_keep, KV_WIDTH), F32),
                 jax.ShapeDtypeStruct((nb, n_keep, KV_WIDTH), F32),
                 jax.ShapeDtypeStruct((nb, carry_rows, CONV_DIM), F32)]
    kern = functools.partial(_inproj_kernel, tm=tm, carry_rows=carry_rows, shift=shift, n_keep=n_keep,
                             has_state=has_state, n_parts=n_parts)
    return pl.pallas_call(
        kern, grid=(nb, n_l), in_specs=in_specs, out_specs=out_specs, out_shape=out_shape,
        scratch_shapes=[pltpu.VMEM((carry_rows + tm, CONV_DIM), F32)],
        compiler_params=pltpu.CompilerParams(dimension_semantics=("arbitrary", "arbitrary"),
                                             vmem_limit_bytes=VMEM_LIMIT),
        name="inproj",
    )(*args)


def _attn_kernel(sinks_ref, q_ref, kc_ref, kp_ref, vc_ref, vp_ref, o_ref, *, tq):
    i = pl.program_id(1)
    nblk = tq // WINDOW
    lane = lax.broadcasted_iota(jnp.int32, (WINDOW, LANES), 1)
    low = lane < HEAD_DIM
    qi = lax.broadcasted_iota(jnp.int32, (WINDOW, 2 * WINDOW), 0)
    kj = lax.broadcasted_iota(jnp.int32, (WINDOW, 2 * WINDOW), 1)
    diff = qi + WINDOW - kj
    band = (diff >= 0) & (diff < WINDOW)
    zero_bf = jnp.zeros((WINDOW, LANES), BF16)
    for blk in range(nblk):
        r0 = blk * WINDOW
        if blk == 0:
            k_prev, v_prev = kp_ref[0], vp_ref[0]
            mask = band & ((i > 0) | (kj >= WINDOW))
        else:
            k_prev = kc_ref[0, r0 - WINDOW:r0, :]
            v_prev = vc_ref[0, r0 - WINDOW:r0, :]
            mask = band
        kk = jnp.concatenate([k_prev, kc_ref[0, r0:r0 + WINDOW, :]], axis=0)
        vv = jnp.concatenate([v_prev, vc_ref[0, r0:r0 + WINDOW, :]], axis=0)
        for pair in range(N_Q_HEADS // 2):
            g = (2 * pair) // (N_Q_HEADS // N_KV_HEADS)
            qp = q_ref[0, r0:r0 + WINDOW, pair * LANES:(pair + 1) * LANES]
            outs = []
            for e in range(2):
                h = 2 * pair + e
                var = 0 if g == e else 1
                qh = jnp.where(low if e == 0 else ~low, qp, zero_bf)
                s = _dot_nt(qh, kk[:, var * LANES:(var + 1) * LANES])
                s = jnp.where(mask, s, -jnp.inf)
                sk = sinks_ref[h]
                m = jnp.maximum(jnp.max(s, axis=-1, keepdims=True), sk)
                p = jnp.exp(s - m)
                den = jnp.sum(p, axis=-1, keepdims=True) + jnp.exp(sk - m)
                o = _dot(p.astype(BF16), vv[:, var * LANES:(var + 1) * LANES])
                outs.append(o / den)
            o_ref[0, r0:r0 + WINDOW, pair * LANES:(pair + 1) * LANES] = (
                jnp.where(low, outs[0], outs[1]).astype(BF16))


def _attention(sinks, q, k2, v2, *, tq):
    nb, seq, _ = q.shape
    r = tq // WINDOW
    cur = lambda w: pl.BlockSpec((1, tq, w), lambda b, i: (b, i, 0))
    prev = lambda w: pl.BlockSpec((1, WINDOW, w), lambda b, i: (b, jnp.maximum(i * r - 1, 0), 0))
    return pl.pallas_call(
        functools.partial(_attn_kernel, tq=tq),
        grid=(nb, seq // tq),
        in_specs=[pl.BlockSpec(memory_space=pltpu.SMEM),
                  cur(ATTN_WIDTH), cur(2 * KV_WIDTH), prev(2 * KV_WIDTH), cur(2 * KV_WIDTH), prev(2 * KV_WIDTH)],
        out_specs=cur(ATTN_WIDTH),
        out_shape=jax.ShapeDtypeStruct((nb, seq, ATTN_WIDTH), BF16),
        compiler_params=pltpu.CompilerParams(dimension_semantics=("arbitrary", "arbitrary"),
                                             vmem_limit_bytes=VMEM_LIMIT),
        name="swa_attn",
    )(sinks, q, k2, k2, v2, v2)


def _ssd_kernel(xs_ref, bc_ref, dt_ref, a_ref, dskip_ref, tri_ref,
                y_ref, hT_ref, h_s, *, tc):
    l = pl.program_id(1)
    n_l = pl.num_programs(1)
    gw = SSM_WIDTH // N_SSM_GROUPS
    hpg = N_SSM_HEADS // N_SSM_GROUPS

    @pl.when(l == 0)
    def _():
        h_s[...] = jnp.zeros(h_s.shape, F32)

    lane = lax.broadcasted_iota(jnp.int32, (SSD_CHUNK, LANES), 1)
    low = lane < SSM_HEAD_DIM
    qi = lax.broadcasted_iota(jnp.int32, (SSD_CHUNK, SSD_CHUNK), 0)
    sj = lax.broadcasted_iota(jnp.int32, (SSD_CHUNK, SSD_CHUNK), 1)
    causal = sj <= qi
    a_neg = -jnp.exp(a_ref[...])
    tri = tri_ref[...]
    zero_bf = jnp.zeros((SSD_CHUNK, LANES), BF16)

    for c in range(tc // SSD_CHUNK):
        rows = slice(c * SSD_CHUNK, (c + 1) * SSD_CHUNK)
        dt = dt_ref[0, rows, :]
        hi, mid, lo = _split3(dt * a_neg)
        acum = (_dot(tri, hi) + _dot(tri, mid) + _dot(tri, lo)) * LOG2E
        acum_t = acum.T
        dt_t = dt.T
        a_end_t = acum_t[:, SSD_CHUNK - 1:SSD_CHUNK]
        w_end_t = dt_t * jnp.exp2(a_end_t - acum_t)
        dec_t = jnp.exp2(a_end_t)
        dec_t = jnp.broadcast_to(dec_t, (LANES, LANES))
        row_t = acum_t - jnp.log2(dt_t)
        for g in range(N_SSM_GROUPS):
            b_g = bc_ref[0, rows, g * D_STATE:(g + 1) * D_STATE]
            c_g = bc_ref[0, rows, (N_SSM_GROUPS + g) * D_STATE:(N_SSM_GROUPS + g + 1) * D_STATE]
            cb = _dot_nt(c_g, b_g)
            b_gt = b_g.astype(F32).T
            y_inter = _dot(c_g, h_s[g].astype(BF16))
            for pr in range(hpg // 2):
                lhs_y, lhs_s, escale, dsc = [], [], [], []
                for e in range(2):
                    h = g * hpg + 2 * pr + e
                    col = jnp.broadcast_to(acum[:, h:h + 1], (SSD_CHUNK, SSD_CHUNK))
                    decay_dt = jnp.where(causal, jnp.exp2(col - row_t[h:h + 1, :]), 0.0)
                    lhs_y.append((cb * decay_dt).astype(BF16))
                    lhs_s.append((b_gt * w_end_t[h:h + 1, :]).astype(BF16))
                    escale.append(jnp.exp2(col))
                    dsc.append(dec_t[h:h + 1, :])
                c0 = g * gw + pr * LANES
                x_pair = xs_ref[0, rows, c0:c0 + LANES]
                x_bf = x_pair.astype(BF16)
                rhs = jnp.concatenate([jnp.where(low, x_bf, zero_bf), jnp.where(low, zero_bf, x_bf)], axis=0)
                y_in = _dot(jnp.concatenate(lhs_y, axis=1), rhs)
                st = _dot(jnp.concatenate(lhs_s, axis=1), rhs)
                lc = pr * LANES
                y = y_in + y_inter[:, lc:lc + LANES] * jnp.where(low, escale[0], escale[1])
                y_ref[0, rows, c0:c0 + LANES] = y + dskip_ref[:, c0:c0 + LANES] * x_pair
                h_s[g, :, lc:lc + LANES] = h_s[g, :, lc:lc + LANES] * jnp.where(low, dsc[0], dsc[1]) + st

    @pl.when(l == n_l - 1)
    def _():
        hT_ref[0] = h_s[...]


def _ssd(xs, bc, dt, a_pad, dskip, tri, *, tc):
    nb, seq, _ = xs.shape
    row = lambda w: pl.BlockSpec((1, tc, w), lambda b, l: (b, l, 0))
    const = lambda shape: pl.BlockSpec(shape, lambda b, l: (0,) * len(shape))
    gw = SSM_WIDTH // N_SSM_GROUPS
    return pl.pallas_call(
        functools.partial(_ssd_kernel, tc=tc),
        grid=(nb, seq // tc),
        in_specs=[row(SSM_WIDTH), row(CONV_DIM - SSM_WIDTH), row(LANES),
                  const((1, LANES)), const((1, SSM_WIDTH)), const((SSD_CHUNK, SSD_CHUNK))],
        out_specs=[row(SSM_WIDTH),
                   pl.BlockSpec((1, N_SSM_GROUPS, D_STATE, gw), lambda b, l: (b, 0, 0, 0))],
        out_shape=[jax.ShapeDtypeStruct((nb, seq, SSM_WIDTH), F32),
                   jax.ShapeDtypeStruct((nb, N_SSM_GROUPS, D_STATE, gw), F32)],
        scratch_shapes=[pltpu.VMEM((N_SSM_GROUPS, D_STATE, gw), F32)],
        compiler_params=pltpu.CompilerParams(dimension_semantics=("arbitrary", "arbitrary"),
                                             vmem_limit_bytes=VMEM_LIMIT),
        name="ssd_scan",
    )(xs, bc, dt, a_pad, dskip, tri)


def _outffn_kernel(*refs, ffc, gated):
    if gated:
        x_ref, oa_ref, ys_ref, z_ref, gnorm_ref, wo_ref, gffn_ref, wgu_ref, wd_ref, o_ref = refs
        gw = SSM_WIDTH // N_SSM_GROUPS
        parts = []
        for g in range(N_SSM_GROUPS):
            ls = slice(g * gw, (g + 1) * gw)
            yg = ys_ref[:, ls] * _silu(z_ref[:, ls])
            yg = yg * lax.rsqrt(jnp.mean(yg * yg, axis=-1, keepdims=True) + EPS)
            parts.append((yg * gnorm_ref[:, ls]).astype(BF16))
        ys = jnp.concatenate(parts, axis=1)
    else:
        x_ref, oa_ref, ys_ref, wo_ref, gffn_ref, wgu_ref, wd_ref, o_ref = refs
        ys = ys_ref[...]
    x = x_ref[...]
    xm = x + _dot(oa_ref[...], wo_ref[0, 0:ATTN_WIDTH, :]) + _dot(ys, wo_ref[0, ATTN_WIDTH:, :])
    hn = _rms_rows(xm, gffn_ref[...]).astype(BF16)
    acc = jnp.zeros_like(xm)
    c0 = 0
    for width in ffc:
        g = _dot(hn, wgu_ref[0, :, c0:c0 + width])
        up = _dot(hn, wgu_ref[0, :, D_FF + c0:D_FF + c0 + width])
        acc = acc + _dot((_silu(g) * up).astype(BF16), wd_ref[0, c0:c0 + width, :])
        c0 += width
    o_ref[...] = xm + acc


def _outffn(x, oa, ys, gate, wo, gffn, wgu, wd, *, layer, tm, ffc):
    rows = x.shape[0]
    row = lambda w: pl.BlockSpec((tm, w), lambda i: (i, 0))
    const = lambda shape: pl.BlockSpec(shape, lambda i: (0,) * len(shape), pipeline_mode=pl.Buffered(1))
    gated = gate is not None
    in_specs = [row(D_MODEL), row(ATTN_WIDTH), row(SSM_WIDTH)]
    args = [x, oa, ys]
    if gated:
        in_specs += [row(SSM_WIDTH), const((1, SSM_WIDTH))]
        args += list(gate)
    wblk = lambda r, w: pl.BlockSpec((1, r, w), lambda i: (layer, 0, 0), pipeline_mode=pl.Buffered(1))
    in_specs += [wblk(ATTN_WIDTH + SSM_WIDTH, D_MODEL), const((1, D_MODEL)),
                 wblk(D_MODEL, 2 * D_FF), wblk(D_FF, D_MODEL)]
    args += [wo, gffn, wgu, wd]
    return pl.pallas_call(
        functools.partial(_outffn_kernel, ffc=ffc, gated=gated),
        grid=(rows // tm,),
        in_specs=in_specs,
        out_specs=row(D_MODEL),
        out_shape=jax.ShapeDtypeStruct((rows, D_MODEL), F32),
        compiler_params=pltpu.CompilerParams(dimension_semantics=("arbitrary",),
                                             vmem_limit_bytes=VMEM_LIMIT),
        name="outproj_ffn",
    )(*args)


T_PAD = SUBLANES


def _attn_dec_kernel(q_ref, kc_ref, kn_ref, vc_ref, vn_ref, sink_ref, o_ref, wk_ref, wv_ref, *, t_new):
    q = q_ref[...]
    bb, nq, _ = q.shape
    qpk = N_Q_HEADS // N_KV_HEADS
    n_c = kc_ref.shape[2]
    kc, vc, kn, vn = kc_ref[0], vc_ref[0], kn_ref[...], vn_ref[...]
    s_c = jnp.einsum('bqd,bkd->bqk', q, kc.astype(BF16), preferred_element_type=F32)
    s_n = jnp.einsum('bqd,bkd->bqk', q, kn.astype(BF16), preferred_element_type=F32)
    t_c = (lax.broadcasted_iota(jnp.int32, (bb, nq, n_c), 1) % (t_new * qpk)) // qpk
    j_c = lax.broadcasted_iota(jnp.int32, (bb, nq, n_c), 2)
    s_c = jnp.where(j_c > t_c + (n_c - WINDOW), s_c, -jnp.inf)
    t_n = (lax.broadcasted_iota(jnp.int32, (bb, nq, T_PAD), 1) % (t_new * qpk)) // qpk
    j_n = lax.broadcasted_iota(jnp.int32, (bb, nq, T_PAD), 2)
    s_n = jnp.where((j_n <= t_n) & (j_n < t_new), s_n, -jnp.inf)
    sk = sink_ref[...]
    m = jnp.maximum(jnp.maximum(jnp.max(s_c, axis=-1, keepdims=True), jnp.max(s_n, axis=-1, keepdims=True)), sk)
    p_c = jnp.exp(s_c - m)
    p_n = jnp.exp(s_n - m)
    den = jnp.sum(p_c, axis=-1, keepdims=True) + jnp.sum(p_n, axis=-1, keepdims=True) + jnp.exp(sk - m)
    o = (jnp.einsum('bqk,bkd->bqd', p_c.astype(BF16), vc.astype(BF16), preferred_element_type=F32)
         + jnp.einsum('bqk,bkd->bqd', p_n.astype(BF16), vn.astype(BF16), preferred_element_type=F32))
    o = o / den
    rows_g = nq // N_KV_HEADS
    lane = lax.broadcasted_iota(jnp.int32, (bb, rows_g, LANES), 2)
    merged = o[:, 0:rows_g, :]
    for g in range(1, N_KV_HEADS):
        merged = jnp.where(lane // HEAD_DIM == g, o[:, g * rows_g:(g + 1) * rows_g, :], merged)
    o_ref[...] = merged.astype(BF16)

    sub = lax.broadcasted_iota(jnp.int32, (bb, T_PAD, LANES), 1)
    for c, n, w_ref in ((kc, kn, wk_ref), (vc, vn, wv_ref)):
        shifted = pltpu.roll(c.reshape(bb * n_c, LANES), bb * n_c - t_new, 0).reshape(bb, n_c, LANES)
        tail = pltpu.roll(n.reshape(bb * T_PAD, LANES), T_PAD - t_new, 0).reshape(bb, T_PAD, LANES)
        w_ref[:, 0:n_c - T_PAD, :] = shifted[:, 0:n_c - T_PAD, :]
        w_ref[:, n_c - T_PAD:, :] = jnp.where(sub < T_PAD - t_new, shifted[:, n_c - T_PAD:, :], tail)


def _attention_dec(q, kc, kn, vc, vn, sink_rows, *, layer, t_new, bb):
    n_seq, nq, w = q.shape
    n_c = kc.shape[2]
    blk = lambda r: pl.BlockSpec((bb, r, w), lambda i: (i, 0, 0))
    cache = pl.BlockSpec((1, bb, n_c, w), lambda i: (layer, i, 0, 0))
    win = jax.ShapeDtypeStruct((n_seq, n_c, w), F32)
    return pl.pallas_call(
        functools.partial(_attn_dec_kernel, t_new=t_new),
        grid=(n_seq // bb,),
        in_specs=[blk(nq), cache, blk(T_PAD), cache, blk(T_PAD), pl.BlockSpec((nq, 1), lambda i: (0, 0))],
        out_specs=[blk(nq // N_KV_HEADS), blk(n_c), blk(n_c)],
        out_shape=[jax.ShapeDtypeStruct((n_seq, nq // N_KV_HEADS, w), BF16), win, win],
        compiler_params=pltpu.CompilerParams(dimension_semantics=("arbitrary",), vmem_limit_bytes=VMEM_LIMIT),
        name="swa_attn_decode",
    )(q, kc, kn, vc, vn, sink_rows)


def _ssd_dec_kernel(*refs, bb, t_new, n_prev):
    (xs_ref, bc_ref, dt_ref, z_ref, h0_ref) = refs[:5]
    prev_refs = refs[5:5 + n_prev]
    (a_ref, dskip_ref, gnorm_ref, tri_ref, ones_ref, exp_ref,
     y_ref, h_ref, aw_s, xdt_s, xw_s, ea_s, dec_s) = refs[5 + n_prev:]
    d = pl.program_id(0)
    gw = SSM_WIDTH // N_SSM_GROUPS
    hpg = N_SSM_HEADS // N_SSM_GROUPS
    rows = bb * T_PAD

    for j in range(n_prev):
        @pl.when(d == j)
        def _(j=j):
            h_ref[0] = prev_refs[j][0]

    @pl.when(d == n_prev)
    def _():
        expand = exp_ref[...]

        def widen(v):
            hi, mid, lo = _split3(v)
            return _dot(hi, expand) + _dot(mid, expand) + _dot(lo, expand)

        def seq_sum(m, v):
            hi, mid, lo = _split3(v)
            return _dot(m, hi) + _dot(m, mid) + _dot(m, lo)

        dt = dt_ref[...].reshape(rows, LANES)
        dta = dt * (-jnp.exp(a_ref[...]))
        a_w = widen(seq_sum(tri_ref[...], dta))
        a_end = seq_sum(ones_ref[...], dta)
        xdt = xs_ref[...].reshape(rows, SSM_WIDTH) * widen(dt)
        aw_s[...] = a_w
        xdt_s[...] = xdt
        xw_s[...] = xdt * jnp.exp(widen(a_end) - a_w)
        ea_s[...] = jnp.exp(a_w)
        dec_s[...] = jnp.exp(a_end)

        t_row = lax.broadcasted_iota(jnp.int32, (T_PAD, gw), 0)

        def seq_body(i, carry):
            r0 = pl.multiple_of(i * T_PAD, T_PAD)
            rs = pl.ds(r0, T_PAD)
            dec = jnp.broadcast_to(dec_s[pl.ds(r0, 1), :], (LANES, LANES)).T
            bc = bc_ref[i]
            for g in range(N_SSM_GROUPS):
                b_g = bc[:, g * D_STATE:(g + 1) * D_STATE]
                c_g = bc[:, (N_SSM_GROUPS + g) * D_STATE:(N_SSM_GROUPS + g + 1) * D_STATE]
                cb = _dot_nt(c_g, b_g)
                ls = slice(g * gw, (g + 1) * gw)
                a_g = aw_s[rs, ls]
                xdt_g = xdt_s[rs, ls]
                y = jnp.zeros((T_PAD, gw), F32)
                for s in range(t_new):
                    w = jnp.where(t_row >= s, jnp.exp(a_g - a_g[s:s + 1, :]), 0.0)
                    y = y + (w * cb[:, s:s + 1]) * xdt_g[s:s + 1, :]
                h0g = h0_ref[0, i, g * hpg:(g + 1) * hpg].reshape(gw, D_STATE)
                y = y + _dot_nt(c_g, h0g.astype(BF16)) * ea_s[rs, ls]
                y = y + dskip_ref[:, ls] * xs_ref[i, :, ls]
                y = y * _silu(z_ref[i, :, ls])
                y = y * lax.rsqrt(jnp.mean(y * y, axis=-1, keepdims=True) + EPS)
                y_ref[i, :, ls] = (y * gnorm_ref[:, ls]).astype(BF16)
                delta = lax.dot_general(xw_s[rs, ls].astype(BF16), b_g, (((0,), (0,)), ((), ())),
                                        preferred_element_type=F32)
                for hl in range(hpg):
                    h = g * hpg + hl
                    h_ref[0, i, h] = (h0_ref[0, i, h] * dec[h:h + 1, :]
                                      + delta[hl * SSM_HEAD_DIM:(hl + 1) * SSM_HEAD_DIM, :])
            return carry

        lax.fori_loop(0, bb, seq_body, 0)


def _ssd_dec(xs3, bc3, dt3, z3, state_all, layer, prev_states, a_pad, dskip, gnorm, tri_b, ones_b, expand,
             *, bb, t_new):
    nb = xs3.shape[0]
    nblk = nb // bb
    n_prev = len(prev_states)
    rows = bb * T_PAD

    def phase_block(j):
        return lambda d, i: jnp.clip(i + (d - j) * nblk, 0, nblk - 1)

    cur = phase_block(n_prev)
    blk = lambda w: pl.BlockSpec((bb, T_PAD, w), lambda d, i: (cur(d, i), 0, 0))
    const = lambda shape: pl.BlockSpec(shape, lambda d, i: (0,) * len(shape))
    st_shape = (1, bb, N_SSM_HEADS, SSM_HEAD_DIM, D_STATE)
    in_specs = [blk(SSM_WIDTH), blk(CONV_DIM - SSM_WIDTH), blk(LANES), blk(SSM_WIDTH),
                pl.BlockSpec(st_shape, lambda d, i: (layer, cur(d, i), 0, 0, 0))]
    for j in range(n_prev):
        in_specs.append(pl.BlockSpec(st_shape, lambda d, i, j=j: (0, phase_block(j)(d, i), 0, 0, 0)))
    in_specs += [const((1, LANES)), const((1, SSM_WIDTH)), const((1, SSM_WIDTH)),
                 const((rows, rows)), const((rows, rows)), const((LANES, SSM_WIDTH))]
    return pl.pallas_call(
        functools.partial(_ssd_dec_kernel, bb=bb, t_new=t_new, n_prev=n_prev),
        grid=(n_prev + 1, nblk),
        in_specs=in_specs,
        out_specs=[blk(SSM_WIDTH), pl.BlockSpec(st_shape, lambda d, i: (d, i, 0, 0, 0))],
        out_shape=[jax.ShapeDtypeStruct((nb, T_PAD, SSM_WIDTH), BF16),
                   jax.ShapeDtypeStruct((n_prev + 1,) + state_all.shape[1:], F32)],
        scratch_shapes=[pltpu.VMEM((rows, SSM_WIDTH), F32), pltpu.VMEM((rows, SSM_WIDTH), F32),
                        pltpu.VMEM((rows, SSM_WIDTH), F32), pltpu.VMEM((rows, SSM_WIDTH), F32),
                        pltpu.VMEM((rows, LANES), F32)],
        compiler_params=pltpu.CompilerParams(dimension_semantics=("arbitrary", "arbitrary"),
                                             vmem_limit_bytes=VMEM_LIMIT),
        name="ssd_decode",
    )(xs3, bc3, dt3, z3, state_all, *prev_states, a_pad, dskip, gnorm, tri_b, ones_b, expand)


def _rope_tables(pos):
    half = HEAD_DIM // 2
    inv = ROPE_THETA ** (-np.arange(half, dtype=np.float64) / half)
    ang = np.asarray(pos, np.float64)[:, None] * inv[None, :]
    cos = np.tile(np.cos(ang), (1, LANES // half))
    sin = np.sin(ang)
    sin_signed = np.tile(np.concatenate([-sin, sin], axis=1), (1, LANES // HEAD_DIM))
    return jnp.asarray(cos, F32), jnp.asarray(sin_signed, F32)


def _layer_consts(l, p):
    return dict(
        layer=l,
        gmix=p['norm_mix'][l][None, :],
        qn=jnp.tile(p['q_norm'][l], LANES // HEAD_DIM)[None, :],
        kn=jnp.tile(p['k_norm'][l], LANES // HEAD_DIM)[None, :],
        convw=p['conv_w'][l],
        convb=p['conv_b'][l][None, :],
        dtb=jnp.pad(p['dt_bias'][l], (0, LANES - N_SSM_HEADS))[None, :],
        a_pad=jnp.pad(p['a_log'][l], (0, LANES - N_SSM_HEADS))[None, :],
        dskip=jnp.repeat(p['d_skip'][l], SSM_HEAD_DIM)[None, :],
        gnorm=p['ssm_norm'][l][None, :],
        sinks=p['sinks'][l],
        gffn=p['norm_ffn'][l][None, :],
    )


def _state_from_t(h_t):
    nb = h_t.shape[0]
    hpg = N_SSM_HEADS // N_SSM_GROUPS
    h = h_t.reshape(nb, N_SSM_GROUPS, D_STATE, hpg, SSM_HEAD_DIM)
    return jnp.transpose(h, (0, 1, 3, 4, 2)).reshape(nb, N_SSM_HEADS, SSM_HEAD_DIM, D_STATE)


def _prompt_layer(x, c, shared, *, tm):
    nb, seq, _ = x.shape
    (q, k2, v2, z, xs, bc, dt, knew, vnew, convst) = _inproj(
        x, c['gmix'], shared['win'], shared['m128'], c['qn'], c['kn'], shared['cos_p'], shared['sin_p'],
        c['convw'], c['convb'], c['dtb'], None, layer=c['layer'], tm=tm, carry_rows=SUBLANES, shift=1, n_keep=min(WINDOW, seq))
    oa = _attention(c['sinks'], q, k2, v2, tq=tm)
    ys, h_t = _ssd(xs, bc, dt, c['a_pad'], c['dskip'], shared['tri'], tc=tm)
    rows = nb * seq
    xo = _outffn(x.reshape(rows, D_MODEL), oa.reshape(rows, ATTN_WIDTH), ys.reshape(rows, SSM_WIDTH),
                 (z.reshape(rows, SSM_WIDTH), c['gnorm']), shared['wo'], c['gffn'], shared['wgu'], shared['wd'],
                 layer=c['layer'], tm=tm,
                 ffc=FF_CHUNKS)
    n_keep = knew.shape[1]
    return (xo.reshape(nb, seq, D_MODEL),
            knew.reshape(nb, n_keep, N_KV_HEADS, HEAD_DIM), vnew.reshape(nb, n_keep, N_KV_HEADS, HEAD_DIM),
            convst[:, SUBLANES - (CONV_W - 1):, :], _state_from_t(h_t))


def _sample_layer(x_t, cache_k, cache_v, conv0, state_all, layer, prev_states, c, shared, *, n_seq, t_new):
    rows = t_new * n_seq
    qpk = N_Q_HEADS // N_KV_HEADS
    n_c = cache_k.shape[2]
    conv0_t = jnp.transpose(conv0, (1, 0, 2)).reshape((CONV_W - 1) * n_seq, CONV_DIM)
    (q, _, _, z, xs, bc, dt, knew, vnew, convst) = _inproj(
        x_t, c['gmix'], shared['win'], shared['m128'], c['qn'], c['kn'], shared['cos_s'], shared['sin_s'],
        c['convw'], c['convb'], c['dtb'], conv0_t, layer=c['layer'], tm=rows, carry_rows=(CONV_W - 1) * n_seq, shift=n_seq,
        n_keep=rows)

    def to_seq(a):
        a = jnp.transpose(a[0].reshape(t_new, n_seq, a.shape[-1]), (1, 0, 2))
        return jnp.pad(a, ((0, 0), (0, T_PAD - t_new), (0, 0)))

    q5 = jnp.transpose(q[0].reshape(t_new, n_seq, N_KV_HEADS, qpk, HEAD_DIM), (1, 2, 0, 3, 4))
    q5 = q5.reshape(n_seq, N_KV_HEADS, t_new * qpk, HEAD_DIM)
    q_m = jnp.concatenate(
        [jnp.pad(q5[:, g], ((0, 0), (0, 0), (g * HEAD_DIM, (N_KV_HEADS - 1 - g) * HEAD_DIM)))
         for g in range(N_KV_HEADS)], axis=1)
    sink_rows = jnp.tile(c['sinks'].reshape(N_KV_HEADS, 1, qpk), (1, t_new, 1)).reshape(-1, 1)
    o_m, win_k, win_v = _attention_dec(
        q_m, cache_k, to_seq(knew), cache_v, to_seq(vnew), sink_rows, layer=layer, t_new=t_new, bb=min(32, n_seq))
    o5 = o_m.reshape(n_seq, t_new, qpk, N_KV_HEADS, HEAD_DIM)
    oa = jnp.transpose(o5, (1, 0, 3, 2, 4)).reshape(rows, ATTN_WIDTH)
    win_k = win_k.reshape(n_seq, n_c, N_KV_HEADS, HEAD_DIM)
    win_v = win_v.reshape(n_seq, n_c, N_KV_HEADS, HEAD_DIM)

    y3, h_new = _ssd_dec(to_seq(xs), to_seq(bc), to_seq(dt), to_seq(z), state_all, layer, prev_states,
                         c['a_pad'], c['dskip'], c['gnorm'], shared['tri_b'], shared['ones_b'], shared['expand'],
                         bb=shared['bb_s'], t_new=t_new)
    ys = jnp.transpose(y3[:, :t_new], (1, 0, 2)).reshape(rows, SSM_WIDTH)

    xo = _outffn(x_t[0], oa, ys, None, shared['wo'], c['gffn'], shared['wgu'], shared['wd'], layer=c['layer'],
                 tm=min(512, rows), ffc=FF_CHUNKS)

    conv_new = jnp.transpose(convst[0].reshape(CONV_W - 1, n_seq, CONV_DIM), (1, 0, 2))
    return xo[None], win_k, win_v, conv_new, h_new


def kernel(x_prompt, x_sample, cache_win_k, cache_win_v, state_conv, state_ssm,
           norm_mix, w_in, q_norm, k_norm, sinks, conv_w, conv_b, dt_bias, a_log,
           d_skip, ssm_norm, w_out, norm_ffn, w_gate_up, w_down):
    p = dict(norm_mix=norm_mix, w_in=w_in, q_norm=q_norm, k_norm=k_norm, sinks=sinks, conv_w=conv_w,
             conv_b=conv_b, dt_bias=dt_bias, a_log=a_log, d_skip=d_skip, ssm_norm=ssm_norm, w_out=w_out,
             norm_ffn=norm_ffn, w_gate_up=w_gate_up, w_down=w_down)
    depth = w_in.shape[0]
    seq = x_prompt.shape[1]
    cos_p, sin_p = _rope_tables(np.arange(seq))
    half_blk = jnp.arange(LANES) // HEAD_DIM
    m128 = (jnp.where(half_blk[:, None] == half_blk[None, :], 1.0 / HEAD_DIM, 0.0)).astype(BF16)
    m128 = jnp.concatenate([m128, m128], axis=0)
    tri = (jnp.arange(SSD_CHUNK)[:, None] >= jnp.arange(SSD_CHUNK)[None, :]).astype(BF16)
    n_seq, t_new, _ = x_sample.shape
    pos_s = PAST_LEN + np.repeat(np.arange(t_new), n_seq)
    cos_s, sin_s = _rope_tables(pos_s)
    bb_s = min(8, n_seq)
    r_idx = jnp.arange(bb_s * T_PAD)
    same_seq = (r_idx[:, None] // T_PAD) == (r_idx[None, :] // T_PAD)
    tri_b = (same_seq & (r_idx[:, None] >= r_idx[None, :])).astype(BF16)
    ones_b = same_seq.astype(BF16)
    expand = (jnp.arange(LANES)[:, None] == (jnp.arange(SSM_WIDTH) // SSM_HEAD_DIM)[None, :]).astype(BF16)
    weights = dict(win=jnp.pad(w_in, ((0, 0), (0, 0), (0, PROJ_PAD - IN_PROJ_WIDTH))).astype(BF16),
                   wo=w_out.astype(BF16), wgu=w_gate_up.astype(BF16), wd=w_down.astype(BF16))
    shared = dict(cos_p=cos_p, sin_p=sin_p, cos_s=cos_s, sin_s=sin_s, m128=m128, tri=tri, tri_b=tri_b, **weights,
                  ones_b=ones_b, bb_s=bb_s, expand=expand)
    tm = min(512, seq)

    cache_k = cache_win_k.reshape(cache_win_k.shape[:3] + (KV_WIDTH,))
    cache_v = cache_win_v.reshape(cache_win_v.shape[:3] + (KV_WIDTH,))
    xp = x_prompt
    xs = jnp.transpose(x_sample, (1, 0, 2)).reshape(1, t_new * n_seq, D_MODEL)
    pk, pv, pc, ph = [], [], [], []
    sk, sv, sc, sh = [], [], [], []
    for l in range(depth):
        c = _layer_consts(l, p)
        xp, k1, v1, c1, h1 = _prompt_layer(xp, c, shared, tm=tm)
        prev = sh if l == depth - 1 else []
        xs, k2, v2, c2, h2 = _sample_layer(xs, cache_k, cache_v, state_conv[l], state_ssm, l, prev,
                                           c, shared, n_seq=n_seq, t_new=t_new)
        pk.append(k1); pv.append(v1); pc.append(c1); ph.append(h1)
        sk.append(k2); sv.append(v2); sc.append(c2); sh.append(h2)
    ys = jnp.transpose(xs.reshape(t_new, n_seq, D_MODEL), (1, 0, 2))
    return (xp, ys, jnp.stack(pk), jnp.stack(pv), jnp.stack(pc), jnp.stack(ph),
            jnp.stack(sk), jnp.stack(sv), jnp.stack(sc), sh[-1])
```

```python
import functools
import math

import numpy as np
import jax
import jax.numpy as jnp
from jax import lax
from jax.experimental import pallas as pl
from jax.experimental.pallas import tpu as pltpu

F32 = jnp.float32
BF16 = jnp.bfloat16

D_MODEL = 1024
HEAD_DIM = 64
N_Q_HEADS = 8
N_KV_HEADS = 2
ATTN_WIDTH = N_Q_HEADS * HEAD_DIM
KV_WIDTH = N_KV_HEADS * HEAD_DIM
WINDOW = 128
ROPE_THETA = 10000.0
ATTN_SCALE = HEAD_DIM ** -0.5
SSM_HEAD_DIM = 64
N_SSM_HEADS = 16
SSM_WIDTH = N_SSM_HEADS * SSM_HEAD_DIM
N_SSM_GROUPS = 2
D_STATE = 128
CONV_W = 4
CONV_DIM = SSM_WIDTH + 2 * N_SSM_GROUPS * D_STATE
SSD_CHUNK = 128
D_FF = 2816
EPS = 1e-6
LOG2E = math.log2(math.e)
PAST_LEN = 16384

LANES = 128
SUBLANES = 8

COL_Q = 0
COL_K = COL_Q + ATTN_WIDTH
COL_V = COL_K + KV_WIDTH
COL_Z = COL_V + KV_WIDTH
COL_XBC = COL_Z + SSM_WIDTH
COL_DT = COL_XBC + CONV_DIM
IN_PROJ_WIDTH = COL_DT + N_SSM_HEADS
PROJ_PAD = COL_DT + LANES
QKV_COLS = COL_Z

VMEM_LIMIT = 56 * 1024 * 1024
FF_CHUNKS = (6 * 256, 5 * 256)


def _dot(a, b):
    return jnp.dot(a, b, preferred_element_type=F32)


def _dot_nt(a, b):
    return lax.dot_general(a, b, (((1,), (1,)), ((), ())), preferred_element_type=F32)


def _silu(x):
    return x / (1.0 + jnp.exp(-x))


def _softplus(x):
    return jnp.maximum(x, 0.0) + jnp.log1p(jnp.exp(-jnp.abs(x)))


def _rms_rows(x, g):
    return x * lax.rsqrt(jnp.mean(x * x, axis=-1, keepdims=True) + EPS) * g


def _split2(x):
    hi = x.astype(BF16)
    lo = (x - hi.astype(F32)).astype(BF16)
    return hi, lo


def _split3(x):
    hi = x.astype(BF16)
    r1 = x - hi.astype(F32)
    mid = r1.astype(BF16)
    lo = (r1 - mid.astype(F32)).astype(BF16)
    return hi, mid, lo


def _head_norm_rope(x, m128, g, cos, sin_signed, low_half):
    hi, lo = _split2(x * x)
    ms = _dot(jnp.concatenate([hi, lo], axis=1), m128)
    xn = x * lax.rsqrt(ms + EPS) * g
    rot = jnp.where(low_half, pltpu.roll(xn, LANES - HEAD_DIM // 2, 1),
                    pltpu.roll(xn, HEAD_DIM // 2, 1))
    return xn * cos + rot * sin_signed


def _inproj_kernel(*refs, tm, carry_rows, shift, n_keep, has_state, n_parts):
    if has_state:
        (x_ref, gmix_ref, win_ref, m128_ref, qn_ref, kn_ref, cos_ref, sin_ref,
         convw_ref, convb_ref, dtb_ref, conv0_ref,
         q_ref, k2_ref, v2_ref, z_ref, xs_ref, bc_ref, dt_ref, knew_ref, vnew_ref, convst_ref,
         xbc_s) = refs
    else:
        (x_ref, gmix_ref, win_ref, m128_ref, qn_ref, kn_ref, cos_ref, sin_ref,
         convw_ref, convb_ref, dtb_ref,
         q_ref, k2_ref, v2_ref, z_ref, xs_ref, bc_ref, dt_ref, knew_ref, vnew_ref, convst_ref,
         xbc_s) = refs
        conv0_ref = None
    l = pl.program_id(1)

    @pl.when(l == 0)
    def _():
        if has_state:
            xbc_s[0:carry_rows, :] = conv0_ref[...]
        else:
            xbc_s[0:carry_rows, :] = jnp.zeros((carry_rows, CONV_DIM), F32)

    m128 = m128_ref[...]
    cw = 512
    nr = tm // n_parts
    lane = lax.broadcasted_iota(jnp.int32, (nr, LANES), 1)
    low_half = (lane % HEAD_DIM) < (HEAD_DIM // 2)
    for part in range(n_parts):
        r0 = part * nr
        rs = slice(r0, r0 + nr)
        base = carry_rows + r0
        u = _rms_rows(x_ref[0, rs, :], gmix_ref[...]).astype(BF16)
        for j in range(CONV_DIM // cw):
            c0 = COL_XBC + j * cw
            xbc_s[base:base + nr, j * cw:(j + 1) * cw] = _dot(u, win_ref[0, :, c0:c0 + cw])
        z_ref[0, rs, :] = _dot(u, win_ref[0, :, COL_Z:COL_Z + SSM_WIDTH])
        heads = _dot(u, win_ref[0, :, 0:QKV_COLS])
        dt_raw = _dot(u, win_ref[0, :, COL_DT:COL_DT + LANES])
        cos = cos_ref[rs, :]
        sin = sin_ref[rs, :]

        for j in range(ATTN_WIDTH // LANES):
            qj = heads[:, COL_Q + j * LANES:COL_Q + (j + 1) * LANES]
            qj = _head_norm_rope(qj, m128, qn_ref[...], cos, sin, low_half)
            q_ref[0, rs, j * LANES:(j + 1) * LANES] = (qj * ATTN_SCALE).astype(BF16)

        k = _head_norm_rope(heads[:, COL_K:COL_K + KV_WIDTH], m128, kn_ref[...], cos, sin, low_half)
        v = heads[:, COL_V:COL_V + KV_WIDTH]
        k2_ref[0, rs, 0:LANES] = k.astype(BF16)
        k2_ref[0, rs, LANES:2 * LANES] = pltpu.roll(k, HEAD_DIM, 1).astype(BF16)
        v2_ref[0, rs, 0:LANES] = v.astype(BF16)
        v2_ref[0, rs, LANES:2 * LANES] = pltpu.roll(v, HEAD_DIM, 1).astype(BF16)
        if part == n_parts - 1:
            knew_ref[0] = k[nr - n_keep:, :]
            vnew_ref[0] = v[nr - n_keep:, :]

        dt_ref[0, rs, :] = _softplus(dt_raw + dtb_ref[...])

        for j in range(CONV_DIM // cw):
            cs = slice(j * cw, (j + 1) * cw)
            w = [convw_ref[i:i + 1, cs] for i in range(CONV_W)]
            bias = convb_ref[:, cs]
            if shift % SUBLANES == 0:
                acc = bias
                for i in range(CONV_W):
                    off = base - (CONV_W - 1 - i) * shift
                    acc = acc + xbc_s[off:off + nr, cs] * w[i]
                act = _silu(acc)
            else:
                cur = xbc_s[base:base + nr, cs]
                acc = cur * w[0]
                for i in range(1, CONV_W):
                    acc = pltpu.roll(acc, shift, 0) + cur * w[i]
                head = bias
                for i in range(CONV_W):
                    off = base - (CONV_W - 1 - i) * shift
                    head = head + xbc_s[off:off + SUBLANES, cs] * w[i]
                act = _silu(jnp.concatenate([head, acc[SUBLANES:, :] + bias], axis=0))
            if j * cw < SSM_WIDTH:
                xs_ref[0, rs, cs] = act
            else:
                bc_ref[0, rs, :] = act.astype(BF16)

    convst_ref[0] = xbc_s[tm:tm + carry_rows, :]
    xbc_s[0:carry_rows, :] = xbc_s[tm:tm + carry_rows, :]


def _inproj(x, gmix, win, m128, qn, kn, cos, sin, convw, convb, dtb, conv0, *, layer, tm, carry_rows, shift,
            n_keep):
    nb, seq, _ = x.shape
    n_l = seq // tm
    n_parts = 1
    has_state = conv0 is not None
    const = lambda shape: pl.BlockSpec(shape, lambda b, l: (0,) * len(shape))
    row = lambda w: pl.BlockSpec((1, tm, w), lambda b, l: (b, l, 0))
    in_specs = [row(D_MODEL), const((1, D_MODEL)),
                pl.BlockSpec((1, D_MODEL, PROJ_PAD), lambda b, l: (layer, 0, 0)), const((2 * LANES, LANES)),
                const((1, LANES)), const((1, LANES)),
                pl.BlockSpec((tm, LANES), lambda b, l: (l, 0)), pl.BlockSpec((tm, LANES), lambda b, l: (l, 0)),
                const((CONV_W, CONV_DIM)), const((1, CONV_DIM)), const((1, LANES))]
    args = [x, gmix, win, m128, qn, kn, cos, sin, convw, convb, dtb]
    if has_state:
        in_specs.append(const((carry_rows, CONV_DIM)))
        args.append(conv0)
    last = lambda rows, w: pl.BlockSpec((1, rows, w), lambda b, l: (b, 0, 0))
    out_specs = [row(ATTN_WIDTH), row(2 * KV_WIDTH), row(2 * KV_WIDTH), row(SSM_WIDTH), row(SSM_WIDTH),
                 row(CONV_DIM - SSM_WIDTH), row(LANES),
                 last(n_keep, KV_WIDTH), last(n_keep, KV_WIDTH), last(carry_rows, CONV_DIM)]
    out_shape = [jax.ShapeDtypeStruct((nb, seq, ATTN_WIDTH), BF16),
                 jax.ShapeDtypeStruct((nb, seq, 2 * KV_WIDTH), BF16),
                 jax.ShapeDtypeStruct((nb, seq, 2 * KV_WIDTH), BF16),
                 jax.ShapeDtypeStruct((nb, seq, SSM_WIDTH), F32),
                 jax.ShapeDtypeStruct((nb, seq, SSM_WIDTH), F32),
                 jax.ShapeDtypeStruct((nb, seq, CONV_DIM - SSM_WIDTH), BF16),
                 jax.ShapeDtypeStruct((nb, seq, LANES), F32),
                 jax.ShapeDtypeStruct((nb, n_keep, KV_WIDTH), F32),
                 jax.ShapeDtypeStruct((nb, n_keep, KV_WIDTH), F32),
                 jax.ShapeDtypeStruct((nb, carry_rows, CONV_DIM), F32)]
    kern = functools.partial(_inproj_kernel, tm=tm, carry_rows=carry_rows, shift=shift, n_keep=n_keep,
                             has_state=has_state, n_parts=n_parts)
    return pl.pallas_call(
        kern, grid=(nb, n_l), in_specs=in_specs, out_specs=out_specs, out_shape=out_shape,
        scratch_shapes=[pltpu.VMEM((carry_rows + tm, CONV_DIM), F32)],
        compiler_params=pltpu.CompilerParams(dimension_semantics=("arbitrary", "arbitrary"),
                                             vmem_limit_bytes=VMEM_LIMIT),
        name="inproj",
    )(*args)


def _attn_kernel(sinks_ref, q_ref, kc_ref, kp_ref, vc_ref, vp_ref, o_ref, *, tq):
    i = pl.program_id(1)
    nblk = tq // WINDOW
    lane = lax.broadcasted_iota(jnp.int32, (WINDOW, LANES), 1)
    low = lane < HEAD_DIM
    qi = lax.broadcasted_iota(jnp.int32, (WINDOW, 2 * WINDOW), 0)
    kj = lax.broadcasted_iota(jnp.int32, (WINDOW, 2 * WINDOW), 1)
    diff = qi + WINDOW - kj
    band = (diff >= 0) & (diff < WINDOW)
    zero_bf = jnp.zeros((WINDOW, LANES), BF16)
    for blk in range(nblk):
        r0 = blk * WINDOW
        if blk == 0:
            k_prev, v_prev = kp_ref[0], vp_ref[0]
            mask = band & ((i > 0) | (kj >= WINDOW))
        else:
            k_prev = kc_ref[0, r0 - WINDOW:r0, :]
            v_prev = vc_ref[0, r0 - WINDOW:r0, :]
            mask = band
        kk = jnp.concatenate([k_prev, kc_ref[0, r0:r0 + WINDOW, :]], axis=0)
        vv = jnp.concatenate([v_prev, vc_ref[0, r0:r0 + WINDOW, :]], axis=0)
        for pair in range(N_Q_HEADS // 2):
            g = (2 * pair) // (N_Q_HEADS // N_KV_HEADS)
            qp = q_ref[0, r0:r0 + WINDOW, pair * LANES:(pair + 1) * LANES]
            outs = []
            for e in range(2):
                h = 2 * pair + e
                var = 0 if g == e else 1
                qh = jnp.where(low if e == 0 else ~low, qp, zero_bf)
                s = _dot_nt(qh, kk[:, var * LANES:(var + 1) * LANES])
                s = jnp.where(mask, s, -jnp.inf)
                sk = sinks_ref[h]
                m = jnp.maximum(jnp.max(s, axis=-1, keepdims=True), sk)
                p = jnp.exp(s - m)
                den = jnp.sum(p, axis=-1, keepdims=True) + jnp.exp(sk - m)
                o = _dot(p.astype(BF16), vv[:, var * LANES:(var + 1) * LANES])
                outs.append(o / den)
            o_ref[0, r0:r0 + WINDOW, pair * LANES:(pair + 1) * LANES] = (
                jnp.where(low, outs[0], outs[1]).astype(BF16))


def _attention(sinks, q, k2, v2, *, tq):
    nb, seq, _ = q.shape
    r = tq // WINDOW
    cur = lambda w: pl.BlockSpec((1, tq, w), lambda b, i: (b, i, 0))
    prev = lambda w: pl.BlockSpec((1, WINDOW, w), lambda b, i: (b, jnp.maximum(i * r - 1, 0), 0))
    return pl.pallas_call(
        functools.partial(_attn_kernel, tq=tq),
        grid=(nb, seq // tq),
        in_specs=[pl.BlockSpec(memory_space=pltpu.SMEM),
                  cur(ATTN_WIDTH), cur(2 * KV_WIDTH), prev(2 * KV_WIDTH), cur(2 * KV_WIDTH), prev(2 * KV_WIDTH)],
        out_specs=cur(ATTN_WIDTH),
        out_shape=jax.ShapeDtypeStruct((nb, seq, ATTN_WIDTH), BF16),
        compiler_params=pltpu.CompilerParams(dimension_semantics=("arbitrary", "arbitrary"),
                                             vmem_limit_bytes=VMEM_LIMIT),
        name="swa_attn",
    )(sinks, q, k2, k2, v2, v2)


def _ssd_kernel(xs_ref, bc_ref, dt_ref, a_ref, dskip_ref, tri_ref,
                y_ref, hT_ref, h_s, *, tc):
    l = pl.program_id(1)
    n_l = pl.num_programs(1)
    gw = SSM_WIDTH // N_SSM_GROUPS
    hpg = N_SSM_HEADS // N_SSM_GROUPS

    @pl.when(l == 0)
    def _():
        h_s[...] = jnp.zeros(h_s.shape, F32)

    lane = lax.broadcasted_iota(jnp.int32, (SSD_CHUNK, LANES), 1)
    low = lane < SSM_HEAD_DIM
    qi = lax.broadcasted_iota(jnp.int32, (SSD_CHUNK, SSD_CHUNK), 0)
    sj = lax.broadcasted_iota(jnp.int32, (SSD_CHUNK, SSD_CHUNK), 1)
    causal = sj <= qi
    a_neg = -jnp.exp(a_ref[...])
    tri = tri_ref[...]
    zero_bf = jnp.zeros((SSD_CHUNK, LANES), BF16)

    for c in range(tc // SSD_CHUNK):
        rows = slice(c * SSD_CHUNK, (c + 1) * SSD_CHUNK)
        dt = dt_ref[0, rows, :]
        hi, mid, lo = _split3(dt * a_neg)
        acum = (_dot(tri, hi) + _dot(tri, mid) + _dot(tri, lo)) * LOG2E
        acum_t = acum.T
        dt_t = dt.T
        a_end_t = acum_t[:, SSD_CHUNK - 1:SSD_CHUNK]
        w_end_t = dt_t * jnp.exp2(a_end_t - acum_t)
        dec_t = jnp.exp2(a_end_t)
        dec_t = jnp.broadcast_to(dec_t, (LANES, LANES))
        row_t = acum_t - jnp.log2(dt_t)
        for g in range(N_SSM_GROUPS):
            b_g = bc_ref[0, rows, g * D_STATE:(g + 1) * D_STATE]
            c_g = bc_ref[0, rows, (N_SSM_GROUPS + g) * D_STATE:(N_SSM_GROUPS + g + 1) * D_STATE]
            cb = _dot_nt(c_g, b_g)
            b_gt = b_g.astype(F32).T
            y_inter = _dot(c_g, h_s[g].astype(BF16))
            for pr in range(hpg // 2):
                lhs_y, lhs_s, escale, dsc = [], [], [], []
                for e in range(2):
                    h = g * hpg + 2 * pr + e
                    col = jnp.broadcast_to(acum[:, h:h + 1], (SSD_CHUNK, SSD_CHUNK))
                    decay_dt = jnp.where(causal, jnp.exp2(col - row_t[h:h + 1, :]), 0.0)
                    lhs_y.append((cb * decay_dt).astype(BF16))
                    lhs_s.append((b_gt * w_end_t[h:h + 1, :]).astype(BF16))
                    escale.append(jnp.exp2(col))
                    dsc.append(dec_t[h:h + 1, :])
                c0 = g * gw + pr * LANES
                x_pair = xs_ref[0, rows, c0:c0 + LANES]
                x_bf = x_pair.astype(BF16)
                rhs = jnp.concatenate([jnp.where(low, x_bf, zero_bf), jnp.where(low, zero_bf, x_bf)], axis=0)
                y_in = _dot(jnp.concatenate(lhs_y, axis=1), rhs)
                st = _dot(jnp.concatenate(lhs_s, axis=1), rhs)
                lc = pr * LANES
                y = y_in + y_inter[:, lc:lc + LANES] * jnp.where(low, escale[0], escale[1])
                y_ref[0, rows, c0:c0 + LANES] = y + dskip_ref[:, c0:c0 + LANES] * x_pair
                h_s[g, :, lc:lc + LANES] = h_s[g, :, lc:lc + LANES] * jnp.where(low, dsc[0], dsc[1]) + st

    @pl.when(l == n_l - 1)
    def _():
        hT_ref[0] = h_s[...]


def _ssd(xs, bc, dt, a_pad, dskip, tri, *, tc):
    nb, seq, _ = xs.shape
    row = lambda w: pl.BlockSpec((1, tc, w), lambda b, l: (b, l, 0))
    const = lambda shape: pl.BlockSpec(shape, lambda b, l: (0,) * len(shape))
    gw = SSM_WIDTH // N_SSM_GROUPS
    return pl.pallas_call(
        functools.partial(_ssd_kernel, tc=tc),
        grid=(nb, seq // tc),
        in_specs=[row(SSM_WIDTH), row(CONV_DIM - SSM_WIDTH), row(LANES),
                  const((1, LANES)), const((1, SSM_WIDTH)), const((SSD_CHUNK, SSD_CHUNK))],
        out_specs=[row(SSM_WIDTH),
                   pl.BlockSpec((1, N_SSM_GROUPS, D_STATE, gw), lambda b, l: (b, 0, 0, 0))],
        out_shape=[jax.ShapeDtypeStruct((nb, seq, SSM_WIDTH), F32),
                   jax.ShapeDtypeStruct((nb, N_SSM_GROUPS, D_STATE, gw), F32)],
        scratch_shapes=[pltpu.VMEM((N_SSM_GROUPS, D_STATE, gw), F32)],
        compiler_params=pltpu.CompilerParams(dimension_semantics=("arbitrary", "arbitrary"),
                                             vmem_limit_bytes=VMEM_LIMIT),
        name="ssd_scan",
    )(xs, bc, dt, a_pad, dskip, tri)


def _outffn_kernel(*refs, ffc, gated):
    if gated:
        x_ref, oa_ref, ys_ref, z_ref, gnorm_ref, wo_ref, gffn_ref, wgu_ref, wd_ref, o_ref = refs
        gw = SSM_WIDTH // N_SSM_GROUPS
        parts = []
        for g in range(N_SSM_GROUPS):
            ls = slice(g * gw, (g + 1) * gw)
            yg = ys_ref[:, ls] * _silu(z_ref[:, ls])
            yg = yg * lax.rsqrt(jnp.mean(yg * yg, axis=-1, keepdims=True) + EPS)
            parts.append((yg * gnorm_ref[:, ls]).astype(BF16))
        ys = jnp.concatenate(parts, axis=1)
    else:
        x_ref, oa_ref, ys_ref, wo_ref, gffn_ref, wgu_ref, wd_ref, o_ref = refs
        ys = ys_ref[...]
    x = x_ref[...]
    xm = x + _dot(oa_ref[...], wo_ref[0, 0:ATTN_WIDTH, :]) + _dot(ys, wo_ref[0, ATTN_WIDTH:, :])
    hn = _rms_rows(xm, gffn_ref[...]).astype(BF16)
    acc = jnp.zeros_like(xm)
    c0 = 0
    for width in ffc:
        g = _dot(hn, wgu_ref[0, :, c0:c0 + width])
        up = _dot(hn, wgu_ref[0, :, D_FF + c0:D_FF + c0 + width])
        acc = acc + _dot((_silu(g) * up).astype(BF16), wd_ref[0, c0:c0 + width, :])
        c0 += width
    o_ref[...] = xm + acc


def _outffn(x, oa, ys, gate, wo, gffn, wgu, wd, *, layer, tm, ffc):
    rows = x.shape[0]
    row = lambda w: pl.BlockSpec((tm, w), lambda i: (i, 0))
    const = lambda shape: pl.BlockSpec(shape, lambda i: (0,) * len(shape), pipeline_mode=pl.Buffered(1))
    gated = gate is not None
    in_specs = [row(D_MODEL), row(ATTN_WIDTH), row(SSM_WIDTH)]
    args = [x, oa, ys]
    if gated:
        in_specs += [row(SSM_WIDTH), const((1, SSM_WIDTH))]
        args += list(gate)
    wblk = lambda r, w: pl.BlockSpec((1, r, w), lambda i: (layer, 0, 0), pipeline_mode=pl.Buffered(1))
    in_specs += [wblk(ATTN_WIDTH + SSM_WIDTH, D_MODEL), const((1, D_MODEL)),
                 wblk(D_MODEL, 2 * D_FF), wblk(D_FF, D_MODEL)]
    args += [wo, gffn, wgu, wd]
    return pl.pallas_call(
        functools.partial(_outffn_kernel, ffc=ffc, gated=gated),
        grid=(rows // tm,),
        in_specs=in_specs,
        out_specs=row(D_MODEL),
        out_shape=jax.ShapeDtypeStruct((rows, D_MODEL), F32),
        compiler_params=pltpu.CompilerParams(dimension_semantics=("arbitrary",),
                                             vmem_limit_bytes=VMEM_LIMIT),
        name="outproj_ffn",
    )(*args)


T_PAD = SUBLANES


def _attn_dec_kernel(q_ref, kc_ref, kn_ref, vc_ref, vn_ref, sink_ref, o_ref, wk_ref, wv_ref, *, t_new):
    q = q_ref[...]
    bb, nq, _ = q.shape
    qpk = N_Q_HEADS // N_KV_HEADS
    n_c = kc_ref.shape[2]
    kc, vc, kn, vn = kc_ref[0], vc_ref[0], kn_ref[...], vn_ref[...]
    s_c = jnp.einsum('bqd,bkd->bqk', q, kc.astype(BF16), preferred_element_type=F32)
    s_n = jnp.einsum('bqd,bkd->bqk', q, kn.astype(BF16), preferred_element_type=F32)
    t_c = (lax.broadcasted_iota(jnp.int32, (bb, nq, n_c), 1) % (t_new * qpk)) // qpk
    j_c = lax.broadcasted_iota(jnp.int32, (bb, nq, n_c), 2)
    s_c = jnp.where(j_c > t_c + (n_c - WINDOW), s_c, -jnp.inf)
    t_n = (lax.broadcasted_iota(jnp.int32, (bb, nq, T_PAD), 1) % (t_new * qpk)) // qpk
    j_n = lax.broadcasted_iota(jnp.int32, (bb, nq, T_PAD), 2)
    s_n = jnp.where((j_n <= t_n) & (j_n < t_new), s_n, -jnp.inf)
    sk = sink_ref[...]
    m = jnp.maximum(jnp.maximum(jnp.max(s_c, axis=-1, keepdims=True), jnp.max(s_n, axis=-1, keepdims=True)), sk)
    p_c = jnp.exp(s_c - m)
    p_n = jnp.exp(s_n - m)
    den = jnp.sum(p_c, axis=-1, keepdims=True) + jnp.sum(p_n, axis=-1, keepdims=True) + jnp.exp(sk - m)
    o = (jnp.einsum('bqk,bkd->bqd', p_c.astype(BF16), vc.astype(BF16), preferred_element_type=F32)
         + jnp.einsum('bqk,bkd->bqd', p_n.astype(BF16), vn.astype(BF16), preferred_element_type=F32))
    o = o / den
    rows_g = nq // N_KV_HEADS
    lane = lax.broadcasted_iota(jnp.int32, (bb, rows_g, LANES), 2)
    merged = o[:, 0:rows_g, :]
    for g in range(1, N_KV_HEADS):
        merged = jnp.where(lane // HEAD_DIM == g, o[:, g * rows_g:(g + 1) * rows_g, :], merged)
    o_ref[...] = merged.astype(BF16)

    sub = lax.broadcasted_iota(jnp.int32, (bb, T_PAD, LANES), 1)
    for c, n, w_ref in ((kc, kn, wk_ref), (vc, vn, wv_ref)):
        shifted = pltpu.roll(c.reshape(bb * n_c, LANES), bb * n_c - t_new, 0).reshape(bb, n_c, LANES)
        tail = pltpu.roll(n.reshape(bb * T_PAD, LANES), T_PAD - t_new, 0).reshape(bb, T_PAD, LANES)
        w_ref[:, 0:n_c - T_PAD, :] = shifted[:, 0:n_c - T_PAD, :]
        w_ref[:, n_c - T_PAD:, :] = jnp.where(sub < T_PAD - t_new, shifted[:, n_c - T_PAD:, :], tail)


def _attention_dec(q, kc, kn, vc, vn, sink_rows, *, layer, t_new, bb):
    n_seq, nq, w = q.shape
    n_c = kc.shape[2]
    blk = lambda r: pl.BlockSpec((bb, r, w), lambda i: (i, 0, 0))
    cache = pl.BlockSpec((1, bb, n_c, w), lambda i: (layer, i, 0, 0))
    win = jax.ShapeDtypeStruct((n_seq, n_c, w), F32)
    return pl.pallas_call(
        functools.partial(_attn_dec_kernel, t_new=t_new),
        grid=(n_seq // bb,),
        in_specs=[blk(nq), cache, blk(T_PAD), cache, blk(T_PAD), pl.BlockSpec((nq, 1), lambda i: (0, 0))],
        out_specs=[blk(nq // N_KV_HEADS), blk(n_c), blk(n_c)],
        out_shape=[jax.ShapeDtypeStruct((n_seq, nq // N_KV_HEADS, w), BF16), win, win],
        compiler_params=pltpu.CompilerParams(dimension_semantics=("arbitrary",), vmem_limit_bytes=VMEM_LIMIT),
        name="swa_attn_decode",
    )(q, kc, kn, vc, vn, sink_rows)


def _ssd_dec_kernel(*refs, bb, t_new, n_prev):
    (xs_ref, bc_ref, dt_ref, z_ref, h0_ref) = refs[:5]
    prev_refs = refs[5:5 + n_prev]
    (a_ref, dskip_ref, gnorm_ref, tri_ref, ones_ref, exp_ref,
     y_ref, h_ref, aw_s, xdt_s, xw_s, ea_s, dec_s) = refs[5 + n_prev:]
    d = pl.program_id(0)
    gw = SSM_WIDTH // N_SSM_GROUPS
    hpg = N_SSM_HEADS // N_SSM_GROUPS
    rows = bb * T_PAD

    for j in range(n_prev):
        @pl.when(d == j)
        def _(j=j):
            h_ref[0] = prev_refs[j][0]

    @pl.when(d == n_prev)
    def _():
        expand = exp_ref[...]

        def widen(v):
            hi, mid, lo = _split3(v)
            return _dot(hi, expand) + _dot(mid, expand) + _dot(lo, expand)

        def seq_sum(m, v):
            hi, mid, lo = _split3(v)
            return _dot(m, hi) + _dot(m, mid) + _dot(m, lo)

        dt = dt_ref[...].reshape(rows, LANES)
        dta = dt * (-jnp.exp(a_ref[...]))
        a_w = widen(seq_sum(tri_ref[...], dta))
        a_end = seq_sum(ones_ref[...], dta)
        xdt = xs_ref[...].reshape(rows, SSM_WIDTH) * widen(dt)
        aw_s[...] = a_w
        xdt_s[...] = xdt
        xw_s[...] = xdt * jnp.exp(widen(a_end) - a_w)
        ea_s[...] = jnp.exp(a_w)
        dec_s[...] = jnp.exp(a_end)

        t_row = lax.broadcasted_iota(jnp.int32, (T_PAD, gw), 0)

        def seq_body(i, carry):
            r0 = pl.multiple_of(i * T_PAD, T_PAD)
            rs = pl.ds(r0, T_PAD)
            dec = jnp.broadcast_to(dec_s[pl.ds(r0, 1), :], (LANES, LANES)).T
            bc = bc_ref[i]
            for g in range(N_SSM_GROUPS):
                b_g = bc[:, g * D_STATE:(g + 1) * D_STATE]
                c_g = bc[:, (N_SSM_GROUPS + g) * D_STATE:(N_SSM_GROUPS + g + 1) * D_STATE]
                cb = _dot_nt(c_g, b_g)
                ls = slice(g * gw, (g + 1) * gw)
                a_g = aw_s[rs, ls]
                xdt_g = xdt_s[rs, ls]
                y = jnp.zeros((T_PAD, gw), F32)
                for s in range(t_new):
                    w = jnp.where(t_row >= s, jnp.exp(a_g - a_g[s:s + 1, :]), 0.0)
                    y = y + (w * cb[:, s:s + 1]) * xdt_g[s:s + 1, :]
                h0g = h0_ref[0, i, g * hpg:(g + 1) * hpg].reshape(gw, D_STATE)
                y = y + _dot_nt(c_g, h0g.astype(BF16)) * ea_s[rs, ls]
                y = y + dskip_ref[:, ls] * xs_ref[i, :, ls]
                y = y * _silu(z_ref[i, :, ls])
                y = y * lax.rsqrt(jnp.mean(y * y, axis=-1, keepdims=True) + EPS)
                y_ref[i, :, ls] = (y * gnorm_ref[:, ls]).astype(BF16)
                delta = lax.dot_general(xw_s[rs, ls].astype(BF16), b_g, (((0,), (0,)), ((), ())),
                                        preferred_element_type=F32)
                for hl in range(hpg):
                    h = g * hpg + hl
                    h_ref[0, i, h] = (h0_ref[0, i, h] * dec[h:h + 1, :]
                                      + delta[hl * SSM_HEAD_DIM:(hl + 1) * SSM_HEAD_DIM, :])
            return carry

        lax.fori_loop(0, bb, seq_body, 0, unroll=True)


def _ssd_dec(xs3, bc3, dt3, z3, state_all, layer, prev_states, a_pad, dskip, gnorm, tri_b, ones_b, expand,
             *, bb, t_new):
    nb = xs3.shape[0]
    nblk = nb // bb
    n_prev = len(prev_states)
    rows = bb * T_PAD

    def phase_block(j):
        return lambda d, i: jnp.clip(i + (d - j) * nblk, 0, nblk - 1)

    cur = phase_block(n_prev)
    blk = lambda w: pl.BlockSpec((bb, T_PAD, w), lambda d, i: (cur(d, i), 0, 0))
    const = lambda shape: pl.BlockSpec(shape, lambda d, i: (0,) * len(shape))
    st_shape = (1, bb, N_SSM_HEADS, SSM_HEAD_DIM, D_STATE)
    in_specs = [blk(SSM_WIDTH), blk(CONV_DIM - SSM_WIDTH), blk(LANES), blk(SSM_WIDTH),
                pl.BlockSpec(st_shape, lambda d, i: (layer, cur(d, i), 0, 0, 0))]
    for j in range(n_prev):
        in_specs.append(pl.BlockSpec(st_shape, lambda d, i, j=j: (0, phase_block(j)(d, i), 0, 0, 0)))
    in_specs += [const((1, LANES)), const((1, SSM_WIDTH)), const((1, SSM_WIDTH)),
                 const((rows, rows)), const((rows, rows)), const((LANES, SSM_WIDTH))]
    return pl.pallas_call(
        functools.partial(_ssd_dec_kernel, bb=bb, t_new=t_new, n_prev=n_prev),
        grid=(n_prev + 1, nblk),
        in_specs=in_specs,
        out_specs=[blk(SSM_WIDTH), pl.BlockSpec(st_shape, lambda d, i: (d, i, 0, 0, 0))],
        out_shape=[jax.ShapeDtypeStruct((nb, T_PAD, SSM_WIDTH), BF16),
                   jax.ShapeDtypeStruct((n_prev + 1,) + state_all.shape[1:], F32)],
        scratch_shapes=[pltpu.VMEM((rows, SSM_WIDTH), F32), pltpu.VMEM((rows, SSM_WIDTH), F32),
                        pltpu.VMEM((rows, SSM_WIDTH), F32), pltpu.VMEM((rows, SSM_WIDTH), F32),
                        pltpu.VMEM((rows, LANES), F32)],
        compiler_params=pltpu.CompilerParams(dimension_semantics=("arbitrary", "arbitrary"),
                                             vmem_limit_bytes=VMEM_LIMIT),
        name="ssd_decode",
    )(xs3, bc3, dt3, z3, state_all, *prev_states, a_pad, dskip, gnorm, tri_b, ones_b, expand)


def _rope_tables(pos):
    half = HEAD_DIM // 2
    inv = ROPE_THETA ** (-np.arange(half, dtype=np.float64) / half)
    ang = np.asarray(pos, np.float64)[:, None] * inv[None, :]
    cos = np.tile(np.cos(ang), (1, LANES // half))
    sin = np.sin(ang)
    sin_signed = np.tile(np.concatenate([-sin, sin], axis=1), (1, LANES // HEAD_DIM))
    return jnp.asarray(cos, F32), jnp.asarray(sin_signed, F32)


def _layer_consts(l, p):
    return dict(
        layer=l,
        gmix=p['norm_mix'][l][None, :],
        qn=jnp.tile(p['q_norm'][l], LANES // HEAD_DIM)[None, :],
        kn=jnp.tile(p['k_norm'][l], LANES // HEAD_DIM)[None, :],
        convw=p['conv_w'][l],
        convb=p['conv_b'][l][None, :],
        dtb=jnp.pad(p['dt_bias'][l], (0, LANES - N_SSM_HEADS))[None, :],
        a_pad=jnp.pad(p['a_log'][l], (0, LANES - N_SSM_HEADS))[None, :],
        dskip=jnp.repeat(p['d_skip'][l], SSM_HEAD_DIM)[None, :],
        gnorm=p['ssm_norm'][l][None, :],
        sinks=p['sinks'][l],
        gffn=p['norm_ffn'][l][None, :],
    )


def _state_from_t(h_t):
    nb = h_t.shape[0]
    hpg = N_SSM_HEADS // N_SSM_GROUPS
    h = h_t.reshape(nb, N_SSM_GROUPS, D_STATE, hpg, SSM_HEAD_DIM)
    return jnp.transpose(h, (0, 1, 3, 4, 2)).reshape(nb, N_SSM_HEADS, SSM_HEAD_DIM, D_STATE)


def _prompt_layer(x, c, shared, *, tm):
    nb, seq, _ = x.shape
    (q, k2, v2, z, xs, bc, dt, knew, vnew, convst) = _inproj(
        x, c['gmix'], shared['win'], shared['m128'], c['qn'], c['kn'], shared['cos_p'], shared['sin_p'],
        c['convw'], c['convb'], c['dtb'], None, layer=c['layer'], tm=tm, carry_rows=SUBLANES, shift=1, n_keep=min(WINDOW, seq))
    oa = _attention(c['sinks'], q, k2, v2, tq=tm)
    ys, h_t = _ssd(xs, bc, dt, c['a_pad'], c['dskip'], shared['tri'], tc=tm)
    rows = nb * seq
    xo = _outffn(x.reshape(rows, D_MODEL), oa.reshape(rows, ATTN_WIDTH), ys.reshape(rows, SSM_WIDTH),
                 (z.reshape(rows, SSM_WIDTH), c['gnorm']), shared['wo'], c['gffn'], shared['wgu'], shared['wd'],
                 layer=c['layer'], tm=tm,
                 ffc=FF_CHUNKS)
    n_keep = knew.shape[1]
    return (xo.reshape(nb, seq, D_MODEL),
            knew.reshape(nb, n_keep, N_KV_HEADS, HEAD_DIM), vnew.reshape(nb, n_keep, N_KV_HEADS, HEAD_DIM),
            convst[:, SUBLANES - (CONV_W - 1):, :], _state_from_t(h_t))


def _sample_layer(x_t, cache_k, cache_v, conv0, state_all, layer, prev_states, c, shared, *, n_seq, t_new):
    rows = t_new * n_seq
    qpk = N_Q_HEADS // N_KV_HEADS
    n_c = cache_k.shape[2]
    conv0_t = jnp.transpose(conv0, (1, 0, 2)).reshape((CONV_W - 1) * n_seq, CONV_DIM)
    (q, _, _, z, xs, bc, dt, knew, vnew, convst) = _inproj(
        x_t, c['gmix'], shared['win'], shared['m128'], c['qn'], c['kn'], shared['cos_s'], shared['sin_s'],
        c['convw'], c['convb'], c['dtb'], conv0_t, layer=c['layer'], tm=rows, carry_rows=(CONV_W - 1) * n_seq, shift=n_seq,
        n_keep=rows)

    def to_seq(a):
        a = jnp.transpose(a[0].reshape(t_new, n_seq, a.shape[-1]), (1, 0, 2))
        return jnp.pad(a, ((0, 0), (0, T_PAD - t_new), (0, 0)))

    q5 = jnp.transpose(q[0].reshape(t_new, n_seq, N_KV_HEADS, qpk, HEAD_DIM), (1, 2, 0, 3, 4))
    q5 = q5.reshape(n_seq, N_KV_HEADS, t_new * qpk, HEAD_DIM)
    q_m = jnp.concatenate(
        [jnp.pad(q5[:, g], ((0, 0), (0, 0), (g * HEAD_DIM, (N_KV_HEADS - 1 - g) * HEAD_DIM)))
         for g in range(N_KV_HEADS)], axis=1)
    sink_rows = jnp.tile(c['sinks'].reshape(N_KV_HEADS, 1, qpk), (1, t_new, 1)).reshape(-1, 1)
    o_m, win_k, win_v = _attention_dec(
        q_m, cache_k, to_seq(knew), cache_v, to_seq(vnew), sink_rows, layer=layer, t_new=t_new, bb=min(32, n_seq))
    o5 = o_m.reshape(n_seq, t_new, qpk, N_KV_HEADS, HEAD_DIM)
    oa = jnp.transpose(o5, (1, 0, 3, 2, 4)).reshape(rows, ATTN_WIDTH)
    win_k = win_k.reshape(n_seq, n_c, N_KV_HEADS, HEAD_DIM)
    win_v = win_v.reshape(n_seq, n_c, N_KV_HEADS, HEAD_DIM)

    y3, h_new = _ssd_dec(to_seq(xs), to_seq(bc), to_seq(dt), to_seq(z), state_all, layer, prev_states,
                         c['a_pad'], c['dskip'], c['gnorm'], shared['tri_b'], shared['ones_b'], shared['expand'],
                         bb=shared['bb_s'], t_new=t_new)
    ys = jnp.transpose(y3[:, :t_new], (1, 0, 2)).reshape(rows, SSM_WIDTH)

    xo = _outffn(x_t[0], oa, ys, None, shared['wo'], c['gffn'], shared['wgu'], shared['wd'], layer=c['layer'],
                 tm=min(512, rows), ffc=FF_CHUNKS)

    conv_new = jnp.transpose(convst[0].reshape(CONV_W - 1, n_seq, CONV_DIM), (1, 0, 2))
    return xo[None], win_k, win_v, conv_new, h_new


def kernel(x_prompt, x_sample, cache_win_k, cache_win_v, state_conv, state_ssm,
           norm_mix, w_in, q_norm, k_norm, sinks, conv_w, conv_b, dt_bias, a_log,
           d_skip, ssm_norm, w_out, norm_ffn, w_gate_up, w_down):
    p = dict(norm_mix=norm_mix, w_in=w_in, q_norm=q_norm, k_norm=k_norm, sinks=sinks, conv_w=conv_w,
             conv_b=conv_b, dt_bias=dt_bias, a_log=a_log, d_skip=d_skip, ssm_norm=ssm_norm, w_out=w_out,
             norm_ffn=norm_ffn, w_gate_up=w_gate_up, w_down=w_down)
    depth = w_in.shape[0]
    seq = x_prompt.shape[1]
    cos_p, sin_p = _rope_tables(np.arange(seq))
    half_blk = jnp.arange(LANES) // HEAD_DIM
    m128 = (jnp.where(half_blk[:, None] == half_blk[None, :], 1.0 / HEAD_DIM, 0.0)).astype(BF16)
    m128 = jnp.concatenate([m128, m128], axis=0)
    tri = (jnp.arange(SSD_CHUNK)[:, None] >= jnp.arange(SSD_CHUNK)[None, :]).astype(BF16)
    n_seq, t_new, _ = x_sample.shape
    pos_s = PAST_LEN + np.repeat(np.arange(t_new), n_seq)
    cos_s, sin_s = _rope_tables(pos_s)
    bb_s = min(8, n_seq)
    r_idx = jnp.arange(bb_s * T_PAD)
    same_seq = (r_idx[:, None] // T_PAD) == (r_idx[None, :] // T_PAD)
    tri_b = (same_seq & (r_idx[:, None] >= r_idx[None, :])).astype(BF16)
    ones_b = same_seq.astype(BF16)
    expand = (jnp.arange(LANES)[:, None] == (jnp.arange(SSM_WIDTH) // SSM_HEAD_DIM)[None, :]).astype(BF16)
    weights = dict(win=jnp.pad(w_in, ((0, 0), (0, 0), (0, PROJ_PAD - IN_PROJ_WIDTH))).astype(BF16),
                   wo=w_out.astype(BF16), wgu=w_gate_up.astype(BF16), wd=w_down.astype(BF16))
    shared = dict(cos_p=cos_p, sin_p=sin_p, cos_s=cos_s, sin_s=sin_s, m128=m128, tri=tri, tri_b=tri_b, **weights,
                  ones_b=ones_b, bb_s=bb_s, expand=expand)
    tm = min(512, seq)

    cache_k = cache_win_k.reshape(cache_win_k.shape[:3] + (KV_WIDTH,))
    cache_v = cache_win_v.reshape(cache_win_v.shape[:3] + (KV_WIDTH,))
    xp = x_prompt
    xs = jnp.transpose(x_sample, (1, 0, 2)).reshape(1, t_new * n_seq, D_MODEL)
    pk, pv, pc, ph = [], [], [], []
    sk, sv, sc, sh = [], [], [], []
    for l in range(depth):
        c = _layer_consts(l, p)
        xp, k1, v1, c1, h1 = _prompt_layer(xp, c, shared, tm=tm)
        prev = sh if l == depth - 1 else []
        xs, k2, v2, c2, h2 = _sample_layer(xs, cache_k, cache_v, state_conv[l], state_ssm, l, prev,
                                           c, shared, n_seq=n_seq, t_new=t_new)
        pk.append(k1); pv.append(v1); pc.append(c1); ph.append(h1)
        sk.append(k2); sv.append(v2); sc.append(c2); sh.append(h2)
    ys = jnp.transpose(xs.reshape(t_new, n_seq, D_MODEL), (1, 0, 2))
    return (xp, ys, jnp.stack(pk), jnp.stack(pv), jnp.stack(pc), jnp.stack(ph),
            jnp.stack(sk), jnp.stack(sv), jnp.stack(sc), sh[-1])
```

```python
import functools
import math

import numpy as np
import jax
import jax.numpy as jnp
from jax import lax
from jax.experimental import pallas as pl
from jax.experimental.pallas import tpu as pltpu

F32 = jnp.float32
BF16 = jnp.bfloat16

D_MODEL = 1024
HEAD_DIM = 64
N_Q_HEADS = 8
N_KV_HEADS = 2
ATTN_WIDTH = N_Q_HEADS * HEAD_DIM
KV_WIDTH = N_KV_HEADS * HEAD_DIM
WINDOW = 128
ROPE_THETA = 10000.0
ATTN_SCALE = HEAD_DIM ** -0.5
SSM_HEAD_DIM = 64
N_SSM_HEADS = 16
SSM_WIDTH = N_SSM_HEADS * SSM_HEAD_DIM
N_SSM_GROUPS = 2
D_STATE = 128
CONV_W = 4
CONV_DIM = SSM_WIDTH + 2 * N_SSM_GROUPS * D_STATE
SSD_CHUNK = 128
D_FF = 2816
EPS = 1e-6
LOG2E = math.log2(math.e)
PAST_LEN = 16384

LANES = 128
SUBLANES = 8

COL_Q = 0
COL_K = COL_Q + ATTN_WIDTH
COL_V = COL_K + KV_WIDTH
COL_Z = COL_V + KV_WIDTH
COL_XBC = COL_Z + SSM_WIDTH
COL_DT = COL_XBC + CONV_DIM
IN_PROJ_WIDTH = COL_DT + N_SSM_HEADS
PROJ_PAD = COL_DT + LANES
QKV_COLS = COL_Z

VMEM_LIMIT = 56 * 1024 * 1024
FF_CHUNKS = (6 * 256, 5 * 256)


def _dot(a, b):
    return jnp.dot(a, b, preferred_element_type=F32)


def _dot_nt(a, b):
    return lax.dot_general(a, b, (((1,), (1,)), ((), ())), preferred_element_type=F32)


def _silu(x):
    return x / (1.0 + jnp.exp(-x))


def _softplus(x):
    return jnp.maximum(x, 0.0) + jnp.log1p(jnp.exp(-jnp.abs(x)))


def _rms_rows(x, g):
    return x * lax.rsqrt(jnp.mean(x * x, axis=-1, keepdims=True) + EPS) * g


def _split2(x):
    hi = x.astype(BF16)
    lo = (x - hi.astype(F32)).astype(BF16)
    return hi, lo


def _split3(x):
    hi = x.astype(BF16)
    r1 = x - hi.astype(F32)
    mid = r1.astype(BF16)
    lo = (r1 - mid.astype(F32)).astype(BF16)
    return hi, mid, lo


def _head_norm_rope(x, m128, g, cos, sin_signed, low_half):
    hi, lo = _split2(x * x)
    ms = _dot(jnp.concatenate([hi, lo], axis=1), m128)
    xn = x * lax.rsqrt(ms + EPS) * g
    rot = jnp.where(low_half, pltpu.roll(xn, LANES - HEAD_DIM // 2, 1),
                    pltpu.roll(xn, HEAD_DIM // 2, 1))
    return xn * cos + rot * sin_signed


def _inproj_kernel(*refs, tm, carry_rows, shift, n_keep, has_state, n_parts):
    if has_state:
        (x_ref, gmix_ref, win_ref, m128_ref, qn_ref, kn_ref, cos_ref, sin_ref,
         convw_ref, convb_ref, dtb_ref, conv0_ref,
         q_ref, k2_ref, v2_ref, z_ref, xs_ref, bc_ref, dt_ref, knew_ref, vnew_ref, convst_ref,
         xbc_s) = refs
    else:
        (x_ref, gmix_ref, win_ref, m128_ref, qn_ref, kn_ref, cos_ref, sin_ref,
         convw_ref, convb_ref, dtb_ref,
         q_ref, k2_ref, v2_ref, z_ref, xs_ref, bc_ref, dt_ref, knew_ref, vnew_ref, convst_ref,
         xbc_s) = refs
        conv0_ref = None
    l = pl.program_id(1)

    @pl.when(l == 0)
    def _():
        if has_state:
            xbc_s[0:carry_rows, :] = conv0_ref[...]
        else:
            xbc_s[0:carry_rows, :] = jnp.zeros((carry_rows, CONV_DIM), F32)

    m128 = m128_ref[...]
    cw = 512
    nr = tm // n_parts
    lane = lax.broadcasted_iota(jnp.int32, (nr, LANES), 1)
    low_half = (lane % HEAD_DIM) < (HEAD_DIM // 2)
    for part in range(n_parts):
        r0 = part * nr
        rs = slice(r0, r0 + nr)
        base = carry_rows + r0
        u = _rms_rows(x_ref[0, rs, :], gmix_ref[...]).astype(BF16)
        for j in range(CONV_DIM // cw):
            c0 = COL_XBC + j * cw
            xbc_s[base:base + nr, j * cw:(j + 1) * cw] = _dot(u, win_ref[0, :, c0:c0 + cw])
        z_ref[0, rs, :] = _dot(u, win_ref[0, :, COL_Z:COL_Z + SSM_WIDTH])
        heads = _dot(u, win_ref[0, :, 0:QKV_COLS])
        dt_raw = _dot(u, win_ref[0, :, COL_DT:COL_DT + LANES])
        cos = cos_ref[rs, :]
        sin = sin_ref[rs, :]

        for j in range(ATTN_WIDTH // LANES):
            qj = heads[:, COL_Q + j * LANES:COL_Q + (j + 1) * LANES]
            qj = _head_norm_rope(qj, m128, qn_ref[...], cos, sin, low_half)
            q_ref[0, rs, j * LANES:(j + 1) * LANES] = (qj * ATTN_SCALE).astype(BF16)

        k = _head_norm_rope(heads[:, COL_K:COL_K + KV_WIDTH], m128, kn_ref[...], cos, sin, low_half)
        v = heads[:, COL_V:COL_V + KV_WIDTH]
        k2_ref[0, rs, 0:LANES] = k.astype(BF16)
        k2_ref[0, rs, LANES:2 * LANES] = pltpu.roll(k, HEAD_DIM, 1).astype(BF16)
        v2_ref[0, rs, 0:LANES] = v.astype(BF16)
        v2_ref[0, rs, LANES:2 * LANES] = pltpu.roll(v, HEAD_DIM, 1).astype(BF16)
        if part == n_parts - 1:
            knew_ref[0] = k[nr - n_keep:, :]
            vnew_ref[0] = v[nr - n_keep:, :]

        dt_ref[0, rs, :] = _softplus(dt_raw + dtb_ref[...])

        for j in range(CONV_DIM // cw):
            cs = slice(j * cw, (j + 1) * cw)
            w = [convw_ref[i:i + 1, cs] for i in range(CONV_W)]
            bias = convb_ref[:, cs]
            if shift % SUBLANES == 0:
                acc = bias
                for i in range(CONV_W):
                    off = base - (CONV_W - 1 - i) * shift
                    acc = acc + xbc_s[off:off + nr, cs] * w[i]
                act = _silu(acc)
            else:
                cur = xbc_s[base:base + nr, cs]
                acc = cur * w[0]
                for i in range(1, CONV_W):
                    acc = pltpu.roll(acc, shift, 0) + cur * w[i]
                head = bias
                for i in range(CONV_W):
                    off = base - (CONV_W - 1 - i) * shift
                    head = head + xbc_s[off:off + SUBLANES, cs] * w[i]
                act = _silu(jnp.concatenate([head, acc[SUBLANES:, :] + bias], axis=0))
            if j * cw < SSM_WIDTH:
                xs_ref[0, rs, cs] = act
            else:
                bc_ref[0, rs, :] = act.astype(BF16)

    convst_ref[0] = xbc_s[tm:tm + carry_rows, :]
    xbc_s[0:carry_rows, :] = xbc_s[tm:tm + carry_rows, :]


def _inproj(x, gmix, win, m128, qn, kn, cos, sin, convw, convb, dtb, conv0, *, layer, tm, carry_rows, shift,
            n_keep):
    nb, seq, _ = x.shape
    n_l = seq // tm
    n_parts = 1
    has_state = conv0 is not None
    const = lambda shape: pl.BlockSpec(shape, lambda b, l: (0,) * len(shape))
    row = lambda w: pl.BlockSpec((1, tm, w), lambda b, l: (b, l, 0))
    in_specs = [row(D_MODEL), const((1, D_MODEL)),
                pl.BlockSpec((1, D_MODEL, PROJ_PAD), lambda b, l: (layer, 0, 0)), const((2 * LANES, LANES)),
                const((1, LANES)), const((1, LANES)),
                pl.BlockSpec((tm, LANES), lambda b, l: (l, 0)), pl.BlockSpec((tm, LANES), lambda b, l: (l, 0)),
                const((CONV_W, CONV_DIM)), const((1, CONV_DIM)), const((1, LANES))]
    args = [x, gmix, win, m128, qn, kn, cos, sin, convw, convb, dtb]
    if has_state:
        in_specs.append(const((carry_rows, CONV_DIM)))
        args.append(conv0)
    last = lambda rows, w: pl.BlockSpec((1, rows, w), lambda b, l: (b, 0, 0))
    out_specs = [row(ATTN_WIDTH), row(2 * KV_WIDTH), row(2 * KV_WIDTH), row(SSM_WIDTH), row(SSM_WIDTH),
                 row(CONV_DIM - SSM_WIDTH), row(LANES),
                 last(n_keep, KV_WIDTH), last(n_keep, KV_WIDTH), last(carry_rows, CONV_DIM)]
    out_shape = [jax.ShapeDtypeStruct((nb, seq, ATTN_WIDTH), BF16),
                 jax.ShapeDtypeStruct((nb, seq, 2 * KV_WIDTH), BF16),
                 jax.ShapeDtypeStruct((nb, seq, 2 * KV_WIDTH), BF16),
                 jax.ShapeDtypeStruct((nb, seq, SSM_WIDTH), F32),
                 jax.ShapeDtypeStruct((nb, seq, SSM_WIDTH), F32),
                 jax.ShapeDtypeStruct((nb, seq, CONV_DIM - SSM_WIDTH), BF16),
                 jax.ShapeDtypeStruct((nb, seq, LANES), F32),
                 jax.ShapeDtypeStruct((nb, n_keep, KV_WIDTH), F32),
                 jax.ShapeDtypeStruct((nb, n_keep, KV_WIDTH), F32),
                 jax.ShapeDtypeStruct((nb, carry_rows, CONV_DIM), F32)]
    kern = functools.partial(_inproj_kernel, tm=tm, carry_rows=carry_rows, shift=shift, n_keep=n_keep,
                             has_state=has_state, n_parts=n_parts)
    return pl.pallas_call(
        kern, grid=(nb, n_l), in_specs=in_specs, out_specs=out_specs, out_shape=out_shape,
        scratch_shapes=[pltpu.VMEM((carry_rows + tm, CONV_DIM), F32)],
        compiler_params=pltpu.CompilerParams(dimension_semantics=("arbitrary", "arbitrary"),
                                             vmem_limit_bytes=VMEM_LIMIT),
        name="inproj",
    )(*args)


def _attn_kernel(sinks_ref, q_ref, kc_ref, kp_ref, vc_ref, vp_ref, o_ref, *, tq):
    i = pl.program_id(1)
    nblk = tq // WINDOW
    lane = lax.broadcasted_iota(jnp.int32, (WINDOW, LANES), 1)
    low = lane < HEAD_DIM
    qi = lax.broadcasted_iota(jnp.int32, (WINDOW, 2 * WINDOW), 0)
    kj = lax.broadcasted_iota(jnp.int32, (WINDOW, 2 * WINDOW), 1)
    diff = qi + WINDOW - kj
    band = (diff >= 0) & (diff < WINDOW)
    zero_bf = jnp.zeros((WINDOW, LANES), BF16)
    for blk in range(nblk):
        r0 = blk * WINDOW
        if blk == 0:
            k_prev, v_prev = kp_ref[0], vp_ref[0]
            mask = band & ((i > 0) | (kj >= WINDOW))
        else:
            k_prev = kc_ref[0, r0 - WINDOW:r0, :]
            v_prev = vc_ref[0, r0 - WINDOW:r0, :]
            mask = band
        kk = jnp.concatenate([k_prev, kc_ref[0, r0:r0 + WINDOW, :]], axis=0)
        vv = jnp.concatenate([v_prev, vc_ref[0, r0:r0 + WINDOW, :]], axis=0)
        for pair in range(N_Q_HEADS // 2):
            g = (2 * pair) // (N_Q_HEADS // N_KV_HEADS)
            qp = q_ref[0, r0:r0 + WINDOW, pair * LANES:(pair + 1) * LANES]
            outs = []
            for e in range(2):
                h = 2 * pair + e
                var = 0 if g == e else 1
                qh = jnp.where(low if e == 0 else ~low, qp, zero_bf)
                s = _dot_nt(qh, kk[:, var * LANES:(var + 1) * LANES])
                s = jnp.where(mask, s, -jnp.inf)
                sk = sinks_ref[h]
                m = jnp.maximum(jnp.max(s, axis=-1, keepdims=True), sk)
                p = jnp.exp(s - m)
                den = jnp.sum(p, axis=-1, keepdims=True) + jnp.exp(sk - m)
                o = _dot(p.astype(BF16), vv[:, var * LANES:(var + 1) * LANES])
                outs.append(o / den)
            o_ref[0, r0:r0 + WINDOW, pair * LANES:(pair + 1) * LANES] = (
                jnp.where(low, outs[0], outs[1]).astype(BF16))


def _attention(sinks, q, k2, v2, *, tq):
    nb, seq, _ = q.shape
    r = tq // WINDOW
    cur = lambda w: pl.BlockSpec((1, tq, w), lambda b, i: (b, i, 0))
    prev = lambda w: pl.BlockSpec((1, WINDOW, w), lambda b, i: (b, jnp.maximum(i * r - 1, 0), 0))
    return pl.pallas_call(
        functools.partial(_attn_kernel, tq=tq),
        grid=(nb, seq // tq),
        in_specs=[pl.BlockSpec(memory_space=pltpu.SMEM),
                  cur(ATTN_WIDTH), cur(2 * KV_WIDTH), prev(2 * KV_WIDTH), cur(2 * KV_WIDTH), prev(2 * KV_WIDTH)],
        out_specs=cur(ATTN_WIDTH),
        out_shape=jax.ShapeDtypeStruct((nb, seq, ATTN_WIDTH), BF16),
        compiler_params=pltpu.CompilerParams(dimension_semantics=("arbitrary", "arbitrary"),
                                             vmem_limit_bytes=VMEM_LIMIT),
        name="swa_attn",
    )(sinks, q, k2, k2, v2, v2)


def _ssd_kernel(xs_ref, bc_ref, dt_ref, a_ref, dskip_ref, tri_ref,
                y_ref, hT_ref, h_s, *, tc):
    l = pl.program_id(1)
    n_l = pl.num_programs(1)
    gw = SSM_WIDTH // N_SSM_GROUPS
    hpg = N_SSM_HEADS // N_SSM_GROUPS

    @pl.when(l == 0)
    def _():
        h_s[...] = jnp.zeros(h_s.shape, F32)

    lane = lax.broadcasted_iota(jnp.int32, (SSD_CHUNK, LANES), 1)
    low = lane < SSM_HEAD_DIM
    qi = lax.broadcasted_iota(jnp.int32, (SSD_CHUNK, SSD_CHUNK), 0)
    sj = lax.broadcasted_iota(jnp.int32, (SSD_CHUNK, SSD_CHUNK), 1)
    causal = sj <= qi
    a_neg = -jnp.exp(a_ref[...])
    tri = tri_ref[...]
    zero_bf = jnp.zeros((SSD_CHUNK, LANES), BF16)

    for c in range(tc // SSD_CHUNK):
        rows = slice(c * SSD_CHUNK, (c + 1) * SSD_CHUNK)
        dt = dt_ref[0, rows, :]
        hi, mid, lo = _split3(dt * a_neg)
        acum = (_dot(tri, hi) + _dot(tri, mid) + _dot(tri, lo)) * LOG2E
        acum_t = acum.T
        dt_t = dt.T
        a_end_t = acum_t[:, SSD_CHUNK - 1:SSD_CHUNK]
        w_end_t = dt_t * jnp.exp2(a_end_t - acum_t)
        dec_t = jnp.exp2(a_end_t)
        dec_t = jnp.broadcast_to(dec_t, (LANES, LANES))
        row_t = acum_t - jnp.log2(dt_t)
        for g in range(N_SSM_GROUPS):
            b_g = bc_ref[0, rows, g * D_STATE:(g + 1) * D_STATE]
            c_g = bc_ref[0, rows, (N_SSM_GROUPS + g) * D_STATE:(N_SSM_GROUPS + g + 1) * D_STATE]
            cb = _dot_nt(c_g, b_g)
            b_gt = b_g.astype(F32).T
            y_inter = _dot(c_g, h_s[g].astype(BF16))
            for pr in range(hpg // 2):
                lhs_y, lhs_s, escale, dsc = [], [], [], []
                for e in range(2):
                    h = g * hpg + 2 * pr + e
                    col = jnp.broadcast_to(acum[:, h:h + 1], (SSD_CHUNK, SSD_CHUNK))
                    decay_dt = jnp.where(causal, jnp.exp2(col - row_t[h:h + 1, :]), 0.0)
                    lhs_y.append((cb * decay_dt).astype(BF16))
                    lhs_s.append((b_gt * w_end_t[h:h + 1, :]).astype(BF16))
                    escale.append(jnp.exp2(col))
                    dsc.append(dec_t[h:h + 1, :])
                c0 = g * gw + pr * LANES
                x_pair = xs_ref[0, rows, c0:c0 + LANES]
                x_bf = x_pair.astype(BF16)
                rhs = jnp.concatenate([jnp.where(low, x_bf, zero_bf), jnp.where(low, zero_bf, x_bf)], axis=0)
                y_in = _dot(jnp.concatenate(lhs_y, axis=1), rhs)
                st = _dot(jnp.concatenate(lhs_s, axis=1), rhs)
                lc = pr * LANES
                y = y_in + y_inter[:, lc:lc + LANES] * jnp.where(low, escale[0], escale[1])
                y_ref[0, rows, c0:c0 + LANES] = y + dskip_ref[:, c0:c0 + LANES] * x_pair
                h_s[g, :, lc:lc + LANES] = h_s[g, :, lc:lc + LANES] * jnp.where(low, dsc[0], dsc[1]) + st

    @pl.when(l == n_l - 1)
    def _():
        hT_ref[0] = h_s[...]


def _ssd(xs, bc, dt, a_pad, dskip, tri, *, tc):
    nb, seq, _ = xs.shape
    row = lambda w: pl.BlockSpec((1, tc, w), lambda b, l: (b, l, 0))
    const = lambda shape: pl.BlockSpec(shape, lambda b, l: (0,) * len(shape))
    gw = SSM_WIDTH // N_SSM_GROUPS
    return pl.pallas_call(
        functools.partial(_ssd_kernel, tc=tc),
        grid=(nb, seq // tc),
        in_specs=[row(SSM_WIDTH), row(CONV_DIM - SSM_WIDTH), row(LANES),
                  const((1, LANES)), const((1, SSM_WIDTH)), const((SSD_CHUNK, SSD_CHUNK))],
        out_specs=[row(SSM_WIDTH),
                   pl.BlockSpec((1, N_SSM_GROUPS, D_STATE, gw), lambda b, l: (b, 0, 0, 0))],
        out_shape=[jax.ShapeDtypeStruct((nb, seq, SSM_WIDTH), F32),
                   jax.ShapeDtypeStruct((nb, N_SSM_GROUPS, D_STATE, gw), F32)],
        scratch_shapes=[pltpu.VMEM((N_SSM_GROUPS, D_STATE, gw), F32)],
        compiler_params=pltpu.CompilerParams(dimension_semantics=("arbitrary", "arbitrary"),
                                             vmem_limit_bytes=VMEM_LIMIT),
        name="ssd_scan",
    )(xs, bc, dt, a_pad, dskip, tri)


def _outffn_kernel(*refs, ffc, gated):
    if gated:
        x_ref, oa_ref, ys_ref, z_ref, gnorm_ref, wo_ref, gffn_ref, wgu_ref, wd_ref, o_ref = refs
        gw = SSM_WIDTH // N_SSM_GROUPS
        parts = []
        for g in range(N_SSM_GROUPS):
            ls = slice(g * gw, (g + 1) * gw)
            yg = ys_ref[:, ls] * _silu(z_ref[:, ls])
            yg = yg * lax.rsqrt(jnp.mean(yg * yg, axis=-1, keepdims=True) + EPS)
            parts.append((yg * gnorm_ref[:, ls]).astype(BF16))
        ys = jnp.concatenate(parts, axis=1)
    else:
        x_ref, oa_ref, ys_ref, wo_ref, gffn_ref, wgu_ref, wd_ref, o_ref = refs
        ys = ys_ref[...]
    x = x_ref[...]
    xm = x + _dot(oa_ref[...], wo_ref[0, 0:ATTN_WIDTH, :]) + _dot(ys, wo_ref[0, ATTN_WIDTH:, :])
    hn = _rms_rows(xm, gffn_ref[...]).astype(BF16)
    acc = jnp.zeros_like(xm)
    c0 = 0
    for width in ffc:
        g = _dot(hn, wgu_ref[0, :, c0:c0 + width])
        up = _dot(hn, wgu_ref[0, :, D_FF + c0:D_FF + c0 + width])
        acc = acc + _dot((_silu(g) * up).astype(BF16), wd_ref[0, c0:c0 + width, :])
        c0 += width
    o_ref[...] = xm + acc


def _outffn(x, oa, ys, gate, wo, gffn, wgu, wd, *, layer, tm, ffc):
    rows = x.shape[0]
    row = lambda w: pl.BlockSpec((tm, w), lambda i: (i, 0))
    const = lambda shape: pl.BlockSpec(shape, lambda i: (0,) * len(shape), pipeline_mode=pl.Buffered(1))
    gated = gate is not None
    in_specs = [row(D_MODEL), row(ATTN_WIDTH), row(SSM_WIDTH)]
    args = [x, oa, ys]
    if gated:
        in_specs += [row(SSM_WIDTH), const((1, SSM_WIDTH))]
        args += list(gate)
    wblk = lambda r, w: pl.BlockSpec((1, r, w), lambda i: (layer, 0, 0), pipeline_mode=pl.Buffered(1))
    in_specs += [wblk(ATTN_WIDTH + SSM_WIDTH, D_MODEL), const((1, D_MODEL)),
                 wblk(D_MODEL, 2 * D_FF), wblk(D_FF, D_MODEL)]
    args += [wo, gffn, wgu, wd]
    return pl.pallas_call(
        functools.partial(_outffn_kernel, ffc=ffc, gated=gated),
        grid=(rows // tm,),
        in_specs=in_specs,
        out_specs=row(D_MODEL),
        out_shape=jax.ShapeDtypeStruct((rows, D_MODEL), F32),
        compiler_params=pltpu.CompilerParams(dimension_semantics=("arbitrary",),
                                             vmem_limit_bytes=VMEM_LIMIT),
        name="outproj_ffn",
    )(*args)


T_PAD = SUBLANES


def _attn_dec_kernel(q_ref, kc_ref, kn_ref, vc_ref, vn_ref, sink_ref, o_ref, wk_ref, wv_ref, *, t_new):
    q = q_ref[...]
    bb, nq, _ = q.shape
    qpk = N_Q_HEADS // N_KV_HEADS
    n_c = kc_ref.shape[2]
    kc, vc, kn, vn = kc_ref[0], vc_ref[0], kn_ref[...], vn_ref[...]
    s_c = jnp.einsum('bqd,bkd->bqk', q, kc.astype(BF16), preferred_element_type=F32)
    s_n = jnp.einsum('bqd,bkd->bqk', q, kn.astype(BF16), preferred_element_type=F32)
    t_c = (lax.broadcasted_iota(jnp.int32, (bb, nq, n_c), 1) % (t_new * qpk)) // qpk
    j_c = lax.broadcasted_iota(jnp.int32, (bb, nq, n_c), 2)
    s_c = jnp.where(j_c > t_c + (n_c - WINDOW), s_c, -jnp.inf)
    t_n = (lax.broadcasted_iota(jnp.int32, (bb, nq, T_PAD), 1) % (t_new * qpk)) // qpk
    j_n = lax.broadcasted_iota(jnp.int32, (bb, nq, T_PAD), 2)
    s_n = jnp.where((j_n <= t_n) & (j_n < t_new), s_n, -jnp.inf)
    sk = sink_ref[...]
    m = jnp.maximum(jnp.maximum(jnp.max(s_c, axis=-1, keepdims=True), jnp.max(s_n, axis=-1, keepdims=True)), sk)
    p_c = jnp.exp(s_c - m)
    p_n = jnp.exp(s_n - m)
    den = jnp.sum(p_c, axis=-1, keepdims=True) + jnp.sum(p_n, axis=-1, keepdims=True) + jnp.exp(sk - m)
    o = (jnp.einsum('bqk,bkd->bqd', p_c.astype(BF16), vc.astype(BF16), preferred_element_type=F32)
         + jnp.einsum('bqk,bkd->bqd', p_n.astype(BF16), vn.astype(BF16), preferred_element_type=F32))
    o = o / den
    rows_g = nq // N_KV_HEADS
    lane = lax.broadcasted_iota(jnp.int32, (bb, rows_g, LANES), 2)
    merged = o[:, 0:rows_g, :]
    for g in range(1, N_KV_HEADS):
        merged = jnp.where(lane // HEAD_DIM == g, o[:, g * rows_g:(g + 1) * rows_g, :], merged)
    o_ref[...] = merged.astype(BF16)

    sub = lax.broadcasted_iota(jnp.int32, (bb, T_PAD, LANES), 1)
    for c, n, w_ref in ((kc, kn, wk_ref), (vc, vn, wv_ref)):
        shifted = pltpu.roll(c.reshape(bb * n_c, LANES), bb * n_c - t_new, 0).reshape(bb, n_c, LANES)
        tail = pltpu.roll(n.reshape(bb * T_PAD, LANES), T_PAD - t_new, 0).reshape(bb, T_PAD, LANES)
        w_ref[:, 0:n_c - T_PAD, :] = shifted[:, 0:n_c - T_PAD, :]
        w_ref[:, n_c - T_PAD:, :] = jnp.where(sub < T_PAD - t_new, shifted[:, n_c - T_PAD:, :], tail)


def _attention_dec(q, kc, kn, vc, vn, sink_rows, *, layer, t_new, bb):
    n_seq, nq, w = q.shape
    n_c = kc.shape[2]
    blk = lambda r: pl.BlockSpec((bb, r, w), lambda i: (i, 0, 0))
    cache = pl.BlockSpec((1, bb, n_c, w), lambda i: (layer, i, 0, 0))
    win = jax.ShapeDtypeStruct((n_seq, n_c, w), F32)
    return pl.pallas_call(
        functools.partial(_attn_dec_kernel, t_new=t_new),
        grid=(n_seq // bb,),
        in_specs=[blk(nq), cache, blk(T_PAD), cache, blk(T_PAD), pl.BlockSpec((nq, 1), lambda i: (0, 0))],
        out_specs=[blk(nq // N_KV_HEADS), blk(n_c), blk(n_c)],
        out_shape=[jax.ShapeDtypeStruct((n_seq, nq // N_KV_HEADS, w), BF16), win, win],
        compiler_params=pltpu.CompilerParams(dimension_semantics=("arbitrary",), vmem_limit_bytes=VMEM_LIMIT),
        name="swa_attn_decode",
    )(q, kc, kn, vc, vn, sink_rows)


def _ssd_dec_kernel(*refs, bb, t_new, n_prev):
    (xs_ref, bc_ref, dt_ref, z_ref, h0_ref) = refs[:5]
    prev_refs = refs[5:5 + n_prev]
    (a_ref, dskip_ref, gnorm_ref, tri_ref, ones_ref, exp_ref,
     y_ref, h_ref, aw_s, xdt_s, xw_s, ea_s, dec_s) = refs[5 + n_prev:]
    d = pl.program_id(0)
    gw = SSM_WIDTH // N_SSM_GROUPS
    hpg = N_SSM_HEADS // N_SSM_GROUPS
    rows = bb * T_PAD

    for j in range(n_prev):
        @pl.when(d == j)
        def _(j=j):
            h_ref[0] = prev_refs[j][0]

    @pl.when(d == n_prev)
    def _():
        expand = exp_ref[...]

        def widen(v):
            hi, mid, lo = _split3(v)
            return _dot(hi, expand) + _dot(mid, expand) + _dot(lo, expand)

        def seq_sum(m, v):
            hi, mid, lo = _split3(v)
            return _dot(m, hi) + _dot(m, mid) + _dot(m, lo)

        dt = dt_ref[...].reshape(rows, LANES)
        dta = dt * (-jnp.exp(a_ref[...]))
        a_w = widen(seq_sum(tri_ref[...], dta))
        a_end = seq_sum(ones_ref[...], dta)
        xdt = xs_ref[...].reshape(rows, SSM_WIDTH) * widen(dt)
        aw_s[...] = a_w
        xdt_s[...] = xdt
        xw_s[...] = xdt * jnp.exp(widen(a_end) - a_w)
        ea_s[...] = jnp.exp(a_w)
        dec_s[...] = jnp.exp(a_end)

        t_row = lax.broadcasted_iota(jnp.int32, (T_PAD, gw), 0)

        def seq_body(i, carry):
            r0 = pl.multiple_of(i * T_PAD, T_PAD)
            rs = pl.ds(r0, T_PAD)
            dec = jnp.broadcast_to(dec_s[pl.ds(r0, 1), :], (LANES, LANES)).T
            bc = bc_ref[i]
            for g in range(N_SSM_GROUPS):
                b_g = bc[:, g * D_STATE:(g + 1) * D_STATE]
                c_g = bc[:, (N_SSM_GROUPS + g) * D_STATE:(N_SSM_GROUPS + g + 1) * D_STATE]
                cb = _dot_nt(c_g, b_g)
                ls = slice(g * gw, (g + 1) * gw)
                a_g = aw_s[rs, ls]
                xdt_g = xdt_s[rs, ls]
                y = jnp.zeros((T_PAD, gw), F32)
                for s in range(t_new):
                    w = jnp.where(t_row >= s, jnp.exp(a_g - a_g[s:s + 1, :]), 0.0)
                    y = y + (w * cb[:, s:s + 1]) * xdt_g[s:s + 1, :]
                h0g = h0_ref[0, i, g * hpg:(g + 1) * hpg].reshape(gw, D_STATE)
                y = y + _dot_nt(c_g, h0g.astype(BF16)) * ea_s[rs, ls]
                y = y + dskip_ref[:, ls] * xs_ref[i, :, ls]
                y = y * _silu(z_ref[i, :, ls])
                y = y * lax.rsqrt(jnp.mean(y * y, axis=-1, keepdims=True) + EPS)
                y_ref[i, :, ls] = (y * gnorm_ref[:, ls]).astype(BF16)
                delta = lax.dot_general(xw_s[rs, ls].astype(BF16), b_g, (((0,), (0,)), ((), ())),
                                        preferred_element_type=F32)
                for hl in range(hpg):
                    h = g * hpg + hl
                    h_ref[0, i, h] = (h0_ref[0, i, h] * dec[h:h + 1, :]
                                      + delta[hl * SSM_HEAD_DIM:(hl + 1) * SSM_HEAD_DIM, :])
            return carry

        lax.fori_loop(0, bb, seq_body, 0, unroll=True)


def _ssd_dec(xs3, bc3, dt3, z3, state_all, layer, prev_states, a_pad, dskip, gnorm, tri_b, ones_b, expand,
             *, bb, t_new):
    nb = xs3.shape[0]
    nblk = nb // bb
    n_prev = len(prev_states)
    rows = bb * T_PAD

    def phase_block(j):
        return lambda d, i: jnp.clip(i + (d - j) * nblk, 0, nblk - 1)

    cur = phase_block(n_prev)
    blk = lambda w: pl.BlockSpec((bb, T_PAD, w), lambda d, i: (cur(d, i), 0, 0))
    const = lambda shape: pl.BlockSpec(shape, lambda d, i: (0,) * len(shape))
    st_shape = (1, bb, N_SSM_HEADS, SSM_HEAD_DIM, D_STATE)
    in_specs = [blk(SSM_WIDTH), blk(CONV_DIM - SSM_WIDTH), blk(LANES), blk(SSM_WIDTH),
                pl.BlockSpec(st_shape, lambda d, i: (layer, cur(d, i), 0, 0, 0))]
    for j in range(n_prev):
        in_specs.append(pl.BlockSpec(st_shape, lambda d, i, j=j: (0, phase_block(j)(d, i), 0, 0, 0)))
    in_specs += [const((1, LANES)), const((1, SSM_WIDTH)), const((1, SSM_WIDTH)),
                 const((rows, rows)), const((rows, rows)), const((LANES, SSM_WIDTH))]
    return pl.pallas_call(
        functools.partial(_ssd_dec_kernel, bb=bb, t_new=t_new, n_prev=n_prev),
        grid=(n_prev + 1, nblk),
        in_specs=in_specs,
        out_specs=[blk(SSM_WIDTH), pl.BlockSpec(st_shape, lambda d, i: (d, i, 0, 0, 0))],
        out_shape=[jax.ShapeDtypeStruct((nb, T_PAD, SSM_WIDTH), BF16),
                   jax.ShapeDtypeStruct((n_prev + 1,) + state_all.shape[1:], F32)],
        scratch_shapes=[pltpu.VMEM((rows, SSM_WIDTH), F32), pltpu.VMEM((rows, SSM_WIDTH), F32),
                        pltpu.VMEM((rows, SSM_WIDTH), F32), pltpu.VMEM((rows, SSM_WIDTH), F32),
                        pltpu.VMEM((rows, LANES), F32)],
        compiler_params=pltpu.CompilerParams(dimension_semantics=("arbitrary", "arbitrary"),
                                             vmem_limit_bytes=VMEM_LIMIT),
        name="ssd_decode",
    )(xs3, bc3, dt3, z3, state_all, *prev_states, a_pad, dskip, gnorm, tri_b, ones_b, expand)


def _rope_tables(pos):
    half = HEAD_DIM // 2
    inv = ROPE_THETA ** (-np.arange(half, dtype=np.float64) / half)
    ang = np.asarray(pos, np.float64)[:, None] * inv[None, :]
    cos = np.tile(np.cos(ang), (1, LANES // half))
    sin = np.sin(ang)
    sin_signed = np.tile(np.concatenate([-sin, sin], axis=1), (1, LANES // HEAD_DIM))
    return jnp.asarray(cos, F32), jnp.asarray(sin_signed, F32)


def _layer_consts(l, p):
    return dict(
        layer=l,
        gmix=p['norm_mix'][l][None, :],
        qn=jnp.tile(p['q_norm'][l], LANES // HEAD_DIM)[None, :],
        kn=jnp.tile(p['k_norm'][l], LANES // HEAD_DIM)[None, :],
        convw=p['conv_w'][l],
        convb=p['conv_b'][l][None, :],
        dtb=jnp.pad(p['dt_bias'][l], (0, LANES - N_SSM_HEADS))[None, :],
        a_pad=jnp.pad(p['a_log'][l], (0, LANES - N_SSM_HEADS))[None, :],
        dskip=jnp.repeat(p['d_skip'][l], SSM_HEAD_DIM)[None, :],
        gnorm=p['ssm_norm'][l][None, :],
        sinks=p['sinks'][l],
        gffn=p['norm_ffn'][l][None, :],
    )


def _state_from_t(h_t):
    nb = h_t.shape[0]
    hpg = N_SSM_HEADS // N_SSM_GROUPS
    h = h_t.reshape(nb, N_SSM_GROUPS, D_STATE, hpg, SSM_HEAD_DIM)
    return jnp.transpose(h, (0, 1, 3, 4, 2)).reshape(nb, N_SSM_HEADS, SSM_HEAD_DIM, D_STATE)


def _prompt_layer(x, c, shared, *, tm):
    nb, seq, _ = x.shape
    (q, k2, v2, z, xs, bc, dt, knew, vnew, convst) = _inproj(
        x, c['gmix'], shared['win'], shared['m128'], c['qn'], c['kn'], shared['cos_p'], shared['sin_p'],
        c['convw'], c['convb'], c['dtb'], None, layer=c['layer'], tm=tm, carry_rows=SUBLANES, shift=1, n_keep=min(WINDOW, seq))
    mix_rows = next(t for t in (4 * tm, 2 * tm, tm) if seq % t == 0)
    oa = _attention(c['sinks'], q, k2, v2, tq=mix_rows)
    ys, h_t = _ssd(xs, bc, dt, c['a_pad'], c['dskip'], shared['tri'], tc=mix_rows)
    rows = nb * seq
    xo = _outffn(x.reshape(rows, D_MODEL), oa.reshape(rows, ATTN_WIDTH), ys.reshape(rows, SSM_WIDTH),
                 (z.reshape(rows, SSM_WIDTH), c['gnorm']), shared['wo'], c['gffn'], shared['wgu'], shared['wd'],
                 layer=c['layer'], tm=tm,
                 ffc=FF_CHUNKS)
    n_keep = knew.shape[1]
    return (xo.reshape(nb, seq, D_MODEL),
            knew.reshape(nb, n_keep, N_KV_HEADS, HEAD_DIM), vnew.reshape(nb, n_keep, N_KV_HEADS, HEAD_DIM),
            convst[:, SUBLANES - (CONV_W - 1):, :], _state_from_t(h_t))


def _sample_layer(x_t, cache_k, cache_v, conv0, state_all, layer, prev_states, c, shared, *, n_seq, t_new):
    rows = t_new * n_seq
    qpk = N_Q_HEADS // N_KV_HEADS
    n_c = cache_k.shape[2]
    conv0_t = jnp.transpose(conv0, (1, 0, 2)).reshape((CONV_W - 1) * n_seq, CONV_DIM)
    (q, _, _, z, xs, bc, dt, knew, vnew, convst) = _inproj(
        x_t, c['gmix'], shared['win'], shared['m128'], c['qn'], c['kn'], shared['cos_s'], shared['sin_s'],
        c['convw'], c['convb'], c['dtb'], conv0_t, layer=c['layer'], tm=rows, carry_rows=(CONV_W - 1) * n_seq, shift=n_seq,
        n_keep=rows)

    def to_seq(a):
        a = jnp.transpose(a[0].reshape(t_new, n_seq, a.shape[-1]), (1, 0, 2))
        return jnp.pad(a, ((0, 0), (0, T_PAD - t_new), (0, 0)))

    q5 = jnp.transpose(q[0].reshape(t_new, n_seq, N_KV_HEADS, qpk, HEAD_DIM), (1, 2, 0, 3, 4))
    q5 = q5.reshape(n_seq, N_KV_HEADS, t_new * qpk, HEAD_DIM)
    q_m = jnp.concatenate(
        [jnp.pad(q5[:, g], ((0, 0), (0, 0), (g * HEAD_DIM, (N_KV_HEADS - 1 - g) * HEAD_DIM)))
         for g in range(N_KV_HEADS)], axis=1)
    sink_rows = jnp.tile(c['sinks'].reshape(N_KV_HEADS, 1, qpk), (1, t_new, 1)).reshape(-1, 1)
    o_m, win_k, win_v = _attention_dec(
        q_m, cache_k, to_seq(knew), cache_v, to_seq(vnew), sink_rows, layer=layer, t_new=t_new, bb=min(32, n_seq))
    o5 = o_m.reshape(n_seq, t_new, qpk, N_KV_HEADS, HEAD_DIM)
    oa = jnp.transpose(o5, (1, 0, 3, 2, 4)).reshape(rows, ATTN_WIDTH)
    win_k = win_k.reshape(n_seq, n_c, N_KV_HEADS, HEAD_DIM)
    win_v = win_v.reshape(n_seq, n_c, N_KV_HEADS, HEAD_DIM)

    y3, h_new = _ssd_dec(to_seq(xs), to_seq(bc), to_seq(dt), to_seq(z), state_all, layer, prev_states,
                         c['a_pad'], c['dskip'], c['gnorm'], shared['tri_b'], shared['ones_b'], shared['expand'],
                         bb=shared['bb_s'], t_new=t_new)
    ys = jnp.transpose(y3[:, :t_new], (1, 0, 2)).reshape(rows, SSM_WIDTH)

    xo = _outffn(x_t[0], oa, ys, None, shared['wo'], c['gffn'], shared['wgu'], shared['wd'], layer=c['layer'],
                 tm=min(512, rows), ffc=FF_CHUNKS)

    conv_new = jnp.transpose(convst[0].reshape(CONV_W - 1, n_seq, CONV_DIM), (1, 0, 2))
    return xo[None], win_k, win_v, conv_new, h_new


def kernel(x_prompt, x_sample, cache_win_k, cache_win_v, state_conv, state_ssm,
           norm_mix, w_in, q_norm, k_norm, sinks, conv_w, conv_b, dt_bias, a_log,
           d_skip, ssm_norm, w_out, norm_ffn, w_gate_up, w_down):
    p = dict(norm_mix=norm_mix, w_in=w_in, q_norm=q_norm, k_norm=k_norm, sinks=sinks, conv_w=conv_w,
             conv_b=conv_b, dt_bias=dt_bias, a_log=a_log, d_skip=d_skip, ssm_norm=ssm_norm, w_out=w_out,
             norm_ffn=norm_ffn, w_gate_up=w_gate_up, w_down=w_down)
    depth = w_in.shape[0]
    seq = x_prompt.shape[1]
    cos_p, sin_p = _rope_tables(np.arange(seq))
    half_blk = jnp.arange(LANES) // HEAD_DIM
    m128 = (jnp.where(half_blk[:, None] == half_blk[None, :], 1.0 / HEAD_DIM, 0.0)).astype(BF16)
    m128 = jnp.concatenate([m128, m128], axis=0)
    tri = (jnp.arange(SSD_CHUNK)[:, None] >= jnp.arange(SSD_CHUNK)[None, :]).astype(BF16)
    n_seq, t_new, _ = x_sample.shape
    pos_s = PAST_LEN + np.repeat(np.arange(t_new), n_seq)
    cos_s, sin_s = _rope_tables(pos_s)
    bb_s = min(8, n_seq)
    r_idx = jnp.arange(bb_s * T_PAD)
    same_seq = (r_idx[:, None] // T_PAD) == (r_idx[None, :] // T_PAD)
    tri_b = (same_seq & (r_idx[:, None] >= r_idx[None, :])).astype(BF16)
    ones_b = same_seq.astype(BF16)
    expand = (jnp.arange(LANES)[:, None] == (jnp.arange(SSM_WIDTH) // SSM_HEAD_DIM)[None, :]).astype(BF16)
    weights = dict(win=jnp.pad(w_in, ((0, 0), (0, 0), (0, PROJ_PAD - IN_PROJ_WIDTH))).astype(BF16),
                   wo=w_out.astype(BF16), wgu=w_gate_up.astype(BF16), wd=w_down.astype(BF16))
    shared = dict(cos_p=cos_p, sin_p=sin_p, cos_s=cos_s, sin_s=sin_s, m128=m128, tri=tri, tri_b=tri_b, **weights,
                  ones_b=ones_b, bb_s=bb_s, expand=expand)
    tm = min(512, seq)

    cache_k = cache_win_k.reshape(cache_win_k.shape[:3] + (KV_WIDTH,))
    cache_v = cache_win_v.reshape(cache_win_v.shape[:3] + (KV_WIDTH,))
    xp = x_prompt
    xs = jnp.transpose(x_sample, (1, 0, 2)).reshape(1, t_new * n_seq, D_MODEL)
    pk, pv, pc, ph = [], [], [], []
    sk, sv, sc, sh = [], [], [], []
    for l in range(depth):
        c = _layer_consts(l, p)
        xp, k1, v1, c1, h1 = _prompt_layer(xp, c, shared, tm=tm)
        prev = sh if l == depth - 1 else []
        xs, k2, v2, c2, h2 = _sample_layer(xs, cache_k, cache_v, state_conv[l], state_ssm, l, prev,
                                           c, shared, n_seq=n_seq, t_new=t_new)
        pk.append(k1); pv.append(v1); pc.append(c1); ph.append(h1)
        sk.append(k2); sv.append(v2); sc.append(c2); sh.append(h2)
    ys = jnp.transpose(xs.reshape(t_new, n_seq, D_MODEL), (1, 0, 2))
    return (xp, ys, jnp.stack(pk), jnp.stack(pv), jnp.stack(pc), jnp.stack(ph),
            jnp.stack(sk), jnp.stack(sv), jnp.stack(sc), sh[-1])
```

```python
import functools
import math

import numpy as np
import jax
import jax.numpy as jnp
from jax import lax
from jax.experimental import pallas as pl
from jax.experimental.pallas import tpu as pltpu

F32 = jnp.float32
BF16 = jnp.bfloat16

D_MODEL = 1024
HEAD_DIM = 64
N_Q_HEADS = 8
N_KV_HEADS = 2
ATTN_WIDTH = N_Q_HEADS * HEAD_DIM
KV_WIDTH = N_KV_HEADS * HEAD_DIM
WINDOW = 128
ROPE_THETA = 10000.0
ATTN_SCALE = HEAD_DIM ** -0.5
SSM_HEAD_DIM = 64
N_SSM_HEADS = 16
SSM_WIDTH = N_SSM_HEADS * SSM_HEAD_DIM
N_SSM_GROUPS = 2
D_STATE = 128
CONV_W = 4
CONV_DIM = SSM_WIDTH + 2 * N_SSM_GROUPS * D_STATE
SSD_CHUNK = 128
D_FF = 2816
EPS = 1e-6
LOG2E = math.log2(math.e)
PAST_LEN = 16384

LANES = 128
SUBLANES = 8

COL_Q = 0
COL_K = COL_Q + ATTN_WIDTH
COL_V = COL_K + KV_WIDTH
COL_Z = COL_V + KV_WIDTH
COL_XBC = COL_Z + SSM_WIDTH
COL_DT = COL_XBC + CONV_DIM
IN_PROJ_WIDTH = COL_DT + N_SSM_HEADS
PROJ_PAD = COL_DT + LANES
QKV_COLS = COL_Z

VMEM_LIMIT = 56 * 1024 * 1024
FF_CHUNKS = (6 * 256, 5 * 256)


def _dot(a, b):
    return jnp.dot(a, b, preferred_element_type=F32)


def _dot_nt(a, b):
    return lax.dot_general(a, b, (((1,), (1,)), ((), ())), preferred_element_type=F32)


def _silu(x):
    return x / (1.0 + jnp.exp(-x))


def _softplus(x):
    return jnp.maximum(x, 0.0) + jnp.log1p(jnp.exp(-jnp.abs(x)))


def _rms_rows(x, g):
    return x * lax.rsqrt(jnp.mean(x * x, axis=-1, keepdims=True) + EPS) * g


def _split2(x):
    hi = x.astype(BF16)
    lo = (x - hi.astype(F32)).astype(BF16)
    return hi, lo


def _split3(x):
    hi = x.astype(BF16)
    r1 = x - hi.astype(F32)
    mid = r1.astype(BF16)
    lo = (r1 - mid.astype(F32)).astype(BF16)
    return hi, mid, lo


def _head_norm_rope(x, m128, g, cos, sin_signed, low_half):
    hi, lo = _split2(x * x)
    ms = _dot(jnp.concatenate([hi, lo], axis=1), m128)
    xn = x * lax.rsqrt(ms + EPS) * g
    rot = jnp.where(low_half, pltpu.roll(xn, LANES - HEAD_DIM // 2, 1),
                    pltpu.roll(xn, HEAD_DIM // 2, 1))
    return xn * cos + rot * sin_signed


def _inproj_kernel(*refs, tm, carry_rows, shift, n_keep, has_state, n_parts):
    if has_state:
        (x_ref, gmix_ref, win_ref, m128_ref, qn_ref, kn_ref, cos_ref, sin_ref,
         convw_ref, convb_ref, dtb_ref, conv0_ref,
         q_ref, k2_ref, v2_ref, z_ref, xs_ref, bc_ref, dt_ref, knew_ref, vnew_ref, convst_ref,
         xbc_s) = refs
    else:
        (x_ref, gmix_ref, win_ref, m128_ref, qn_ref, kn_ref, cos_ref, sin_ref,
         convw_ref, convb_ref, dtb_ref,
         q_ref, k2_ref, v2_ref, z_ref, xs_ref, bc_ref, dt_ref, knew_ref, vnew_ref, convst_ref,
         xbc_s) = refs
        conv0_ref = None
    l = pl.program_id(1)

    @pl.when(l == 0)
    def _():
        if has_state:
            xbc_s[0:carry_rows, :] = conv0_ref[...]
        else:
            xbc_s[0:carry_rows, :] = jnp.zeros((carry_rows, CONV_DIM), F32)

    m128 = m128_ref[...]
    cw = 512
    nr = tm // n_parts
    lane = lax.broadcasted_iota(jnp.int32, (nr, LANES), 1)
    low_half = (lane % HEAD_DIM) < (HEAD_DIM // 2)
    for part in range(n_parts):
        r0 = part * nr
        rs = slice(r0, r0 + nr)
        base = carry_rows + r0
        u = _rms_rows(x_ref[0, rs, :], gmix_ref[...]).astype(BF16)
        for j in range(CONV_DIM // cw):
            c0 = COL_XBC + j * cw
            xbc_s[base:base + nr, j * cw:(j + 1) * cw] = _dot(u, win_ref[0, :, c0:c0 + cw])
        z_ref[0, rs, :] = _dot(u, win_ref[0, :, COL_Z:COL_Z + SSM_WIDTH])
        heads = _dot(u, win_ref[0, :, 0:QKV_COLS])
        dt_raw = _dot(u, win_ref[0, :, COL_DT:COL_DT + LANES])
        cos = cos_ref[rs, :]
        sin = sin_ref[rs, :]

        for j in range(ATTN_WIDTH // LANES):
            qj = heads[:, COL_Q + j * LANES:COL_Q + (j + 1) * LANES]
            qj = _head_norm_rope(qj, m128, qn_ref[...], cos, sin, low_half)
            q_ref[0, rs, j * LANES:(j + 1) * LANES] = (qj * (ATTN_SCALE * LOG2E)).astype(BF16)

        k = _head_norm_rope(heads[:, COL_K:COL_K + KV_WIDTH], m128, kn_ref[...], cos, sin, low_half)
        v = heads[:, COL_V:COL_V + KV_WIDTH]
        k2_ref[0, rs, 0:LANES] = k.astype(BF16)
        k2_ref[0, rs, LANES:2 * LANES] = pltpu.roll(k, HEAD_DIM, 1).astype(BF16)
        v2_ref[0, rs, 0:LANES] = v.astype(BF16)
        v2_ref[0, rs, LANES:2 * LANES] = pltpu.roll(v, HEAD_DIM, 1).astype(BF16)
        if part == n_parts - 1:
            knew_ref[0] = k[nr - n_keep:, :]
            vnew_ref[0] = v[nr - n_keep:, :]

        dt_ref[0, rs, :] = _softplus(dt_raw + dtb_ref[...])

        for j in range(CONV_DIM // cw):
            cs = slice(j * cw, (j + 1) * cw)
            w = [convw_ref[i:i + 1, cs] for i in range(CONV_W)]
            bias = convb_ref[:, cs]
            if shift % SUBLANES == 0:
                acc = bias
                for i in range(CONV_W):
                    off = base - (CONV_W - 1 - i) * shift
                    acc = acc + xbc_s[off:off + nr, cs] * w[i]
                act = _silu(acc)
            else:
                cur = xbc_s[base:base + nr, cs]
                acc = cur * w[0]
                for i in range(1, CONV_W):
                    acc = pltpu.roll(acc, shift, 0) + cur * w[i]
                head = bias
                for i in range(CONV_W):
                    off = base - (CONV_W - 1 - i) * shift
                    head = head + xbc_s[off:off + SUBLANES, cs] * w[i]
                act = _silu(jnp.concatenate([head, acc[SUBLANES:, :] + bias], axis=0))
            if j * cw < SSM_WIDTH:
                xs_ref[0, rs, cs] = act
            else:
                bc_ref[0, rs, :] = act.astype(BF16)

    convst_ref[0] = xbc_s[tm:tm + carry_rows, :]
    xbc_s[0:carry_rows, :] = xbc_s[tm:tm + carry_rows, :]


def _inproj(x, gmix, win, m128, qn, kn, cos, sin, convw, convb, dtb, conv0, *, layer, tm, carry_rows, shift,
            n_keep):
    nb, seq, _ = x.shape
    n_l = seq // tm
    n_parts = 1
    has_state = conv0 is not None
    const = lambda shape: pl.BlockSpec(shape, lambda b, l: (0,) * len(shape))
    row = lambda w: pl.BlockSpec((1, tm, w), lambda b, l: (b, l, 0))
    in_specs = [row(D_MODEL), const((1, D_MODEL)),
                pl.BlockSpec((1, D_MODEL, PROJ_PAD), lambda b, l: (layer, 0, 0)), const((2 * LANES, LANES)),
                const((1, LANES)), const((1, LANES)),
                pl.BlockSpec((tm, LANES), lambda b, l: (l, 0)), pl.BlockSpec((tm, LANES), lambda b, l: (l, 0)),
                const((CONV_W, CONV_DIM)), const((1, CONV_DIM)), const((1, LANES))]
    args = [x, gmix, win, m128, qn, kn, cos, sin, convw, convb, dtb]
    if has_state:
        in_specs.append(const((carry_rows, CONV_DIM)))
        args.append(conv0)
    last = lambda rows, w: pl.BlockSpec((1, rows, w), lambda b, l: (b, 0, 0))
    out_specs = [row(ATTN_WIDTH), row(2 * KV_WIDTH), row(2 * KV_WIDTH), row(SSM_WIDTH), row(SSM_WIDTH),
                 row(CONV_DIM - SSM_WIDTH), row(LANES),
                 last(n_keep, KV_WIDTH), last(n_keep, KV_WIDTH), last(carry_rows, CONV_DIM)]
    out_shape = [jax.ShapeDtypeStruct((nb, seq, ATTN_WIDTH), BF16),
                 jax.ShapeDtypeStruct((nb, seq, 2 * KV_WIDTH), BF16),
                 jax.ShapeDtypeStruct((nb, seq, 2 * KV_WIDTH), BF16),
                 jax.ShapeDtypeStruct((nb, seq, SSM_WIDTH), F32),
                 jax.ShapeDtypeStruct((nb, seq, SSM_WIDTH), F32),
                 jax.ShapeDtypeStruct((nb, seq, CONV_DIM - SSM_WIDTH), BF16),
                 jax.ShapeDtypeStruct((nb, seq, LANES), F32),
                 jax.ShapeDtypeStruct((nb, n_keep, KV_WIDTH), F32),
                 jax.ShapeDtypeStruct((nb, n_keep, KV_WIDTH), F32),
                 jax.ShapeDtypeStruct((nb, carry_rows, CONV_DIM), F32)]
    kern = functools.partial(_inproj_kernel, tm=tm, carry_rows=carry_rows, shift=shift, n_keep=n_keep,
                             has_state=has_state, n_parts=n_parts)
    return pl.pallas_call(
        kern, grid=(nb, n_l), in_specs=in_specs, out_specs=out_specs, out_shape=out_shape,
        scratch_shapes=[pltpu.VMEM((carry_rows + tm, CONV_DIM), F32)],
        compiler_params=pltpu.CompilerParams(dimension_semantics=("arbitrary", "arbitrary"),
                                             vmem_limit_bytes=VMEM_LIMIT),
        name="inproj",
    )(*args)


def _attn_kernel(sinks_ref, q_ref, kc_ref, kp_ref, vc_ref, vp_ref, o_ref, *, tq):
    i = pl.program_id(1)
    nblk = tq // WINDOW
    lane = lax.broadcasted_iota(jnp.int32, (WINDOW, LANES), 1)
    low = lane < HEAD_DIM
    qi = lax.broadcasted_iota(jnp.int32, (WINDOW, 2 * WINDOW), 0)
    kj = lax.broadcasted_iota(jnp.int32, (WINDOW, 2 * WINDOW), 1)
    diff = qi + WINDOW - kj
    band = (diff >= 0) & (diff < WINDOW)
    zero_bf = jnp.zeros((WINDOW, LANES), BF16)
    for blk in range(nblk):
        r0 = blk * WINDOW
        if blk == 0:
            k_prev, v_prev = kp_ref[0], vp_ref[0]
            mask = band & ((i > 0) | (kj >= WINDOW))
        else:
            k_prev = kc_ref[0, r0 - WINDOW:r0, :]
            v_prev = vc_ref[0, r0 - WINDOW:r0, :]
            mask = band
        kk = jnp.concatenate([k_prev, kc_ref[0, r0:r0 + WINDOW, :]], axis=0)
        vv = jnp.concatenate([v_prev, vc_ref[0, r0:r0 + WINDOW, :]], axis=0)
        for pair in range(N_Q_HEADS // 2):
            g = (2 * pair) // (N_Q_HEADS // N_KV_HEADS)
            qp = q_ref[0, r0:r0 + WINDOW, pair * LANES:(pair + 1) * LANES]
            outs = []
            for e in range(2):
                h = 2 * pair + e
                var = 0 if g == e else 1
                qh = jnp.where(low if e == 0 else ~low, qp, zero_bf)
                s = _dot_nt(qh, kk[:, var * LANES:(var + 1) * LANES])
                s = jnp.where(mask, s, -jnp.inf)
                sk = sinks_ref[h] * LOG2E
                m = jnp.maximum(jnp.max(s, axis=-1, keepdims=True), sk)
                p = jnp.exp2(s - m)
                den = jnp.sum(p, axis=-1, keepdims=True) + jnp.exp2(sk - m)
                o = _dot(p.astype(BF16), vv[:, var * LANES:(var + 1) * LANES])
                outs.append(o / den)
            o_ref[0, r0:r0 + WINDOW, pair * LANES:(pair + 1) * LANES] = (
                jnp.where(low, outs[0], outs[1]).astype(BF16))


def _attention(sinks, q, k2, v2, *, tq):
    nb, seq, _ = q.shape
    r = tq // WINDOW
    cur = lambda w: pl.BlockSpec((1, tq, w), lambda b, i: (b, i, 0))
    prev = lambda w: pl.BlockSpec((1, WINDOW, w), lambda b, i: (b, jnp.maximum(i * r - 1, 0), 0))
    return pl.pallas_call(
        functools.partial(_attn_kernel, tq=tq),
        grid=(nb, seq // tq),
        in_specs=[pl.BlockSpec(memory_space=pltpu.SMEM),
                  cur(ATTN_WIDTH), cur(2 * KV_WIDTH), prev(2 * KV_WIDTH), cur(2 * KV_WIDTH), prev(2 * KV_WIDTH)],
        out_specs=cur(ATTN_WIDTH),
        out_shape=jax.ShapeDtypeStruct((nb, seq, ATTN_WIDTH), BF16),
        compiler_params=pltpu.CompilerParams(dimension_semantics=("arbitrary", "arbitrary"),
                                             vmem_limit_bytes=VMEM_LIMIT),
        name="swa_attn",
    )(sinks, q, k2, k2, v2, v2)


def _ssd_kernel(xs_ref, bc_ref, dt_ref, a_ref, dskip_ref, tri_ref,
                y_ref, hT_ref, h_s, *, tc):
    l = pl.program_id(1)
    n_l = pl.num_programs(1)
    gw = SSM_WIDTH // N_SSM_GROUPS
    hpg = N_SSM_HEADS // N_SSM_GROUPS

    @pl.when(l == 0)
    def _():
        h_s[...] = jnp.zeros(h_s.shape, F32)

    lane = lax.broadcasted_iota(jnp.int32, (SSD_CHUNK, LANES), 1)
    low = lane < SSM_HEAD_DIM
    qi = lax.broadcasted_iota(jnp.int32, (SSD_CHUNK, SSD_CHUNK), 0)
    sj = lax.broadcasted_iota(jnp.int32, (SSD_CHUNK, SSD_CHUNK), 1)
    causal = sj <= qi
    a_neg = -jnp.exp(a_ref[...])
    tri = tri_ref[...]
    zero_bf = jnp.zeros((SSD_CHUNK, LANES), BF16)

    for c in range(tc // SSD_CHUNK):
        rows = slice(c * SSD_CHUNK, (c + 1) * SSD_CHUNK)
        dt = dt_ref[0, rows, :]
        hi, mid, lo = _split3(dt * a_neg)
        acum = (_dot(tri, hi) + _dot(tri, mid) + _dot(tri, lo)) * LOG2E
        acum_t = acum.T
        dt_t = dt.T
        a_end_t = acum_t[:, SSD_CHUNK - 1:SSD_CHUNK]
        w_end_t = dt_t * jnp.exp2(a_end_t - acum_t)
        dec_t = jnp.exp2(a_end_t)
        dec_t = jnp.broadcast_to(dec_t, (LANES, LANES))
        row_t = acum_t - jnp.log2(dt_t)
        for g in range(N_SSM_GROUPS):
            b_g = bc_ref[0, rows, g * D_STATE:(g + 1) * D_STATE]
            c_g = bc_ref[0, rows, (N_SSM_GROUPS + g) * D_STATE:(N_SSM_GROUPS + g + 1) * D_STATE]
            cb = _dot_nt(c_g, b_g)
            b_gt = b_g.astype(F32).T
            y_inter = _dot(c_g, h_s[g].astype(BF16))
            for pr in range(hpg // 2):
                lhs_y, lhs_s, escale, dsc = [], [], [], []
                for e in range(2):
                    h = g * hpg + 2 * pr + e
                    col = jnp.broadcast_to(acum[:, h:h + 1], (SSD_CHUNK, SSD_CHUNK))
                    decay_dt = jnp.where(causal, jnp.exp2(col - row_t[h:h + 1, :]), 0.0)
                    lhs_y.append((cb * decay_dt).astype(BF16))
                    lhs_s.append((b_gt * w_end_t[h:h + 1, :]).astype(BF16))
                    escale.append(jnp.exp2(col))
                    dsc.append(dec_t[h:h + 1, :])
                c0 = g * gw + pr * LANES
                x_pair = xs_ref[0, rows, c0:c0 + LANES]
                x_bf = x_pair.astype(BF16)
                rhs = jnp.concatenate([jnp.where(low, x_bf, zero_bf), jnp.where(low, zero_bf, x_bf)], axis=0)
                y_in = _dot(jnp.concatenate(lhs_y, axis=1), rhs)
                st = _dot(jnp.concatenate(lhs_s, axis=1), rhs)
                lc = pr * LANES
                y = y_in + y_inter[:, lc:lc + LANES] * jnp.where(low, escale[0], escale[1])
                y_ref[0, rows, c0:c0 + LANES] = y + dskip_ref[:, c0:c0 + LANES] * x_pair
                h_s[g, :, lc:lc + LANES] = h_s[g, :, lc:lc + LANES] * jnp.where(low, dsc[0], dsc[1]) + st

    @pl.when(l == n_l - 1)
    def _():
        hT_ref[0] = h_s[...]


def _ssd(xs, bc, dt, a_pad, dskip, tri, *, tc):
    nb, seq, _ = xs.shape
    row = lambda w: pl.BlockSpec((1, tc, w), lambda b, l: (b, l, 0))
    const = lambda shape: pl.BlockSpec(shape, lambda b, l: (0,) * len(shape))
    gw = SSM_WIDTH // N_SSM_GROUPS
    return pl.pallas_call(
        functools.partial(_ssd_kernel, tc=tc),
        grid=(nb, seq // tc),
        in_specs=[row(SSM_WIDTH), row(CONV_DIM - SSM_WIDTH), row(LANES),
                  const((1, LANES)), const((1, SSM_WIDTH)), const((SSD_CHUNK, SSD_CHUNK))],
        out_specs=[row(SSM_WIDTH),
                   pl.BlockSpec((1, N_SSM_GROUPS, D_STATE, gw), lambda b, l: (b, 0, 0, 0))],
        out_shape=[jax.ShapeDtypeStruct((nb, seq, SSM_WIDTH), F32),
                   jax.ShapeDtypeStruct((nb, N_SSM_GROUPS, D_STATE, gw), F32)],
        scratch_shapes=[pltpu.VMEM((N_SSM_GROUPS, D_STATE, gw), F32)],
        compiler_params=pltpu.CompilerParams(dimension_semantics=("arbitrary", "arbitrary"),
                                             vmem_limit_bytes=VMEM_LIMIT),
        name="ssd_scan",
    )(xs, bc, dt, a_pad, dskip, tri)


def _outffn_kernel(*refs, ffc, gated):
    if gated:
        x_ref, oa_ref, ys_ref, z_ref, gnorm_ref, wo_ref, gffn_ref, wgu_ref, wd_ref, o_ref = refs
        gw = SSM_WIDTH // N_SSM_GROUPS
        parts = []
        for g in range(N_SSM_GROUPS):
            ls = slice(g * gw, (g + 1) * gw)
            yg = ys_ref[:, ls] * _silu(z_ref[:, ls])
            yg = yg * lax.rsqrt(jnp.mean(yg * yg, axis=-1, keepdims=True) + EPS)
            parts.append((yg * gnorm_ref[:, ls]).astype(BF16))
        ys = jnp.concatenate(parts, axis=1)
    else:
        x_ref, oa_ref, ys_ref, wo_ref, gffn_ref, wgu_ref, wd_ref, o_ref = refs
        ys = ys_ref[...]
    x = x_ref[...]
    xm = x + _dot(oa_ref[...], wo_ref[0, 0:ATTN_WIDTH, :]) + _dot(ys, wo_ref[0, ATTN_WIDTH:, :])
    hn = _rms_rows(xm, gffn_ref[...]).astype(BF16)
    acc = jnp.zeros_like(xm)
    c0 = 0
    for width in ffc:
        g = _dot(hn, wgu_ref[0, :, c0:c0 + width])
        up = _dot(hn, wgu_ref[0, :, D_FF + c0:D_FF + c0 + width])
        acc = acc + _dot((_silu(g) * up).astype(BF16), wd_ref[0, c0:c0 + width, :])
        c0 += width
    o_ref[...] = xm + acc


def _outffn(x, oa, ys, gate, wo, gffn, wgu, wd, *, layer, tm, ffc):
    rows = x.shape[0]
    row = lambda w: pl.BlockSpec((tm, w), lambda i: (i, 0))
    const = lambda shape: pl.BlockSpec(shape, lambda i: (0,) * len(shape), pipeline_mode=pl.Buffered(1))
    gated = gate is not None
    in_specs = [row(D_MODEL), row(ATTN_WIDTH), row(SSM_WIDTH)]
    args = [x, oa, ys]
    if gated:
        in_specs += [row(SSM_WIDTH), const((1, SSM_WIDTH))]
        args += list(gate)
    wblk = lambda r, w: pl.BlockSpec((1, r, w), lambda i: (layer, 0, 0), pipeline_mode=pl.Buffered(1))
    in_specs += [wblk(ATTN_WIDTH + SSM_WIDTH, D_MODEL), const((1, D_MODEL)),
                 wblk(D_MODEL, 2 * D_FF), wblk(D_FF, D_MODEL)]
    args += [wo, gffn, wgu, wd]
    return pl.pallas_call(
        functools.partial(_outffn_kernel, ffc=ffc, gated=gated),
        grid=(rows // tm,),
        in_specs=in_specs,
        out_specs=row(D_MODEL),
        out_shape=jax.ShapeDtypeStruct((rows, D_MODEL), F32),
        compiler_params=pltpu.CompilerParams(dimension_semantics=("arbitrary",),
                                             vmem_limit_bytes=VMEM_LIMIT),
        name="outproj_ffn",
    )(*args)


T_PAD = SUBLANES


def _attn_dec_kernel(q_ref, kc_ref, kn_ref, vc_ref, vn_ref, sink_ref, o_ref, wk_ref, wv_ref, *, t_new):
    q = q_ref[...]
    bb, nq, _ = q.shape
    qpk = N_Q_HEADS // N_KV_HEADS
    n_c = kc_ref.shape[2]
    kc, vc, kn, vn = kc_ref[0], vc_ref[0], kn_ref[...], vn_ref[...]
    s_c = jnp.einsum('bqd,bkd->bqk', q, kc.astype(BF16), preferred_element_type=F32)
    s_n = jnp.einsum('bqd,bkd->bqk', q, kn.astype(BF16), preferred_element_type=F32)
    t_c = (lax.broadcasted_iota(jnp.int32, (bb, nq, n_c), 1) % (t_new * qpk)) // qpk
    j_c = lax.broadcasted_iota(jnp.int32, (bb, nq, n_c), 2)
    s_c = jnp.where(j_c > t_c + (n_c - WINDOW), s_c, -jnp.inf)
    t_n = (lax.broadcasted_iota(jnp.int32, (bb, nq, T_PAD), 1) % (t_new * qpk)) // qpk
    j_n = lax.broadcasted_iota(jnp.int32, (bb, nq, T_PAD), 2)
    s_n = jnp.where((j_n <= t_n) & (j_n < t_new), s_n, -jnp.inf)
    sk = sink_ref[...] * LOG2E
    m = jnp.maximum(jnp.maximum(jnp.max(s_c, axis=-1, keepdims=True), jnp.max(s_n, axis=-1, keepdims=True)), sk)
    p_c = jnp.exp2(s_c - m)
    p_n = jnp.exp2(s_n - m)
    den = jnp.sum(p_c, axis=-1, keepdims=True) + jnp.sum(p_n, axis=-1, keepdims=True) + jnp.exp2(sk - m)
    o = (jnp.einsum('bqk,bkd->bqd', p_c.astype(BF16), vc.astype(BF16), preferred_element_type=F32)
         + jnp.einsum('bqk,bkd->bqd', p_n.astype(BF16), vn.astype(BF16), preferred_element_type=F32))
    o = o / den
    rows_g = nq // N_KV_HEADS
    lane = lax.broadcasted_iota(jnp.int32, (bb, rows_g, LANES), 2)
    merged = o[:, 0:rows_g, :]
    for g in range(1, N_KV_HEADS):
        merged = jnp.where(lane // HEAD_DIM == g, o[:, g * rows_g:(g + 1) * rows_g, :], merged)
    o_ref[...] = merged.astype(BF16)

    sub = lax.broadcasted_iota(jnp.int32, (bb, T_PAD, LANES), 1)
    for c, n, w_ref in ((kc, kn, wk_ref), (vc, vn, wv_ref)):
        shifted = pltpu.roll(c.reshape(bb * n_c, LANES), bb * n_c - t_new, 0).reshape(bb, n_c, LANES)
        tail = pltpu.roll(n.reshape(bb * T_PAD, LANES), T_PAD - t_new, 0).reshape(bb, T_PAD, LANES)
        w_ref[:, 0:n_c - T_PAD, :] = shifted[:, 0:n_c - T_PAD, :]
        w_ref[:, n_c - T_PAD:, :] = jnp.where(sub < T_PAD - t_new, shifted[:, n_c - T_PAD:, :], tail)


def _attention_dec(q, kc, kn, vc, vn, sink_rows, *, layer, t_new, bb):
    n_seq, nq, w = q.shape
    n_c = kc.shape[2]
    blk = lambda r: pl.BlockSpec((bb, r, w), lambda i: (i, 0, 0))
    cache = pl.BlockSpec((1, bb, n_c, w), lambda i: (layer, i, 0, 0))
    win = jax.ShapeDtypeStruct((n_seq, n_c, w), F32)
    return pl.pallas_call(
        functools.partial(_attn_dec_kernel, t_new=t_new),
        grid=(n_seq // bb,),
        in_specs=[blk(nq), cache, blk(T_PAD), cache, blk(T_PAD), pl.BlockSpec((nq, 1), lambda i: (0, 0))],
        out_specs=[blk(nq // N_KV_HEADS), blk(n_c), blk(n_c)],
        out_shape=[jax.ShapeDtypeStruct((n_seq, nq // N_KV_HEADS, w), BF16), win, win],
        compiler_params=pltpu.CompilerParams(dimension_semantics=("arbitrary",), vmem_limit_bytes=VMEM_LIMIT),
        name="swa_attn_decode",
    )(q, kc, kn, vc, vn, sink_rows)


def _ssd_dec_kernel(*refs, bb, t_new, n_prev):
    (xs_ref, bc_ref, dt_ref, z_ref, h0_ref) = refs[:5]
    prev_refs = refs[5:5 + n_prev]
    (a_ref, dskip_ref, gnorm_ref, tri_ref, ones_ref, exp_ref,
     y_ref, h_ref, aw_s, xdt_s, xw_s, ea_s, dec_s) = refs[5 + n_prev:]
    d = pl.program_id(0)
    gw = SSM_WIDTH // N_SSM_GROUPS
    hpg = N_SSM_HEADS // N_SSM_GROUPS
    rows = bb * T_PAD

    for j in range(n_prev):
        @pl.when(d == j)
        def _(j=j):
            h_ref[0] = prev_refs[j][0]

    @pl.when(d == n_prev)
    def _():
        expand = exp_ref[...]

        def widen(v):
            hi, mid, lo = _split3(v)
            return _dot(hi, expand) + _dot(mid, expand) + _dot(lo, expand)

        def seq_sum(m, v):
            hi, mid, lo = _split3(v)
            return _dot(m, hi) + _dot(m, mid) + _dot(m, lo)

        dt = dt_ref[...].reshape(rows, LANES)
        dta = dt * (-jnp.exp(a_ref[...]))
        a_w = widen(seq_sum(tri_ref[...], dta))
        a_end = seq_sum(ones_ref[...], dta)
        xdt = xs_ref[...].reshape(rows, SSM_WIDTH) * widen(dt)
        aw_s[...] = a_w
        xdt_s[...] = xdt
        xw_s[...] = xdt * jnp.exp(widen(a_end) - a_w)
        ea_s[...] = jnp.exp(a_w)
        dec_s[...] = jnp.exp(a_end)

        t_row = lax.broadcasted_iota(jnp.int32, (T_PAD, gw), 0)

        def seq_body(i, carry):
            r0 = pl.multiple_of(i * T_PAD, T_PAD)
            rs = pl.ds(r0, T_PAD)
            dec = jnp.broadcast_to(dec_s[pl.ds(r0, 1), :], (LANES, LANES)).T
            bc = bc_ref[i]
            for g in range(N_SSM_GROUPS):
                b_g = bc[:, g * D_STATE:(g + 1) * D_STATE]
                c_g = bc[:, (N_SSM_GROUPS + g) * D_STATE:(N_SSM_GROUPS + g + 1) * D_STATE]
                cb = _dot_nt(c_g, b_g)
                ls = slice(g * gw, (g + 1) * gw)
                a_g = aw_s[rs, ls]
                xdt_g = xdt_s[rs, ls]
                y = jnp.zeros((T_PAD, gw), F32)
                for s in range(t_new):
                    w = jnp.where(t_row >= s, jnp.exp(a_g - a_g[s:s + 1, :]), 0.0)
                    y = y + (w * cb[:, s:s + 1]) * xdt_g[s:s + 1, :]
                h0g = h0_ref[0, i, g * hpg:(g + 1) * hpg].reshape(gw, D_STATE)
                y = y + _dot_nt(c_g, h0g.astype(BF16)) * ea_s[rs, ls]
                y = y + dskip_ref[:, ls] * xs_ref[i, :, ls]
                y = y * _silu(z_ref[i, :, ls])
                y = y * lax.rsqrt(jnp.mean(y * y, axis=-1, keepdims=True) + EPS)
                y_ref[i, :, ls] = (y * gnorm_ref[:, ls]).astype(BF16)
                delta = lax.dot_general(xw_s[rs, ls].astype(BF16), b_g, (((0,), (0,)), ((), ())),
                                        preferred_element_type=F32)
                for hl in range(hpg):
                    h = g * hpg + hl
                    h_ref[0, i, h] = (h0_ref[0, i, h] * dec[h:h + 1, :]
                                      + delta[hl * SSM_HEAD_DIM:(hl + 1) * SSM_HEAD_DIM, :])
            return carry

        lax.fori_loop(0, bb, seq_body, 0, unroll=True)


def _ssd_dec(xs3, bc3, dt3, z3, state_all, layer, prev_states, a_pad, dskip, gnorm, tri_b, ones_b, expand,
             *, bb, t_new):
    nb = xs3.shape[0]
    nblk = nb // bb
    n_prev = len(prev_states)
    rows = bb * T_PAD

    def phase_block(j):
        return lambda d, i: jnp.clip(i + (d - j) * nblk, 0, nblk - 1)

    cur = phase_block(n_prev)
    blk = lambda w: pl.BlockSpec((bb, T_PAD, w), lambda d, i: (cur(d, i), 0, 0))
    const = lambda shape: pl.BlockSpec(shape, lambda d, i: (0,) * len(shape))
    st_shape = (1, bb, N_SSM_HEADS, SSM_HEAD_DIM, D_STATE)
    in_specs = [blk(SSM_WIDTH), blk(CONV_DIM - SSM_WIDTH), blk(LANES), blk(SSM_WIDTH),
                pl.BlockSpec(st_shape, lambda d, i: (layer, cur(d, i), 0, 0, 0))]
    for j in range(n_prev):
        in_specs.append(pl.BlockSpec(st_shape, lambda d, i, j=j: (0, phase_block(j)(d, i), 0, 0, 0)))
    in_specs += [const((1, LANES)), const((1, SSM_WIDTH)), const((1, SSM_WIDTH)),
                 const((rows, rows)), const((rows, rows)), const((LANES, SSM_WIDTH))]
    return pl.pallas_call(
        functools.partial(_ssd_dec_kernel, bb=bb, t_new=t_new, n_prev=n_prev),
        grid=(n_prev + 1, nblk),
        in_specs=in_specs,
        out_specs=[blk(SSM_WIDTH), pl.BlockSpec(st_shape, lambda d, i: (d, i, 0, 0, 0))],
        out_shape=[jax.ShapeDtypeStruct((nb, T_PAD, SSM_WIDTH), BF16),
                   jax.ShapeDtypeStruct((n_prev + 1,) + state_all.shape[1:], F32)],
        scratch_shapes=[pltpu.VMEM((rows, SSM_WIDTH), F32), pltpu.VMEM((rows, SSM_WIDTH), F32),
                        pltpu.VMEM((rows, SSM_WIDTH), F32), pltpu.VMEM((rows, SSM_WIDTH), F32),
                        pltpu.VMEM((rows, LANES), F32)],
        compiler_params=pltpu.CompilerParams(dimension_semantics=("arbitrary", "arbitrary"),
                                             vmem_limit_bytes=VMEM_LIMIT),
        name="ssd_decode",
    )(xs3, bc3, dt3, z3, state_all, *prev_states, a_pad, dskip, gnorm, tri_b, ones_b, expand)


def _rope_tables(pos):
    half = HEAD_DIM // 2
    inv = ROPE_THETA ** (-np.arange(half, dtype=np.float64) / half)
    ang = np.asarray(pos, np.float64)[:, None] * inv[None, :]
    cos = np.tile(np.cos(ang), (1, LANES // half))
    sin = np.sin(ang)
    sin_signed = np.tile(np.concatenate([-sin, sin], axis=1), (1, LANES // HEAD_DIM))
    return jnp.asarray(cos, F32), jnp.asarray(sin_signed, F32)


def _layer_consts(l, p):
    return dict(
        layer=l,
        gmix=p['norm_mix'][l][None, :],
        qn=jnp.tile(p['q_norm'][l], LANES // HEAD_DIM)[None, :],
        kn=jnp.tile(p['k_norm'][l], LANES // HEAD_DIM)[None, :],
        convw=p['conv_w'][l],
        convb=p['conv_b'][l][None, :],
        dtb=jnp.pad(p['dt_bias'][l], (0, LANES - N_SSM_HEADS))[None, :],
        a_pad=jnp.pad(p['a_log'][l], (0, LANES - N_SSM_HEADS))[None, :],
        dskip=jnp.repeat(p['d_skip'][l], SSM_HEAD_DIM)[None, :],
        gnorm=p['ssm_norm'][l][None, :],
        sinks=p['sinks'][l],
        gffn=p['norm_ffn'][l][None, :],
    )


def _state_from_t(h_t):
    nb = h_t.shape[0]
    hpg = N_SSM_HEADS // N_SSM_GROUPS
    h = h_t.reshape(nb, N_SSM_GROUPS, D_STATE, hpg, SSM_HEAD_DIM)
    return jnp.transpose(h, (0, 1, 3, 4, 2)).reshape(nb, N_SSM_HEADS, SSM_HEAD_DIM, D_STATE)


def _prompt_layer(x, c, shared, *, tm):
    nb, seq, _ = x.shape
    (q, k2, v2, z, xs, bc, dt, knew, vnew, convst) = _inproj(
        x, c['gmix'], shared['win'], shared['m128'], c['qn'], c['kn'], shared['cos_p'], shared['sin_p'],
        c['convw'], c['convb'], c['dtb'], None, layer=c['layer'], tm=tm, carry_rows=SUBLANES, shift=1, n_keep=min(WINDOW, seq))
    mix_rows = next(t for t in (4 * tm, 2 * tm, tm) if seq % t == 0)
    oa = _attention(c['sinks'], q, k2, v2, tq=mix_rows)
    ys, h_t = _ssd(xs, bc, dt, c['a_pad'], c['dskip'], shared['tri'], tc=mix_rows)
    rows = nb * seq
    xo = _outffn(x.reshape(rows, D_MODEL), oa.reshape(rows, ATTN_WIDTH), ys.reshape(rows, SSM_WIDTH),
                 (z.reshape(rows, SSM_WIDTH), c['gnorm']), shared['wo'], c['gffn'], shared['wgu'], shared['wd'],
                 layer=c['layer'], tm=tm,
                 ffc=FF_CHUNKS)
    n_keep = knew.shape[1]
    return (xo.reshape(nb, seq, D_MODEL),
            knew.reshape(nb, n_keep, N_KV_HEADS, HEAD_DIM), vnew.reshape(nb, n_keep, N_KV_HEADS, HEAD_DIM),
            convst[:, SUBLANES - (CONV_W - 1):, :], _state_from_t(h_t))


def _sample_layer(x_t, cache_k, cache_v, conv0, state_all, layer, prev_states, c, shared, *, n_seq, t_new):
    rows = t_new * n_seq
    qpk = N_Q_HEADS // N_KV_HEADS
    n_c = cache_k.shape[2]
    conv0_t = jnp.transpose(conv0, (1, 0, 2)).reshape((CONV_W - 1) * n_seq, CONV_DIM)
    (q, _, _, z, xs, bc, dt, knew, vnew, convst) = _inproj(
        x_t, c['gmix'], shared['win'], shared['m128'], c['qn'], c['kn'], shared['cos_s'], shared['sin_s'],
        c['convw'], c['convb'], c['dtb'], conv0_t, layer=c['layer'], tm=rows, carry_rows=(CONV_W - 1) * n_seq, shift=n_seq,
        n_keep=rows)

    def to_seq(a):
        a = jnp.transpose(a[0].reshape(t_new, n_seq, a.shape[-1]), (1, 0, 2))
        return jnp.pad(a, ((0, 0), (0, T_PAD - t_new), (0, 0)))

    q5 = jnp.transpose(q[0].reshape(t_new, n_seq, N_KV_HEADS, qpk, HEAD_DIM), (1, 2, 0, 3, 4))
    q5 = q5.reshape(n_seq, N_KV_HEADS, t_new * qpk, HEAD_DIM)
    q_m = jnp.concatenate(
        [jnp.pad(q5[:, g], ((0, 0), (0, 0), (g * HEAD_DIM, (N_KV_HEADS - 1 - g) * HEAD_DIM)))
         for g in range(N_KV_HEADS)], axis=1)
    sink_rows = jnp.tile(c['sinks'].reshape(N_KV_HEADS, 1, qpk), (1, t_new, 1)).reshape(-1, 1)
    o_m, win_k, win_v = _attention_dec(
        q_m, cache_k, to_seq(knew), cache_v, to_seq(vnew), sink_rows, layer=layer, t_new=t_new, bb=min(32, n_seq))
    o5 = o_m.reshape(n_seq, t_new, qpk, N_KV_HEADS, HEAD_DIM)
    oa = jnp.transpose(o5, (1, 0, 3, 2, 4)).reshape(rows, ATTN_WIDTH)
    win_k = win_k.reshape(n_seq, n_c, N_KV_HEADS, HEAD_DIM)
    win_v = win_v.reshape(n_seq, n_c, N_KV_HEADS, HEAD_DIM)

    y3, h_new = _ssd_dec(to_seq(xs), to_seq(bc), to_seq(dt), to_seq(z), state_all, layer, prev_states,
                         c['a_pad'], c['dskip'], c['gnorm'], shared['tri_b'], shared['ones_b'], shared['expand'],
                         bb=shared['bb_s'], t_new=t_new)
    ys = jnp.transpose(y3[:, :t_new], (1, 0, 2)).reshape(rows, SSM_WIDTH)

    xo = _outffn(x_t[0], oa, ys, None, shared['wo'], c['gffn'], shared['wgu'], shared['wd'], layer=c['layer'],
                 tm=min(512, rows), ffc=FF_CHUNKS)

    conv_new = jnp.transpose(convst[0].reshape(CONV_W - 1, n_seq, CONV_DIM), (1, 0, 2))
    return xo[None], win_k, win_v, conv_new, h_new


def kernel(x_prompt, x_sample, cache_win_k, cache_win_v, state_conv, state_ssm,
           norm_mix, w_in, q_norm, k_norm, sinks, conv_w, conv_b, dt_bias, a_log,
           d_skip, ssm_norm, w_out, norm_ffn, w_gate_up, w_down):
    p = dict(norm_mix=norm_mix, w_in=w_in, q_norm=q_norm, k_norm=k_norm, sinks=sinks, conv_w=conv_w,
             conv_b=conv_b, dt_bias=dt_bias, a_log=a_log, d_skip=d_skip, ssm_norm=ssm_norm, w_out=w_out,
             norm_ffn=norm_ffn, w_gate_up=w_gate_up, w_down=w_down)
    depth = w_in.shape[0]
    seq = x_prompt.shape[1]
    cos_p, sin_p = _rope_tables(np.arange(seq))
    half_blk = jnp.arange(LANES) // HEAD_DIM
    m128 = (jnp.where(half_blk[:, None] == half_blk[None, :], 1.0 / HEAD_DIM, 0.0)).astype(BF16)
    m128 = jnp.concatenate([m128, m128], axis=0)
    tri = (jnp.arange(SSD_CHUNK)[:, None] >= jnp.arange(SSD_CHUNK)[None, :]).astype(BF16)
    n_seq, t_new, _ = x_sample.shape
    pos_s = PAST_LEN + np.repeat(np.arange(t_new), n_seq)
    cos_s, sin_s = _rope_tables(pos_s)
    bb_s = min(8, n_seq)
    r_idx = jnp.arange(bb_s * T_PAD)
    same_seq = (r_idx[:, None] // T_PAD) == (r_idx[None, :] // T_PAD)
    tri_b = (same_seq & (r_idx[:, None] >= r_idx[None, :])).astype(BF16)
    ones_b = same_seq.astype(BF16)
    expand = (jnp.arange(LANES)[:, None] == (jnp.arange(SSM_WIDTH) // SSM_HEAD_DIM)[None, :]).astype(BF16)
    weights = dict(win=jnp.pad(w_in, ((0, 0), (0, 0), (0, PROJ_PAD - IN_PROJ_WIDTH))).astype(BF16),
                   wo=w_out.astype(BF16), wgu=w_gate_up.astype(BF16), wd=w_down.astype(BF16))
    shared = dict(cos_p=cos_p, sin_p=sin_p, cos_s=cos_s, sin_s=sin_s, m128=m128, tri=tri, tri_b=tri_b, **weights,
                  ones_b=ones_b, bb_s=bb_s, expand=expand)
    tm = min(512, seq)

    cache_k = cache_win_k.reshape(cache_win_k.shape[:3] + (KV_WIDTH,))
    cache_v = cache_win_v.reshape(cache_win_v.shape[:3] + (KV_WIDTH,))
    xp = x_prompt
    xs = jnp.transpose(x_sample, (1, 0, 2)).reshape(1, t_new * n_seq, D_MODEL)
    pk, pv, pc, ph = [], [], [], []
    sk, sv, sc, sh = [], [], [], []
    for l in range(depth):
        c = _layer_consts(l, p)
        xp, k1, v1, c1, h1 = _prompt_layer(xp, c, shared, tm=tm)
        prev = sh if l == depth - 1 else []
        xs, k2, v2, c2, h2 = _sample_layer(xs, cache_k, cache_v, state_conv[l], state_ssm, l, prev,
                                           c, shared, n_seq=n_seq, t_new=t_new)
        pk.append(k1); pv.append(v1); pc.append(c1); ph.append(h1)
        sk.append(k2); sv.append(v2); sc.append(c2); sh.append(h2)
    ys = jnp.transpose(xs.reshape(t_new, n_seq, D_MODEL), (1, 0, 2))
    return (xp, ys, jnp.stack(pk), jnp.stack(pv), jnp.stack(pc), jnp.stack(ph),
            jnp.stack(sk), jnp.stack(sv), jnp.stack(sc), sh[-1])
```

```python
import functools
import math

import numpy as np
import jax
import jax.numpy as jnp
from jax import lax
from jax.experimental import pallas as pl
from jax.experimental.pallas import tpu as pltpu

F32 = jnp.float32
BF16 = jnp.bfloat16

D_MODEL = 1024
HEAD_DIM = 64
N_Q_HEADS = 8
N_KV_HEADS = 2
ATTN_WIDTH = N_Q_HEADS * HEAD_DIM
KV_WIDTH = N_KV_HEADS * HEAD_DIM
WINDOW = 128
ROPE_THETA = 10000.0
ATTN_SCALE = HEAD_DIM ** -0.5
SSM_HEAD_DIM = 64
N_SSM_HEADS = 16
SSM_WIDTH = N_SSM_HEADS * SSM_HEAD_DIM
N_SSM_GROUPS = 2
D_STATE = 128
CONV_W = 4
CONV_DIM = SSM_WIDTH + 2 * N_SSM_GROUPS * D_STATE
SSD_CHUNK = 128
D_FF = 2816
EPS = 1e-6
LOG2E = math.log2(math.e)
PAST_LEN = 16384

LANES = 128
SUBLANES = 8

COL_Q = 0
COL_K = COL_Q + ATTN_WIDTH
COL_V = COL_K + KV_WIDTH
COL_Z = COL_V + KV_WIDTH
COL_XBC = COL_Z + SSM_WIDTH
COL_DT = COL_XBC + CONV_DIM
IN_PROJ_WIDTH = COL_DT + N_SSM_HEADS
PROJ_PAD = COL_DT + LANES
QKV_COLS = COL_Z

VMEM_LIMIT = 56 * 1024 * 1024
FF_CHUNKS = (6 * 256, 5 * 256)


def _dot(a, b):
    return jnp.dot(a, b, preferred_element_type=F32)


def _dot_nt(a, b):
    return lax.dot_general(a, b, (((1,), (1,)), ((), ())), preferred_element_type=F32)


def _silu(x):
    return x / (1.0 + jnp.exp(-x))


def _softplus(x):
    return jnp.maximum(x, 0.0) + jnp.log1p(jnp.exp(-jnp.abs(x)))


def _rms_rows(x, g):
    return x * lax.rsqrt(jnp.mean(x * x, axis=-1, keepdims=True) + EPS) * g


def _split2(x):
    hi = x.astype(BF16)
    lo = (x - hi.astype(F32)).astype(BF16)
    return hi, lo


def _split3(x):
    hi = x.astype(BF16)
    r1 = x - hi.astype(F32)
    mid = r1.astype(BF16)
    lo = (r1 - mid.astype(F32)).astype(BF16)
    return hi, mid, lo


def _head_norm_rope(x, m128, g, cos, sin_signed, low_half):
    hi, lo = _split2(x * x)
    ms = _dot(jnp.concatenate([hi, lo], axis=1), m128)
    xn = x * lax.rsqrt(ms + EPS) * g
    rot = jnp.where(low_half, pltpu.roll(xn, LANES - HEAD_DIM // 2, 1),
                    pltpu.roll(xn, HEAD_DIM // 2, 1))
    return xn * cos + rot * sin_signed


def _inproj_kernel(*refs, tm, carry_rows, shift, n_keep, has_state, n_parts):
    if has_state:
        (x_ref, gmix_ref, win_ref, m128_ref, qn_ref, kn_ref, cos_ref, sin_ref,
         convw_ref, convb_ref, dtb_ref, conv0_ref,
         q_ref, k2_ref, v2_ref, z_ref, xs_ref, bc_ref, dt_ref, knew_ref, vnew_ref, convst_ref,
         xbc_s) = refs
    else:
        (x_ref, gmix_ref, win_ref, m128_ref, qn_ref, kn_ref, cos_ref, sin_ref,
         convw_ref, convb_ref, dtb_ref,
         q_ref, k2_ref, v2_ref, z_ref, xs_ref, bc_ref, dt_ref, knew_ref, vnew_ref, convst_ref,
         xbc_s) = refs
        conv0_ref = None
    l = pl.program_id(1)

    @pl.when(l == 0)
    def _():
        if has_state:
            xbc_s[0:carry_rows, :] = conv0_ref[...]
        else:
            xbc_s[0:carry_rows, :] = jnp.zeros((carry_rows, CONV_DIM), F32)

    m128 = m128_ref[...]
    cw = 512
    nr = tm // n_parts
    lane = lax.broadcasted_iota(jnp.int32, (nr, LANES), 1)
    low_half = (lane % HEAD_DIM) < (HEAD_DIM // 2)
    for part in range(n_parts):
        r0 = part * nr
        rs = slice(r0, r0 + nr)
        base = carry_rows + r0
        u = _rms_rows(x_ref[0, rs, :], gmix_ref[...]).astype(BF16)
        for j in range(CONV_DIM // cw):
            c0 = COL_XBC + j * cw
            xbc_s[base:base + nr, j * cw:(j + 1) * cw] = _dot(u, win_ref[0, :, c0:c0 + cw])
        z_ref[0, rs, :] = _dot(u, win_ref[0, :, COL_Z:COL_Z + SSM_WIDTH])
        heads = _dot(u, win_ref[0, :, 0:QKV_COLS])
        dt_raw = _dot(u, win_ref[0, :, COL_DT:COL_DT + LANES])
        cos = cos_ref[rs, :]
        sin = sin_ref[rs, :]

        for j in range(ATTN_WIDTH // LANES):
            qj = heads[:, COL_Q + j * LANES:COL_Q + (j + 1) * LANES]
            qj = _head_norm_rope(qj, m128, qn_ref[...], cos, sin, low_half)
            q_ref[0, rs, j * LANES:(j + 1) * LANES] = (qj * (ATTN_SCALE * LOG2E)).astype(BF16)

        k = _head_norm_rope(heads[:, COL_K:COL_K + KV_WIDTH], m128, kn_ref[...], cos, sin, low_half)
        v = heads[:, COL_V:COL_V + KV_WIDTH]
        k2_ref[0, rs, 0:LANES] = k.astype(BF16)
        k2_ref[0, rs, LANES:2 * LANES] = pltpu.roll(k, HEAD_DIM, 1).astype(BF16)
        v2_ref[0, rs, 0:LANES] = v.astype(BF16)
        v2_ref[0, rs, LANES:2 * LANES] = pltpu.roll(v, HEAD_DIM, 1).astype(BF16)
        if part == n_parts - 1:
            knew_ref[0] = k[nr - n_keep:, :]
            vnew_ref[0] = v[nr - n_keep:, :]

        dt_ref[0, rs, :] = _softplus(dt_raw + dtb_ref[...])

        for j in range(CONV_DIM // cw):
            cs = slice(j * cw, (j + 1) * cw)
            w = [convw_ref[i:i + 1, cs] for i in range(CONV_W)]
            bias = convb_ref[:, cs]
            if shift % SUBLANES == 0:
                acc = bias
                for i in range(CONV_W):
                    off = base - (CONV_W - 1 - i) * shift
                    acc = acc + xbc_s[off:off + nr, cs] * w[i]
                act = _silu(acc)
            else:
                cur = xbc_s[base:base + nr, cs]
                acc = cur * w[0]
                for i in range(1, CONV_W):
                    acc = pltpu.roll(acc, shift, 0) + cur * w[i]
                head = bias
                for i in range(CONV_W):
                    off = base - (CONV_W - 1 - i) * shift
                    head = head + xbc_s[off:off + SUBLANES, cs] * w[i]
                act = _silu(jnp.concatenate([head, acc[SUBLANES:, :] + bias], axis=0))
            if j * cw < SSM_WIDTH:
                xs_ref[0, rs, cs] = act
            else:
                bc_ref[0, rs, :] = act.astype(BF16)

    convst_ref[0] = xbc_s[tm:tm + carry_rows, :]
    xbc_s[0:carry_rows, :] = xbc_s[tm:tm + carry_rows, :]


def _inproj(x, gmix, win, m128, qn, kn, cos, sin, convw, convb, dtb, conv0, *, layer, tm, carry_rows, shift,
            n_keep):
    nb, seq, _ = x.shape
    n_l = seq // tm
    n_parts = 1
    has_state = conv0 is not None
    const = lambda shape: pl.BlockSpec(shape, lambda b, l: (0,) * len(shape))
    row = lambda w: pl.BlockSpec((1, tm, w), lambda b, l: (b, l, 0))
    in_specs = [row(D_MODEL), const((1, D_MODEL)),
                pl.BlockSpec((1, D_MODEL, PROJ_PAD), lambda b, l: (layer, 0, 0)), const((2 * LANES, LANES)),
                const((1, LANES)), const((1, LANES)),
                pl.BlockSpec((tm, LANES), lambda b, l: (l, 0)), pl.BlockSpec((tm, LANES), lambda b, l: (l, 0)),
                const((CONV_W, CONV_DIM)), const((1, CONV_DIM)), const((1, LANES))]
    args = [x, gmix, win, m128, qn, kn, cos, sin, convw, convb, dtb]
    if has_state:
        in_specs.append(const((carry_rows, CONV_DIM)))
        args.append(conv0)
    last = lambda rows, w: pl.BlockSpec((1, rows, w), lambda b, l: (b, 0, 0))
    out_specs = [row(ATTN_WIDTH), row(2 * KV_WIDTH), row(2 * KV_WIDTH), row(SSM_WIDTH), row(SSM_WIDTH),
                 row(CONV_DIM - SSM_WIDTH), row(LANES),
                 last(n_keep, KV_WIDTH), last(n_keep, KV_WIDTH), last(carry_rows, CONV_DIM)]
    out_shape = [jax.ShapeDtypeStruct((nb, seq, ATTN_WIDTH), BF16),
                 jax.ShapeDtypeStruct((nb, seq, 2 * KV_WIDTH), BF16),
                 jax.ShapeDtypeStruct((nb, seq, 2 * KV_WIDTH), BF16),
                 jax.ShapeDtypeStruct((nb, seq, SSM_WIDTH), F32),
                 jax.ShapeDtypeStruct((nb, seq, SSM_WIDTH), F32),
                 jax.ShapeDtypeStruct((nb, seq, CONV_DIM - SSM_WIDTH), BF16),
                 jax.ShapeDtypeStruct((nb, seq, LANES), F32),
                 jax.ShapeDtypeStruct((nb, n_keep, KV_WIDTH), F32),
                 jax.ShapeDtypeStruct((nb, n_keep, KV_WIDTH), F32),
                 jax.ShapeDtypeStruct((nb, carry_rows, CONV_DIM), F32)]
    kern = functools.partial(_inproj_kernel, tm=tm, carry_rows=carry_rows, shift=shift, n_keep=n_keep,
                             has_state=has_state, n_parts=n_parts)
    return pl.pallas_call(
        kern, grid=(nb, n_l), in_specs=in_specs, out_specs=out_specs, out_shape=out_shape,
        scratch_shapes=[pltpu.VMEM((carry_rows + tm, CONV_DIM), F32)],
        compiler_params=pltpu.CompilerParams(dimension_semantics=("arbitrary", "arbitrary"),
                                             vmem_limit_bytes=VMEM_LIMIT),
        name="inproj",
    )(*args)


def _attn_kernel(sinks_ref, q_ref, kc_ref, kp_ref, vc_ref, vp_ref, o_ref, *, tq):
    i = pl.program_id(1)
    nblk = tq // WINDOW
    lane = lax.broadcasted_iota(jnp.int32, (WINDOW, LANES), 1)
    low = lane < HEAD_DIM
    qi = lax.broadcasted_iota(jnp.int32, (WINDOW, 2 * WINDOW), 0)
    kj = lax.broadcasted_iota(jnp.int32, (WINDOW, 2 * WINDOW), 1)
    diff = qi + WINDOW - kj
    band = (diff >= 0) & (diff < WINDOW)
    zero_bf = jnp.zeros((WINDOW, LANES), BF16)
    for blk in range(nblk):
        r0 = blk * WINDOW
        if blk == 0:
            k_prev, v_prev = kp_ref[0], vp_ref[0]
            mask = band & ((i > 0) | (kj >= WINDOW))
        else:
            k_prev = kc_ref[0, r0 - WINDOW:r0, :]
            v_prev = vc_ref[0, r0 - WINDOW:r0, :]
            mask = band
        kk = jnp.concatenate([k_prev, kc_ref[0, r0:r0 + WINDOW, :]], axis=0)
        vv = jnp.concatenate([v_prev, vc_ref[0, r0:r0 + WINDOW, :]], axis=0)
        for pair in range(N_Q_HEADS // 2):
            g = (2 * pair) // (N_Q_HEADS // N_KV_HEADS)
            qp = q_ref[0, r0:r0 + WINDOW, pair * LANES:(pair + 1) * LANES]
            outs = []
            for e in range(2):
                h = 2 * pair + e
                var = 0 if g == e else 1
                qh = jnp.where(low if e == 0 else ~low, qp, zero_bf)
                s = _dot_nt(qh, kk[:, var * LANES:(var + 1) * LANES])
                s = jnp.where(mask, s, -jnp.inf)
                sk = sinks_ref[h] * LOG2E
                m = jnp.maximum(jnp.max(s, axis=-1, keepdims=True), sk)
                p = jnp.exp2(s - m)
                den = jnp.sum(p, axis=-1, keepdims=True) + jnp.exp2(sk - m)
                o = _dot(p.astype(BF16), vv[:, var * LANES:(var + 1) * LANES])
                outs.append(o / den)
            o_ref[0, r0:r0 + WINDOW, pair * LANES:(pair + 1) * LANES] = (
                jnp.where(low, outs[0], outs[1]).astype(BF16))


def _attention(sinks, q, k2, v2, *, tq):
    nb, seq, _ = q.shape
    r = tq // WINDOW
    cur = lambda w: pl.BlockSpec((1, tq, w), lambda b, i: (b, i, 0))
    prev = lambda w: pl.BlockSpec((1, WINDOW, w), lambda b, i: (b, jnp.maximum(i * r - 1, 0), 0))
    return pl.pallas_call(
        functools.partial(_attn_kernel, tq=tq),
        grid=(nb, seq // tq),
        in_specs=[pl.BlockSpec(memory_space=pltpu.SMEM),
                  cur(ATTN_WIDTH), cur(2 * KV_WIDTH), prev(2 * KV_WIDTH), cur(2 * KV_WIDTH), prev(2 * KV_WIDTH)],
        out_specs=cur(ATTN_WIDTH),
        out_shape=jax.ShapeDtypeStruct((nb, seq, ATTN_WIDTH), BF16),
        compiler_params=pltpu.CompilerParams(dimension_semantics=("arbitrary", "arbitrary"),
                                             vmem_limit_bytes=VMEM_LIMIT),
        name="swa_attn",
    )(sinks, q, k2, k2, v2, v2)


def _ssd_kernel(xs_ref, bc_ref, dt_ref, a_ref, dskip_ref, tri_ref,
                y_ref, hT_ref, h_s, *, tc):
    l = pl.program_id(1)
    n_l = pl.num_programs(1)
    gw = SSM_WIDTH // N_SSM_GROUPS
    hpg = N_SSM_HEADS // N_SSM_GROUPS

    @pl.when(l == 0)
    def _():
        h_s[...] = jnp.zeros(h_s.shape, F32)

    lane = lax.broadcasted_iota(jnp.int32, (SSD_CHUNK, LANES), 1)
    low = lane < SSM_HEAD_DIM
    qi = lax.broadcasted_iota(jnp.int32, (SSD_CHUNK, SSD_CHUNK), 0)
    sj = lax.broadcasted_iota(jnp.int32, (SSD_CHUNK, SSD_CHUNK), 1)
    causal = sj <= qi
    a_neg = -jnp.exp(a_ref[...])
    tri = tri_ref[...]
    zero_bf = jnp.zeros((SSD_CHUNK, LANES), BF16)

    for c in range(tc // SSD_CHUNK):
        rows = slice(c * SSD_CHUNK, (c + 1) * SSD_CHUNK)
        dt = dt_ref[0, rows, :]
        hi, mid, lo = _split3(dt * a_neg)
        acum = (_dot(tri, hi) + _dot(tri, mid) + _dot(tri, lo)) * LOG2E
        acum_t = acum.T
        dt_t = dt.T
        a_end_t = acum_t[:, SSD_CHUNK - 1:SSD_CHUNK]
        w_end_t = dt_t * jnp.exp2(a_end_t - acum_t)
        dec_t = jnp.exp2(a_end_t)
        dec_t = jnp.broadcast_to(dec_t, (LANES, LANES))
        row_t = acum_t - jnp.log2(dt_t)
        for g in range(N_SSM_GROUPS):
            b_g = bc_ref[0, rows, g * D_STATE:(g + 1) * D_STATE]
            c_g = bc_ref[0, rows, (N_SSM_GROUPS + g) * D_STATE:(N_SSM_GROUPS + g + 1) * D_STATE]
            cb = _dot_nt(c_g, b_g)
            b_gt = b_g.astype(F32).T
            y_inter = _dot(c_g, h_s[g].astype(BF16))
            for pr in range(hpg // 2):
                lhs_y, lhs_s, escale, dsc = [], [], [], []
                for e in range(2):
                    h = g * hpg + 2 * pr + e
                    col = jnp.broadcast_to(acum[:, h:h + 1], (SSD_CHUNK, SSD_CHUNK))
                    decay_dt = jnp.where(causal, jnp.exp2(col - row_t[h:h + 1, :]), 0.0)
                    lhs_y.append((cb * decay_dt).astype(BF16))
                    lhs_s.append((b_gt * w_end_t[h:h + 1, :]).astype(BF16))
                    escale.append(jnp.exp2(col))
                    dsc.append(dec_t[h:h + 1, :])
                c0 = g * gw + pr * LANES
                x_pair = xs_ref[0, rows, c0:c0 + LANES]
                x_bf = x_pair.astype(BF16)
                rhs = jnp.concatenate([jnp.where(low, x_bf, zero_bf), jnp.where(low, zero_bf, x_bf)], axis=0)
                y_in = _dot(jnp.concatenate(lhs_y, axis=1), rhs)
                st = _dot(jnp.concatenate(lhs_s, axis=1), rhs)
                lc = pr * LANES
                y = y_in + y_inter[:, lc:lc + LANES] * jnp.where(low, escale[0], escale[1])
                y_ref[0, rows, c0:c0 + LANES] = y + dskip_ref[:, c0:c0 + LANES] * x_pair
                h_s[g, :, lc:lc + LANES] = h_s[g, :, lc:lc + LANES] * jnp.where(low, dsc[0], dsc[1]) + st

    @pl.when(l == n_l - 1)
    def _():
        hT_ref[0] = h_s[...]


def _ssd(xs, bc, dt, a_pad, dskip, tri, *, tc):
    nb, seq, _ = xs.shape
    row = lambda w: pl.BlockSpec((1, tc, w), lambda b, l: (b, l, 0))
    const = lambda shape: pl.BlockSpec(shape, lambda b, l: (0,) * len(shape))
    gw = SSM_WIDTH // N_SSM_GROUPS
    return pl.pallas_call(
        functools.partial(_ssd_kernel, tc=tc),
        grid=(nb, seq // tc),
        in_specs=[row(SSM_WIDTH), row(CONV_DIM - SSM_WIDTH), row(LANES),
                  const((1, LANES)), const((1, SSM_WIDTH)), const((SSD_CHUNK, SSD_CHUNK))],
        out_specs=[row(SSM_WIDTH),
                   pl.BlockSpec((1, N_SSM_GROUPS, D_STATE, gw), lambda b, l: (b, 0, 0, 0))],
        out_shape=[jax.ShapeDtypeStruct((nb, seq, SSM_WIDTH), F32),
                   jax.ShapeDtypeStruct((nb, N_SSM_GROUPS, D_STATE, gw), F32)],
        scratch_shapes=[pltpu.VMEM((N_SSM_GROUPS, D_STATE, gw), F32)],
        compiler_params=pltpu.CompilerParams(dimension_semantics=("arbitrary", "arbitrary"),
                                             vmem_limit_bytes=VMEM_LIMIT),
        name="ssd_scan",
    )(xs, bc, dt, a_pad, dskip, tri)


def _mix_kernel(*refs, tq):
    sinks_ref, q_ref, kc_ref, kp_ref, vc_ref, vp_ref, xs_ref, bc_ref, dt_ref, a_ref, dskip_ref, tri_ref = refs[:12]
    o_ref, y_ref, ht_ref, h_s = refs[12:]
    _attn_kernel(sinks_ref, q_ref, kc_ref, kp_ref, vc_ref, vp_ref, o_ref, tq=tq)
    _ssd_kernel(xs_ref, bc_ref, dt_ref, a_ref, dskip_ref, tri_ref, y_ref, ht_ref, h_s, tc=tq)


def _mix(sinks, q, k2, v2, xs, bc, dt, a_pad, dskip, tri, *, tq):
    nb, seq, _ = q.shape
    r = tq // WINDOW
    gw = SSM_WIDTH // N_SSM_GROUPS
    cur = lambda w: pl.BlockSpec((1, tq, w), lambda b, i: (b, i, 0))
    prev = lambda w: pl.BlockSpec((1, WINDOW, w), lambda b, i: (b, jnp.maximum(i * r - 1, 0), 0))
    const = lambda shape: pl.BlockSpec(shape, lambda b, i: (0,) * len(shape))
    return pl.pallas_call(
        functools.partial(_mix_kernel, tq=tq),
        grid=(nb, seq // tq),
        in_specs=[pl.BlockSpec(memory_space=pltpu.SMEM),
                  cur(ATTN_WIDTH), cur(2 * KV_WIDTH), prev(2 * KV_WIDTH), cur(2 * KV_WIDTH), prev(2 * KV_WIDTH),
                  cur(SSM_WIDTH), cur(CONV_DIM - SSM_WIDTH), cur(LANES),
                  const((1, LANES)), const((1, SSM_WIDTH)), const((SSD_CHUNK, SSD_CHUNK))],
        out_specs=[cur(ATTN_WIDTH), cur(SSM_WIDTH),
                   pl.BlockSpec((1, N_SSM_GROUPS, D_STATE, gw), lambda b, i: (b, 0, 0, 0))],
        out_shape=[jax.ShapeDtypeStruct((nb, seq, ATTN_WIDTH), BF16),
                   jax.ShapeDtypeStruct((nb, seq, SSM_WIDTH), F32),
                   jax.ShapeDtypeStruct((nb, N_SSM_GROUPS, D_STATE, gw), F32)],
        scratch_shapes=[pltpu.VMEM((N_SSM_GROUPS, D_STATE, gw), F32)],
        compiler_params=pltpu.CompilerParams(dimension_semantics=("arbitrary", "arbitrary"),
                                             vmem_limit_bytes=VMEM_LIMIT),
        name="attn_ssd",
    )(sinks, q, k2, k2, v2, v2, xs, bc, dt, a_pad, dskip, tri)


def _outffn_kernel(*refs, ffc, gated):
    if gated:
        x_ref, oa_ref, ys_ref, z_ref, gnorm_ref, wo_ref, gffn_ref, wgu_ref, wd_ref, o_ref = refs
        gw = SSM_WIDTH // N_SSM_GROUPS
        parts = []
        for g in range(N_SSM_GROUPS):
            ls = slice(g * gw, (g + 1) * gw)
            yg = ys_ref[:, ls] * _silu(z_ref[:, ls])
            yg = yg * lax.rsqrt(jnp.mean(yg * yg, axis=-1, keepdims=True) + EPS)
            parts.append((yg * gnorm_ref[:, ls]).astype(BF16))
        ys = jnp.concatenate(parts, axis=1)
    else:
        x_ref, oa_ref, ys_ref, wo_ref, gffn_ref, wgu_ref, wd_ref, o_ref = refs
        ys = ys_ref[...]
    x = x_ref[...]
    xm = x + _dot(oa_ref[...], wo_ref[0, 0:ATTN_WIDTH, :]) + _dot(ys, wo_ref[0, ATTN_WIDTH:, :])
    hn = _rms_rows(xm, gffn_ref[...]).astype(BF16)
    acc = jnp.zeros_like(xm)
    c0 = 0
    for width in ffc:
        g = _dot(hn, wgu_ref[0, :, c0:c0 + width])
        up = _dot(hn, wgu_ref[0, :, D_FF + c0:D_FF + c0 + width])
        acc = acc + _dot((_silu(g) * up).astype(BF16), wd_ref[0, c0:c0 + width, :])
        c0 += width
    o_ref[...] = xm + acc


def _outffn(x, oa, ys, gate, wo, gffn, wgu, wd, *, layer, tm, ffc):
    rows = x.shape[0]
    row = lambda w: pl.BlockSpec((tm, w), lambda i: (i, 0))
    const = lambda shape: pl.BlockSpec(shape, lambda i: (0,) * len(shape), pipeline_mode=pl.Buffered(1))
    gated = gate is not None
    in_specs = [row(D_MODEL), row(ATTN_WIDTH), row(SSM_WIDTH)]
    args = [x, oa, ys]
    if gated:
        in_specs += [row(SSM_WIDTH), const((1, SSM_WIDTH))]
        args += list(gate)
    wblk = lambda r, w: pl.BlockSpec((1, r, w), lambda i: (layer, 0, 0), pipeline_mode=pl.Buffered(1))
    in_specs += [wblk(ATTN_WIDTH + SSM_WIDTH, D_MODEL), const((1, D_MODEL)),
                 wblk(D_MODEL, 2 * D_FF), wblk(D_FF, D_MODEL)]
    args += [wo, gffn, wgu, wd]
    return pl.pallas_call(
        functools.partial(_outffn_kernel, ffc=ffc, gated=gated),
        grid=(rows // tm,),
        in_specs=in_specs,
        out_specs=row(D_MODEL),
        out_shape=jax.ShapeDtypeStruct((rows, D_MODEL), F32),
        compiler_params=pltpu.CompilerParams(dimension_semantics=("arbitrary",),
                                             vmem_limit_bytes=VMEM_LIMIT),
        name="outproj_ffn",
    )(*args)


T_PAD = SUBLANES


def _attn_dec_kernel(q_ref, kc_ref, kn_ref, vc_ref, vn_ref, sink_ref, o_ref, wk_ref, wv_ref, *, t_new):
    q = q_ref[...]
    bb, nq, _ = q.shape
    qpk = N_Q_HEADS // N_KV_HEADS
    n_c = kc_ref.shape[2]
    kc, vc, kn, vn = kc_ref[0], vc_ref[0], kn_ref[...], vn_ref[...]
    s_c = jnp.einsum('bqd,bkd->bqk', q, kc.astype(BF16), preferred_element_type=F32)
    s_n = jnp.einsum('bqd,bkd->bqk', q, kn.astype(BF16), preferred_element_type=F32)
    t_c = (lax.broadcasted_iota(jnp.int32, (bb, nq, n_c), 1) % (t_new * qpk)) // qpk
    j_c = lax.broadcasted_iota(jnp.int32, (bb, nq, n_c), 2)
    s_c = jnp.where(j_c > t_c + (n_c - WINDOW), s_c, -jnp.inf)
    t_n = (lax.broadcasted_iota(jnp.int32, (bb, nq, T_PAD), 1) % (t_new * qpk)) // qpk
    j_n = lax.broadcasted_iota(jnp.int32, (bb, nq, T_PAD), 2)
    s_n = jnp.where((j_n <= t_n) & (j_n < t_new), s_n, -jnp.inf)
    sk = sink_ref[...] * LOG2E
    m = jnp.maximum(jnp.maximum(jnp.max(s_c, axis=-1, keepdims=True), jnp.max(s_n, axis=-1, keepdims=True)), sk)
    p_c = jnp.exp2(s_c - m)
    p_n = jnp.exp2(s_n - m)
    den = jnp.sum(p_c, axis=-1, keepdims=True) + jnp.sum(p_n, axis=-1, keepdims=True) + jnp.exp2(sk - m)
    o = (jnp.einsum('bqk,bkd->bqd', p_c.astype(BF16), vc.astype(BF16), preferred_element_type=F32)
         + jnp.einsum('bqk,bkd->bqd', p_n.astype(BF16), vn.astype(BF16), preferred_element_type=F32))
    o = o / den
    rows_g = nq // N_KV_HEADS
    lane = lax.broadcasted_iota(jnp.int32, (bb, rows_g, LANES), 2)
    merged = o[:, 0:rows_g, :]
    for g in range(1, N_KV_HEADS):
        merged = jnp.where(lane // HEAD_DIM == g, o[:, g * rows_g:(g + 1) * rows_g, :], merged)
    o_ref[...] = merged.astype(BF16)

    sub = lax.broadcasted_iota(jnp.int32, (bb, T_PAD, LANES), 1)
    for c, n, w_ref in ((kc, kn, wk_ref), (vc, vn, wv_ref)):
        shifted = pltpu.roll(c.reshape(bb * n_c, LANES), bb * n_c - t_new, 0).reshape(bb, n_c, LANES)
        tail = pltpu.roll(n.reshape(bb * T_PAD, LANES), T_PAD - t_new, 0).reshape(bb, T_PAD, LANES)
        w_ref[:, 0:n_c - T_PAD, :] = shifted[:, 0:n_c - T_PAD, :]
        w_ref[:, n_c - T_PAD:, :] = jnp.where(sub < T_PAD - t_new, shifted[:, n_c - T_PAD:, :], tail)


def _attention_dec(q, kc, kn, vc, vn, sink_rows, *, layer, t_new, bb):
    n_seq, nq, w = q.shape
    n_c = kc.shape[2]
    blk = lambda r: pl.BlockSpec((bb, r, w), lambda i: (i, 0, 0))
    cache = pl.BlockSpec((1, bb, n_c, w), lambda i: (layer, i, 0, 0))
    win = jax.ShapeDtypeStruct((n_seq, n_c, w), F32)
    return pl.pallas_call(
        functools.partial(_attn_dec_kernel, t_new=t_new),
        grid=(n_seq // bb,),
        in_specs=[blk(nq), cache, blk(T_PAD), cache, blk(T_PAD), pl.BlockSpec((nq, 1), lambda i: (0, 0))],
        out_specs=[blk(nq // N_KV_HEADS), blk(n_c), blk(n_c)],
        out_shape=[jax.ShapeDtypeStruct((n_seq, nq // N_KV_HEADS, w), BF16), win, win],
        compiler_params=pltpu.CompilerParams(dimension_semantics=("arbitrary",), vmem_limit_bytes=VMEM_LIMIT),
        name="swa_attn_decode",
    )(q, kc, kn, vc, vn, sink_rows)


def _ssd_dec_kernel(*refs, bb, t_new, n_prev):
    (xs_ref, bc_ref, dt_ref, z_ref, h0_ref) = refs[:5]
    prev_refs = refs[5:5 + n_prev]
    (a_ref, dskip_ref, gnorm_ref, tri_ref, ones_ref, exp_ref,
     y_ref, h_ref, aw_s, xdt_s, xw_s, ea_s, dec_s) = refs[5 + n_prev:]
    d = pl.program_id(0)
    gw = SSM_WIDTH // N_SSM_GROUPS
    hpg = N_SSM_HEADS // N_SSM_GROUPS
    rows = bb * T_PAD

    for j in range(n_prev):
        @pl.when(d == j)
        def _(j=j):
            h_ref[0] = prev_refs[j][0]

    @pl.when(d == n_prev)
    def _():
        expand = exp_ref[...]

        def widen(v):
            hi, mid, lo = _split3(v)
            return _dot(hi, expand) + _dot(mid, expand) + _dot(lo, expand)

        def seq_sum(m, v):
            hi, mid, lo = _split3(v)
            return _dot(m, hi) + _dot(m, mid) + _dot(m, lo)

        dt = dt_ref[...].reshape(rows, LANES)
        dta = dt * (-jnp.exp(a_ref[...]))
        a_w = widen(seq_sum(tri_ref[...], dta))
        a_end = seq_sum(ones_ref[...], dta)
        xdt = xs_ref[...].reshape(rows, SSM_WIDTH) * widen(dt)
        aw_s[...] = a_w
        xdt_s[...] = xdt
        xw_s[...] = xdt * jnp.exp(widen(a_end) - a_w)
        ea_s[...] = jnp.exp(a_w)
        dec_s[...] = jnp.exp(a_end)

        t_row = lax.broadcasted_iota(jnp.int32, (T_PAD, gw), 0)

        def seq_body(i, carry):
            r0 = pl.multiple_of(i * T_PAD, T_PAD)
            rs = pl.ds(r0, T_PAD)
            dec = jnp.broadcast_to(dec_s[pl.ds(r0, 1), :], (LANES, LANES)).T
            bc = bc_ref[i]
            for g in range(N_SSM_GROUPS):
                b_g = bc[:, g * D_STATE:(g + 1) * D_STATE]
                c_g = bc[:, (N_SSM_GROUPS + g) * D_STATE:(N_SSM_GROUPS + g + 1) * D_STATE]
                cb = _dot_nt(c_g, b_g)
                ls = slice(g * gw, (g + 1) * gw)
                a_g = aw_s[rs, ls]
                xdt_g = xdt_s[rs, ls]
                y = jnp.zeros((T_PAD, gw), F32)
                for s in range(t_new):
                    w = jnp.where(t_row >= s, jnp.exp(a_g - a_g[s:s + 1, :]), 0.0)
                    y = y + (w * cb[:, s:s + 1]) * xdt_g[s:s + 1, :]
                h0g = h0_ref[0, i, g * hpg:(g + 1) * hpg].reshape(gw, D_STATE)
                y = y + _dot_nt(c_g, h0g.astype(BF16)) * ea_s[rs, ls]
                y = y + dskip_ref[:, ls] * xs_ref[i, :, ls]
                y = y * _silu(z_ref[i, :, ls])
                y = y * lax.rsqrt(jnp.mean(y * y, axis=-1, keepdims=True) + EPS)
                y_ref[i, :, ls] = (y * gnorm_ref[:, ls]).astype(BF16)
                delta = lax.dot_general(xw_s[rs, ls].astype(BF16), b_g, (((0,), (0,)), ((), ())),
                                        preferred_element_type=F32)
                for hl in range(hpg):
                    h = g * hpg + hl
                    h_ref[0, i, h] = (h0_ref[0, i, h] * dec[h:h + 1, :]
                                      + delta[hl * SSM_HEAD_DIM:(hl + 1) * SSM_HEAD_DIM, :])
            return carry

        lax.fori_loop(0, bb, seq_body, 0, unroll=True)


def _ssd_dec(xs3, bc3, dt3, z3, state_all, layer, prev_states, a_pad, dskip, gnorm, tri_b, ones_b, expand,
             *, bb, t_new):
    nb = xs3.shape[0]
    nblk = nb // bb
    n_prev = len(prev_states)
    rows = bb * T_PAD

    def phase_block(j):
        return lambda d, i: jnp.clip(i + (d - j) * nblk, 0, nblk - 1)

    cur = phase_block(n_prev)
    blk = lambda w: pl.BlockSpec((bb, T_PAD, w), lambda d, i: (cur(d, i), 0, 0))
    const = lambda shape: pl.BlockSpec(shape, lambda d, i: (0,) * len(shape))
    st_shape = (1, bb, N_SSM_HEADS, SSM_HEAD_DIM, D_STATE)
    in_specs = [blk(SSM_WIDTH), blk(CONV_DIM - SSM_WIDTH), blk(LANES), blk(SSM_WIDTH),
                pl.BlockSpec(st_shape, lambda d, i: (layer, cur(d, i), 0, 0, 0))]
    for j in range(n_prev):
        in_specs.append(pl.BlockSpec(st_shape, lambda d, i, j=j: (0, phase_block(j)(d, i), 0, 0, 0)))
    in_specs += [const((1, LANES)), const((1, SSM_WIDTH)), const((1, SSM_WIDTH)),
                 const((rows, rows)), const((rows, rows)), const((LANES, SSM_WIDTH))]
    return pl.pallas_call(
        functools.partial(_ssd_dec_kernel, bb=bb, t_new=t_new, n_prev=n_prev),
        grid=(n_prev + 1, nblk),
        in_specs=in_specs,
        out_specs=[blk(SSM_WIDTH), pl.BlockSpec(st_shape, lambda d, i: (d, i, 0, 0, 0))],
        out_shape=[jax.ShapeDtypeStruct((nb, T_PAD, SSM_WIDTH), BF16),
                   jax.ShapeDtypeStruct((n_prev + 1,) + state_all.shape[1:], F32)],
        scratch_shapes=[pltpu.VMEM((rows, SSM_WIDTH), F32), pltpu.VMEM((rows, SSM_WIDTH), F32),
                        pltpu.VMEM((rows, SSM_WIDTH), F32), pltpu.VMEM((rows, SSM_WIDTH), F32),
                        pltpu.VMEM((rows, LANES), F32)],
        compiler_params=pltpu.CompilerParams(dimension_semantics=("arbitrary", "arbitrary"),
                                             vmem_limit_bytes=VMEM_LIMIT),
        name="ssd_decode",
    )(xs3, bc3, dt3, z3, state_all, *prev_states, a_pad, dskip, gnorm, tri_b, ones_b, expand)


def _rope_tables(pos):
    half = HEAD_DIM // 2
    inv = ROPE_THETA ** (-np.arange(half, dtype=np.float64) / half)
    ang = np.asarray(pos, np.float64)[:, None] * inv[None, :]
    cos = np.tile(np.cos(ang), (1, LANES // half))
    sin = np.sin(ang)
    sin_signed = np.tile(np.concatenate([-sin, sin], axis=1), (1, LANES // HEAD_DIM))
    return jnp.asarray(cos, F32), jnp.asarray(sin_signed, F32)


def _layer_consts(l, p):
    return dict(
        layer=l,
        gmix=p['norm_mix'][l][None, :],
        qn=jnp.tile(p['q_norm'][l], LANES // HEAD_DIM)[None, :],
        kn=jnp.tile(p['k_norm'][l], LANES // HEAD_DIM)[None, :],
        convw=p['conv_w'][l],
        convb=p['conv_b'][l][None, :],
        dtb=jnp.pad(p['dt_bias'][l], (0, LANES - N_SSM_HEADS))[None, :],
        a_pad=jnp.pad(p['a_log'][l], (0, LANES - N_SSM_HEADS))[None, :],
        dskip=jnp.repeat(p['d_skip'][l], SSM_HEAD_DIM)[None, :],
        gnorm=p['ssm_norm'][l][None, :],
        sinks=p['sinks'][l],
        gffn=p['norm_ffn'][l][None, :],
    )


def _state_from_t(h_t):
    nb = h_t.shape[0]
    hpg = N_SSM_HEADS // N_SSM_GROUPS
    h = h_t.reshape(nb, N_SSM_GROUPS, D_STATE, hpg, SSM_HEAD_DIM)
    return jnp.transpose(h, (0, 1, 3, 4, 2)).reshape(nb, N_SSM_HEADS, SSM_HEAD_DIM, D_STATE)


def _prompt_layer(x, c, shared, *, tm):
    nb, seq, _ = x.shape
    (q, k2, v2, z, xs, bc, dt, knew, vnew, convst) = _inproj(
        x, c['gmix'], shared['win'], shared['m128'], c['qn'], c['kn'], shared['cos_p'], shared['sin_p'],
        c['convw'], c['convb'], c['dtb'], None, layer=c['layer'], tm=tm, carry_rows=SUBLANES, shift=1, n_keep=min(WINDOW, seq))
    mix_rows = next(t for t in (4 * tm, 2 * tm, tm) if seq % t == 0)
    oa, ys, h_t = _mix(c['sinks'], q, k2, v2, xs, bc, dt, c['a_pad'], c['dskip'], shared['tri'], tq=mix_rows)
    rows = nb * seq
    xo = _outffn(x.reshape(rows, D_MODEL), oa.reshape(rows, ATTN_WIDTH), ys.reshape(rows, SSM_WIDTH),
                 (z.reshape(rows, SSM_WIDTH), c['gnorm']), shared['wo'], c['gffn'], shared['wgu'], shared['wd'],
                 layer=c['layer'], tm=tm,
                 ffc=FF_CHUNKS)
    n_keep = knew.shape[1]
    return (xo.reshape(nb, seq, D_MODEL),
            knew.reshape(nb, n_keep, N_KV_HEADS, HEAD_DIM), vnew.reshape(nb, n_keep, N_KV_HEADS, HEAD_DIM),
            convst[:, SUBLANES - (CONV_W - 1):, :], _state_from_t(h_t))


def _sample_layer(x_t, cache_k, cache_v, conv0, state_all, layer, prev_states, c, shared, *, n_seq, t_new):
    rows = t_new * n_seq
    qpk = N_Q_HEADS // N_KV_HEADS
    n_c = cache_k.shape[2]
    conv0_t = jnp.transpose(conv0, (1, 0, 2)).reshape((CONV_W - 1) * n_seq, CONV_DIM)
    (q, _, _, z, xs, bc, dt, knew, vnew, convst) = _inproj(
        x_t, c['gmix'], shared['win'], shared['m128'], c['qn'], c['kn'], shared['cos_s'], shared['sin_s'],
        c['convw'], c['convb'], c['dtb'], conv0_t, layer=c['layer'], tm=rows, carry_rows=(CONV_W - 1) * n_seq, shift=n_seq,
        n_keep=rows)

    def to_seq(a):
        a = jnp.transpose(a[0].reshape(t_new, n_seq, a.shape[-1]), (1, 0, 2))
        return jnp.pad(a, ((0, 0), (0, T_PAD - t_new), (0, 0)))

    q5 = jnp.transpose(q[0].reshape(t_new, n_seq, N_KV_HEADS, qpk, HEAD_DIM), (1, 2, 0, 3, 4))
    q5 = q5.reshape(n_seq, N_KV_HEADS, t_new * qpk, HEAD_DIM)
    q_m = jnp.concatenate(
        [jnp.pad(q5[:, g], ((0, 0), (0, 0), (g * HEAD_DIM, (N_KV_HEADS - 1 - g) * HEAD_DIM)))
         for g in range(N_KV_HEADS)], axis=1)
    sink_rows = jnp.tile(c['sinks'].reshape(N_KV_HEADS, 1, qpk), (1, t_new, 1)).reshape(-1, 1)
    o_m, win_k, win_v = _attention_dec(
        q_m, cache_k, to_seq(knew), cache_v, to_seq(vnew), sink_rows, layer=layer, t_new=t_new, bb=min(32, n_seq))
    o5 = o_m.reshape(n_seq, t_new, qpk, N_KV_HEADS, HEAD_DIM)
    oa = jnp.transpose(o5, (1, 0, 3, 2, 4)).reshape(rows, ATTN_WIDTH)
    win_k = win_k.reshape(n_seq, n_c, N_KV_HEADS, HEAD_DIM)
    win_v = win_v.reshape(n_seq, n_c, N_KV_HEADS, HEAD_DIM)

    y3, h_new = _ssd_dec(to_seq(xs), to_seq(bc), to_seq(dt), to_seq(z), state_all, layer, prev_states,
                         c['a_pad'], c['dskip'], c['gnorm'], shared['tri_b'], shared['ones_b'], shared['expand'],
                         bb=shared['bb_s'], t_new=t_new)
    ys = jnp.transpose(y3[:, :t_new], (1, 0, 2)).reshape(rows, SSM_WIDTH)

    xo = _outffn(x_t[0], oa, ys, None, shared['wo'], c['gffn'], shared['wgu'], shared['wd'], layer=c['layer'],
                 tm=min(512, rows), ffc=FF_CHUNKS)

    conv_new = jnp.transpose(convst[0].reshape(CONV_W - 1, n_seq, CONV_DIM), (1, 0, 2))
    return xo[None], win_k, win_v, conv_new, h_new


def kernel(x_prompt, x_sample, cache_win_k, cache_win_v, state_conv, state_ssm,
           norm_mix, w_in, q_norm, k_norm, sinks, conv_w, conv_b, dt_bias, a_log,
           d_skip, ssm_norm, w_out, norm_ffn, w_gate_up, w_down):
    p = dict(norm_mix=norm_mix, w_in=w_in, q_norm=q_norm, k_norm=k_norm, sinks=sinks, conv_w=conv_w,
             conv_b=conv_b, dt_bias=dt_bias, a_log=a_log, d_skip=d_skip, ssm_norm=ssm_norm, w_out=w_out,
             norm_ffn=norm_ffn, w_gate_up=w_gate_up, w_down=w_down)
    depth = w_in.shape[0]
    seq = x_prompt.shape[1]
    cos_p, sin_p = _rope_tables(np.arange(seq))
    half_blk = jnp.arange(LANES) // HEAD_DIM
    m128 = (jnp.where(half_blk[:, None] == half_blk[None, :], 1.0 / HEAD_DIM, 0.0)).astype(BF16)
    m128 = jnp.concatenate([m128, m128], axis=0)
    tri = (jnp.arange(SSD_CHUNK)[:, None] >= jnp.arange(SSD_CHUNK)[None, :]).astype(BF16)
    n_seq, t_new, _ = x_sample.shape
    pos_s = PAST_LEN + np.repeat(np.arange(t_new), n_seq)
    cos_s, sin_s = _rope_tables(pos_s)
    bb_s = min(8, n_seq)
    r_idx = jnp.arange(bb_s * T_PAD)
    same_seq = (r_idx[:, None] // T_PAD) == (r_idx[None, :] // T_PAD)
    tri_b = (same_seq & (r_idx[:, None] >= r_idx[None, :])).astype(BF16)
    ones_b = same_seq.astype(BF16)
    expand = (jnp.arange(LANES)[:, None] == (jnp.arange(SSM_WIDTH) // SSM_HEAD_DIM)[None, :]).astype(BF16)
    weights = dict(win=jnp.pad(w_in, ((0, 0), (0, 0), (0, PROJ_PAD - IN_PROJ_WIDTH))).astype(BF16),
                   wo=w_out.astype(BF16), wgu=w_gate_up.astype(BF16), wd=w_down.astype(BF16))
    shared = dict(cos_p=cos_p, sin_p=sin_p, cos_s=cos_s, sin_s=sin_s, m128=m128, tri=tri, tri_b=tri_b, **weights,
                  ones_b=ones_b, bb_s=bb_s, expand=expand)
    tm = min(512, seq)

    cache_k = cache_win_k.reshape(cache_win_k.shape[:3] + (KV_WIDTH,))
    cache_v = cache_win_v.reshape(cache_win_v.shape[:3] + (KV_WIDTH,))
    xp = x_prompt
    xs = jnp.transpose(x_sample, (1, 0, 2)).reshape(1, t_new * n_seq, D_MODEL)
    pk, pv, pc, ph = [], [], [], []
    sk, sv, sc, sh = [], [], [], []
    for l in range(depth):
        c = _layer_consts(l, p)
        xp, k1, v1, c1, h1 = _prompt_layer(xp, c, shared, tm=tm)
        prev = sh if l == depth - 1 else []
        xs, k2, v2, c2, h2 = _sample_layer(xs, cache_k, cache_v, state_conv[l], state_ssm, l, prev,
                                           c, shared, n_seq=n_seq, t_new=t_new)
        pk.append(k1); pv.append(v1); pc.append(c1); ph.append(h1)
        sk.append(k2); sv.append(v2); sc.append(c2); sh.append(h2)
    ys = jnp.transpose(xs.reshape(t_new, n_seq, D_MODEL), (1, 0, 2))
    return (xp, ys, jnp.stack(pk), jnp.stack(pv), jnp.stack(pc), jnp.stack(ph),
            jnp.stack(sk), jnp.stack(sv), jnp.stack(sc), sh[-1])
```
